```python
import math
import jax
import jax.numpy as jnp
from jax import lax
import numpy as np

D_MODEL = 1024
BATCH = 8
SEQ = 2048
DEPTH = 4
DEC_BATCH = 128
DEC_SEQ = 4
PAST_LEN = 16384
PAGE_SIZE = 128

HEAD = 64
W_A = D_MODEL // 4
H_A = W_A // HEAD
DK_A = HEAD
DV_A = W_A // H_A
W_B = (D_MODEL - W_A) // 2
H_B = 4
DV_B = W_B // H_B
DK_B = DV_B // 2
GLA_LR = 16
GLA_NORM = 16.0
W_C = D_MODEL - W_A - W_B
H_C = W_C // HEAD
N_C = HEAD
DECAY_LR = 32
AAA_LR = 32
GN_EPS = 64e-5
CHUNK = 16
TINY = 1e-30
A_SIZES = (H_A * DK_A, H_A * DK_A, W_A, W_A)
B_SIZES = (H_B * DK_B, H_B * DK_B, W_B, GLA_LR, W_B)
C_SHIFT_SIZES = (W_C, DECAY_LR, W_C, W_C, AAA_LR)
D_A_IN = 2 * H_A * DK_A + 2 * W_A
D_B_IN = 2 * H_B * DK_B + 2 * W_B + GLA_LR
D_SHIFT = 3 * W_C + DECAY_LR + AAA_LR
D_IN = D_A_IN + D_B_IN + D_SHIFT + W_C

kernel_name = 'hymba_style_hgrn2_gla_rwkv7_decode_step'


def _rmsnorm(x, w, eps=1e-6):
    xf = x.astype(jnp.float32)
    y = xf * lax.rsqrt(jnp.mean(xf * xf, axis=-1, keepdims=True) + eps)
    return (y * w.astype(jnp.float32)).astype(x.dtype)


def _heads(t, h):
    return t.reshape(t.shape[:-1] + (h, t.shape[-1] // h))


def _merge(t):
    return t.reshape(t.shape[:-2] + (t.shape[-2] * t.shape[-1],))


def _head_rmsnorm(o, w, eps=1e-5):
    y = o * lax.rsqrt(jnp.mean(o * o, axis=-1, keepdims=True) + eps)
    return _merge(y) * w.astype(jnp.float32)


def _split(t, sizes):
    return jnp.split(t, [int(s) for s in np.cumsum(sizes)[:-1]], axis=-1)


def _chunked_gla(q, k, v, g, s0):
    b, t, h, _ = q.shape
    dv = v.shape[-1]
    nc = -(-t // CHUNK)
    pad = nc * CHUNK - t

    def to_chunks(a):
        a = jnp.pad(a, ((0, 0), (0, pad), (0, 0), (0, 0)))
        return a.reshape(b, nc, CHUNK, h, a.shape[-1]).transpose(1, 0, 3, 2, 4)

    qc, kc, vc, gc = (to_chunks(a) for a in (q, k, v, g))
    G = jnp.cumsum(gc, axis=3)
    causal = jnp.tril(jnp.ones((CHUNK, CHUNK), dtype=bool))[:, :, None]

    def step(S, xs):
        qi, ki, vi, Gi = xs
        g_last = Gi[:, :, -1, :]
        diff = Gi[:, :, :, None, :] - Gi[:, :, None, :, :]
        decay = jnp.where(causal, jnp.exp(jnp.minimum(diff, 0.0)), 0.0)
        A = jnp.einsum('bhik,bhjk,bhijk->bhij', qi, ki, decay)
        o = (jnp.einsum('bhij,bhjv->bhiv', A, vi)
             + jnp.einsum('bhik,bhkv->bhiv', qi * jnp.exp(Gi), S))
        S = (jnp.exp(g_last)[..., None] * S
             + jnp.einsum('bhjk,bhjv->bhkv', ki * jnp.exp(g_last[:, :, None, :] - Gi), vi))
        return S, o

    s_final, o = lax.scan(step, s0, (qc, kc, vc, G))
    o = o.transpose(1, 0, 3, 2, 4).reshape(b, nc * CHUNK, h, dv)[:, :t]
    return o, s_final


def _rwkv7_scan(r, w, k, v, kk, a, s0):
    def step(S, xs):
        rt, wt, kt, vt, kkt, at = xs
        sab = jnp.einsum('bhvk,bhk->bhv', S, -kkt)
        S = (S * wt[:, :, None, :] + sab[..., None] * (kkt * at)[:, :, None, :]
             + vt[..., None] * kt[:, :, None, :])
        return S, jnp.einsum('bhvk,bhk->bhv', S, rt)

    xs = tuple(jnp.swapaxes(z, 0, 1) for z in (r, w, k, v, kk, a))
    s_final, y = lax.scan(step, s0, xs)
    return jnp.swapaxes(y, 0, 1), s_final


def _trunk(x, st_a, st_b, st_c, st_s, norm_w, w_in, hgrn_lb, hgrn_norm_w, gla_gk_w2, gla_gk_b,
           gla_norm_w, rwkv_mu, rwkv_w0, rwkv_w2, rwkv_a0, rwkv_a2, rwkv_k_k, rwkv_k_a, rwkv_r_k,
           rwkv_ln_w, rwkv_ln_b, w_out, final_norm_w):
    f32 = jnp.float32
    sm = jax.nn.softmax(hgrn_lb.astype(f32), axis=0)
    lbs = jnp.cumsum(sm, axis=0) - sm[0]
    new_a, new_b, new_c, new_s = [], [], [], []
    for l in range(DEPTH):
        h = _rmsnorm(x, norm_w[l])
        P = jnp.einsum('btd,de->bte', h, w_in[l]).astype(f32)
        pa, pb, pc, gate_c = jnp.split(P, [D_A_IN, D_A_IN + D_B_IN, D_A_IN + D_B_IN + D_SHIFT], axis=-1)

        qa, za, ia, gate_a = _split(pa, A_SIZES)
        lb = lbs[l]
        logsig = jax.nn.log_sigmoid(za)
        log_f = jnp.where(lb > 0.0,
                          jnp.logaddexp(jnp.log(jnp.maximum(lb, TINY)), jnp.log1p(-lb) + logsig),
                          logsig)
        k_a = (1.0 - lb) * jax.nn.sigmoid(-za)
        oa, sa = _chunked_gla(_heads(qa, H_A) * DK_A ** -0.5, _heads(k_a, H_A), _heads(ia, H_A),
                              _heads(log_f, H_A), st_a[l].astype(f32))
        ya = _head_rmsnorm(oa, hgrn_norm_w[l]) * jax.nn.silu(gate_a)

        qb, kb, vb, gkl, gate_b = _split(pb, B_SIZES)
        gk = jax.nn.log_sigmoid(gkl @ gla_gk_w2[l].astype(f32) + gla_gk_b[l].astype(f32)) / GLA_NORM
        ob, sb = _chunked_gla(_heads(qb, H_B) * DK_B ** -0.5, _heads(kb, H_B), _heads(vb, H_B),
                              _heads(gk, H_B), st_b[l].astype(f32))
        yb = _head_rmsnorm(ob, gla_norm_w[l]) * jax.nn.silu(gate_b)

        prev = jnp.concatenate([st_s[l][:, None].astype(f32), pc[:, :-1]], axis=1)
        xs = pc + (prev - pc) * rwkv_mu[l].astype(f32)
        r, wl, kc, vc, al = _split(xs, C_SHIFT_SIZES)
        w_log = -jax.nn.softplus(-(rwkv_w0[l].astype(f32) + jnp.tanh(wl) @ rwkv_w2[l].astype(f32))) - 0.5
        w = jnp.exp(-jnp.exp(w_log))
        a = jax.nn.sigmoid(rwkv_a0[l].astype(f32) + al @ rwkv_a2[l].astype(f32))
        kk = _heads(kc * rwkv_k_k[l].astype(f32), H_C)
        kk = kk * lax.rsqrt(jnp.maximum(jnp.sum(kk * kk, axis=-1, keepdims=True), 1e-24))
        kc = kc * (1.0 + (a - 1.0) * rwkv_k_a[l].astype(f32))
        rh, kh, vh = _heads(r, H_C), _heads(kc, H_C), _heads(vc, H_C)
        oc, sc = _rwkv7_scan(rh, _heads(w, H_C), kh, vh, kk, _heads(a, H_C), st_c[l].astype(f32))
        mu = jnp.mean(oc, axis=-1, keepdims=True)
        var = jnp.mean(jnp.square(oc - mu), axis=-1, keepdims=True)
        oc = _merge((oc - mu) * lax.rsqrt(var + GN_EPS)) * rwkv_ln_w[l].astype(f32) + rwkv_ln_b[l].astype(f32)
        bonus = jnp.sum(rh * kh * _heads(rwkv_r_k[l].astype(f32), H_C), axis=-1, keepdims=True) * vh
        yc = (oc + _merge(bonus)) * jax.nn.silu(gate_c)

        y = jnp.concatenate([ya, yb, yc], axis=-1).astype(x.dtype)
        x = x + jnp.einsum('bte,ed->btd', y, w_out[l])
        new_a.append(sa)
        new_b.append(sb)
        new_c.append(sc)
        new_s.append(pc[:, -1])
    return (_rmsnorm(x, final_norm_w), jnp.stack(new_a), jnp.stack(new_b),
            jnp.stack(new_c), jnp.stack(new_s))


def setup_inputs(seed: int = 0) -> dict:
    key = jax.random.key(seed)
    ks = jax.random.split(key, 32)
    f32 = jnp.float32

    def nrm(k, shape, scale):
        return jax.random.normal(k, shape, f32) * scale

    return {
        'x_prompt': nrm(ks[0], (BATCH, SEQ, D_MODEL), 1.0),
        'x_sample': nrm(ks[1], (DEC_BATCH, DEC_SEQ, D_MODEL), 1.0),
        'state_hgrn': nrm(ks[2], (DEPTH, DEC_BATCH, H_A, DK_A, DV_A), 0.5),
        'state_gla': nrm(ks[3], (DEPTH, DEC_BATCH, H_B, DK_B, DV_B), 0.5),
        'state_rwkv': nrm(ks[4], (DEPTH, DEC_BATCH, H_C, N_C, N_C), 0.3),
        'state_shift': nrm(ks[5], (DEPTH, DEC_BATCH, D_SHIFT), 1.0),
        'norm_w': 1.0 + nrm(ks[6], (DEPTH, D_MODEL), 0.02),
        'w_in': nrm(ks[7], (DEPTH, D_MODEL, D_IN), D_MODEL ** -0.5),
        'hgrn_lb': nrm(ks[8], (DEPTH, H_A * DK_A), 0.5),
        'hgrn_norm_w': 1.0 + nrm(ks[9], (DEPTH, W_A), 0.02),
        'gla_gk_w2': nrm(ks[10], (DEPTH, GLA_LR, H_B * DK_B), GLA_LR ** -0.5),
        'gla_gk_b': nrm(ks[11], (DEPTH, H_B * DK_B), 0.1),
        'gla_norm_w': 1.0 + nrm(ks[12], (DEPTH, W_B), 0.02),
        'rwkv_mu': jax.random.uniform(ks[13], (DEPTH, D_SHIFT), f32),
        'rwkv_w0': nrm(ks[14], (DEPTH, W_C), 0.3),
        'rwkv_w2': nrm(ks[15], (DEPTH, DECAY_LR, W_C), 0.1),
        'rwkv_a0': nrm(ks[16], (DEPTH, W_C), 0.1),
        'rwkv_a2': nrm(ks[17], (DEPTH, AAA_LR, W_C), AAA_LR ** -0.5),
        'rwkv_k_k': 0.85 + nrm(ks[18], (DEPTH, W_C), 0.02),
        'rwkv_k_a': 1.0 + nrm(ks[19], (DEPTH, W_C), 0.02),
        'rwkv_r_k': nrm(ks[20], (DEPTH, W_C), 0.1),
        'rwkv_ln_w': 1.0 + nrm(ks[21], (DEPTH, W_C), 0.02),
        'rwkv_ln_b': nrm(ks[22], (DEPTH, W_C), 0.02),
        'w_out': nrm(ks[23], (DEPTH, D_MODEL, D_MODEL), D_MODEL ** -0.5),
        'final_norm_w': 1.0 + nrm(ks[24], (D_MODEL,), 0.02),
    }


def reference(x_prompt, x_sample, state_hgrn, state_gla, state_rwkv, state_shift, norm_w, w_in,
              hgrn_lb, hgrn_norm_w, gla_gk_w2, gla_gk_b, gla_norm_w, rwkv_mu, rwkv_w0, rwkv_w2,
              rwkv_a0, rwkv_a2, rwkv_k_k, rwkv_k_a, rwkv_r_k, rwkv_ln_w, rwkv_ln_b, w_out,
              final_norm_w):
    f32 = jnp.float32
    weights = (norm_w, w_in, hgrn_lb, hgrn_norm_w, gla_gk_w2, gla_gk_b, gla_norm_w, rwkv_mu,
               rwkv_w0, rwkv_w2, rwkv_a0, rwkv_a2, rwkv_k_k, rwkv_k_a, rwkv_r_k, rwkv_ln_w,
               rwkv_ln_b, w_out, final_norm_w)
    bp = x_prompt.shape[0]
    y_prompt, hgrn_p, gla_p, rwkv_p, shift_p = _trunk(
        x_prompt,
        jnp.zeros((DEPTH, bp, H_A, DK_A, DV_A), f32),
        jnp.zeros((DEPTH, bp, H_B, DK_B, DV_B), f32),
        jnp.zeros((DEPTH, bp, H_C, N_C, N_C), f32),
        jnp.zeros((DEPTH, bp, D_SHIFT), f32),
        *weights)
    y_sample, hgrn_s, gla_s, rwkv_s, shift_s = _trunk(
        x_sample, state_hgrn, state_gla, state_rwkv, state_shift, *weights)
    return (y_prompt, y_sample, hgrn_p, gla_p, rwkv_p, shift_p, hgrn_s, gla_s, rwkv_s, shift_s)
```

```python
import collections
import functools

import jax
import jax.numpy as jnp
from jax import lax
from jax.experimental import pallas as pl
from jax.experimental.pallas import tpu as pltpu

F32 = jnp.float32
BF16 = jnp.bfloat16

D_MODEL = 1024
DEPTH = 4
HEAD = 64
W_A = D_MODEL // 4
H_A = W_A // HEAD
DK_A = HEAD
DV_A = W_A // H_A
W_B = (D_MODEL - W_A) // 2
H_B = 4
DV_B = W_B // H_B
DK_B = DV_B // 2
GLA_LR = 16
GLA_NORM = 16.0
W_C = D_MODEL - W_A - W_B
H_C = W_C // HEAD
N_C = HEAD
DECAY_LR = 32
AAA_LR = 32
GN_EPS = 64e-5
CHUNK = 16
TINY = 1e-30
D_A_IN = 2 * H_A * DK_A + 2 * W_A
D_B_IN = 2 * H_B * DK_B + 2 * W_B + GLA_LR
D_SHIFT = 3 * W_C + DECAY_LR + AAA_LR
D_IN = D_A_IN + D_B_IN + D_SHIFT + W_C

QA = (0, H_A * DK_A)
ZA = (QA[1], QA[1] + H_A * DK_A)
IA = (ZA[1], ZA[1] + W_A)
GATE_A = (IA[1], IA[1] + W_A)
QB = (D_A_IN, D_A_IN + H_B * DK_B)
KB = (QB[1], QB[1] + H_B * DK_B)
VB = (KB[1], KB[1] + W_B)
GKL = (VB[1], VB[1] + GLA_LR)
GATE_B = (GKL[1], GKL[1] + W_B)
PC = (D_A_IN + D_B_IN, D_A_IN + D_B_IN + D_SHIFT)
GATE_C = (PC[1], PC[1] + W_C)
SH_R = (0, W_C)
SH_WL = (SH_R[1], SH_R[1] + DECAY_LR)
SH_K = (SH_WL[1], SH_WL[1] + W_C)
SH_V = (SH_K[1], SH_K[1] + W_C)
SH_AL = (SH_V[1], SH_V[1] + AAA_LR)

SUBLANES = 8
VMEM_LIMIT_BYTES = 52 * 1024 * 1024

Cfg = collections.namedtuple("Cfg", "layer rows tseq chunk has_state final")


def _dot(a, b):
    return jnp.dot(a.astype(BF16), b.astype(BF16), preferred_element_type=F32)


def _dot_nt(a, b):
    return lax.dot_general(a.astype(BF16), b.astype(BF16), (((1,), (1,)), ((), ())),
                           preferred_element_type=F32)


def _dot_tn(a, b):
    return lax.dot_general(a.astype(BF16), b.astype(BF16), (((0,), (0,)), ((), ())),
                           preferred_element_type=F32)


def _dot_nt3(a, b):
    ah, bh = a.astype(BF16), b.astype(BF16)
    al, bl = (a - ah.astype(F32)).astype(BF16), (b - bh.astype(F32)).astype(BF16)
    dims = (((1,), (1,)), ((), ()))
    return (lax.dot_general(ah, bh, dims, preferred_element_type=F32)
            + lax.dot_general(ah, bl, dims, preferred_element_type=F32)
            + lax.dot_general(al, bh, dims, preferred_element_type=F32))


def _dot_hi(a, b):
    return jnp.dot(a, b, precision=lax.Precision.HIGHEST, preferred_element_type=F32)


def _segsum(x, m):
    hi = x.astype(BF16)
    lo = (x - hi.astype(F32)).astype(BF16)
    return (jnp.dot(hi, m, preferred_element_type=F32)
            + jnp.dot(lo, m, preferred_element_type=F32))


def _softplus(x):
    return jnp.maximum(x, 0.0) + jnp.log1p(jnp.exp(-jnp.abs(x)))


def _logsig(x):
    return jnp.minimum(x, 0.0) - jnp.log1p(jnp.exp(-jnp.abs(x)))


def _silu(x):
    return x * jax.nn.sigmoid(x)


def _chunk_cumsum(x, chunk):
    pos = lax.broadcasted_iota(jnp.int32, x.shape, 0) & (chunk - 1)
    s = 1
    while s < chunk:
        x = x + jnp.where(pos >= s, pltpu.roll(x, s, axis=0), 0.0)
        s *= 2
    return x


def _tri_mask(c, strict):
    i = lax.broadcasted_iota(jnp.int32, (c, c), 0)
    j = lax.broadcasted_iota(jnp.int32, (c, c), 1)
    return (j < i) if strict else (j <= i)


def _inv_unit_lower(a, c):
    w = jnp.where(_tri_mask(c, False) & ~_tri_mask(c, True), 1.0, 0.0) - a
    p = a
    n = 2
    while n < c:
        p = _dot_hi(p, p)
        w = w + _dot_hi(w, p)
        n *= 2
    return w


def _column(row, width):
    return jnp.broadcast_to(row, (SUBLANES, width)).T[:, 0:1]


def _gla_chunk(q, k, v, g, s_ref, seq, heads, dk, dv, c):
    mid = c // 2 - 1
    g_mid = g[mid:mid + 1]
    g_last = g[c - 1:c]
    q_rel = q * jnp.exp(g - g_mid)
    k_rel = k * jnp.exp(g_mid - g)
    q_abs = q * jnp.exp(g)
    k_end = k * jnp.exp(g_last - g)
    decay_col = _column(jnp.exp(g_last), heads * dk)
    causal = _tri_mask(c, False)
    outs = []
    for h in range(heads):
        ks = slice(h * dk, (h + 1) * dk)
        vs = slice(h * dv, (h + 1) * dv)
        s = s_ref[seq, h]
        a = jnp.where(causal, _dot_nt3(q_rel[:, ks], k_rel[:, ks]), 0.0)
        outs.append(_dot(a, v[:, vs]) + _dot(q_abs[:, ks], s))
        s_ref[seq, h] = s * decay_col[ks] + _dot_tn(k_end[:, ks], v[:, vs])
    return outs


def _rwkv_chunk(r, k, v, kk, b, lc, lw, s_ref, seq, c):
    l_last = lc[c - 1:c]
    inv_cum = jnp.exp(-lc)
    kk_in = kk * jnp.exp(lc - lw)
    b_out = b * inv_cum
    k_out = k * inv_cum
    r_in = r * jnp.exp(lc)
    to_end = jnp.exp(l_last - lc)
    b_end = b * to_end
    k_end = k * to_end
    decay_row = jnp.exp(l_last)
    strict = _tri_mask(c, True)
    causal = _tri_mask(c, False)
    outs = []
    for h in range(H_C):
        cs = slice(h * N_C, (h + 1) * N_C)
        s = s_ref[seq, h]
        vh = v[:, cs]
        a = jnp.where(strict, _dot_nt3(kk_in[:, cs], b_out[:, cs]), 0.0)
        bm = jnp.where(strict, _dot_nt3(kk_in[:, cs], k_out[:, cs]), 0.0)
        rhs = _dot_nt(kk_in[:, cs], s) + _dot(bm, vh)
        u = -_dot_hi(_inv_unit_lower(a, c), rhs)
        rb = jnp.where(causal, _dot_nt3(r_in[:, cs], b_out[:, cs]), 0.0)
        rk = jnp.where(causal, _dot_nt3(r_in[:, cs], k_out[:, cs]), 0.0)
        outs.append(_dot_nt(r_in[:, cs], s) + _dot(rb, u) + _dot(rk, vh))
        s_ref[seq, h] = s * decay_row[:, cs] + _dot_tn(u, b_end[:, cs]) + _dot_tn(vh, k_end[:, cs])
    return outs


def _layer_kernel(names, cfg, *refs):
    r = dict(zip(names, refs))
    rows, c, tseq = cfg.rows, cfg.chunk, cfg.tseq
    nseq = rows // tseq
    t = pl.program_id(1)
    p = r["p"]

    @pl.when(t == 0)
    def _init():
        if cfg.has_state:
            r["sa"][...] = r["sa_in"][...]
            r["sb"][...] = r["sb_in"][...]
            r["sc"][...] = r["sc_in"][...]
        else:
            r["sa"][...] = jnp.zeros(r["sa"].shape, F32)
            r["sb"][...] = jnp.zeros(r["sb"].shape, F32)
            r["sc"][...] = jnp.zeros(r["sc"].shape, F32)
            r["pct"][...] = jnp.zeros(r["pct"].shape, F32)

    x = r["x"][...]
    hn = x * lax.rsqrt(jnp.mean(x * x, axis=-1, keepdims=True) + 1e-6) * r["norm_w"][...]
    p[...] = jnp.dot(hn.astype(BF16), r["w_in"][...], preferred_element_type=F32)

    hl = r["hgrn_lb"][...]
    e = jnp.exp(hl - jnp.max(hl, axis=0, keepdims=True))
    sm = e / jnp.sum(e, axis=0, keepdims=True)
    cum = sm[0:1]
    for j in range(1, cfg.layer + 1):
        cum = cum + sm[j:j + 1]
    lb = cum - sm[0:1]
    za = p[:, ZA[0]:ZA[1]]
    ls = _logsig(za)
    la = jnp.log(jnp.maximum(lb, TINY))
    bb = jnp.log1p(-lb) + ls
    lae = jnp.maximum(la, bb) + jnp.log1p(jnp.exp(-jnp.abs(la - bb)))
    r["ga"][...] = _chunk_cumsum(jnp.where(lb > 0.0, lae, ls), c)
    r["ka"][...] = (1.0 - lb) * jax.nn.sigmoid(-za)
    r["qa"][...] = p[:, QA[0]:QA[1]] * (DK_A ** -0.5)

    gk = _logsig(_dot_hi(p[:, GKL[0]:GKL[1]], r["gk_w2"][...]) + r["gk_b"][...]) * (1.0 / GLA_NORM)
    r["gb"][...] = _chunk_cumsum(gk, c)
    r["qb"][...] = p[:, QB[0]:QB[1]] * (DK_B ** -0.5)

    pc = p[:, PC[0]:PC[1]]
    row = lax.broadcasted_iota(jnp.int32, (rows, 1), 0)
    first = (row & (tseq - 1)) == 0
    if cfg.has_state:
        prev0 = r["shift_rows"][...]
    else:
        prev0 = r["pct"][SUBLANES - 1:SUBLANES, :]
    prev = jnp.where(first, prev0, pltpu.roll(pc, 1, axis=0))
    r["xs"][...] = pc + (prev - pc) * r["mu"][...]
    if cfg.has_state:
        r["pct"][...] = pc
    else:
        r["pct"][...] = pc[rows - SUBLANES:rows]
    xs = r["xs"]
    kc = xs[:, SH_K[0]:SH_K[1]]
    wlog = -_softplus(-(r["w0"][...] + _dot_hi(jnp.tanh(xs[:, SH_WL[0]:SH_WL[1]]), r["w2"][...]))) - 0.5
    lw = -jnp.exp(wlog)
    av = jax.nn.sigmoid(r["a0"][...] + _dot_hi(xs[:, SH_AL[0]:SH_AL[1]], r["a2"][...]))
    kk = kc * r["k_k"][...]
    kk = kk * lax.rsqrt(jnp.maximum(_segsum(kk * kk, r["m64c"][...]), 1e-24))
    r["rc"][...] = xs[:, SH_R[0]:SH_R[1]]
    r["vc"][...] = xs[:, SH_V[0]:SH_V[1]]
    r["kc"][...] = kc * (1.0 + (av - 1.0) * r["k_a"][...])
    r["kkc"][...] = kk
    r["bc"][...] = kk * av
    r["lc"][...] = _chunk_cumsum(lw, c)
    r["lwc"][...] = lw

    rpi = max(c, SUBLANES)
    spi = rpi // c
    assert nseq == 1 or tseq == c

    def body(n, carry):
        rs = pl.ds(pl.multiple_of(n * rpi, rpi), rpi)
        qa, ka, ga, va = r["qa"][rs, :], r["ka"][rs, :], r["ga"][rs, :], p[rs, IA[0]:IA[1]]
        qb, kb, gb, vb = r["qb"][rs, :], p[rs, KB[0]:KB[1]], r["gb"][rs, :], p[rs, VB[0]:VB[1]]
        rc, kcm, vc, kkc = r["rc"][rs, :], r["kc"][rs, :], r["vc"][rs, :], r["kkc"][rs, :]
        bc, lc, lwc = r["bc"][rs, :], r["lc"][rs, :], r["lwc"][rs, :]
        oa, ob, oc = [], [], []
        for j in range(spi):
            sl = slice(j * c, (j + 1) * c)
            seq = 0 if nseq == 1 else n * spi + j
            oa.append(_gla_chunk(qa[sl], ka[sl], va[sl], ga[sl], r["sa"], seq, H_A, DK_A, DV_A, c))
            ob.append(_gla_chunk(qb[sl], kb[sl], vb[sl], gb[sl], r["sb"], seq, H_B, DK_B, DV_B, c))
            oc.append(_rwkv_chunk(rc[sl], kcm[sl], vc[sl], kkc[sl], bc[sl], lc[sl], lwc[sl],
                                  r["sc"], seq, c))
        for outs, base, heads, dv in ((oa, 0, H_A, DV_A), (ob, W_A, H_B, DV_B), (oc, W_A + W_B, H_C, N_C)):
            for h in range(heads):
                tile = outs[0][h] if spi == 1 else jnp.concatenate([o[h] for o in outs], axis=0)
                r["o"][rs, base + h * dv:base + (h + 1) * dv] = tile
        return carry

    lax.fori_loop(0, rows // rpi, body, 0)

    o = r["o"]
    oa = o[:, 0:W_A]
    ya = (oa * lax.rsqrt(_segsum(oa * oa, r["m64a"][...]) * (1.0 / DV_A) + 1e-5)
          * r["hgrn_norm_w"][...] * _silu(p[:, GATE_A[0]:GATE_A[1]]))
    ob = o[:, W_A:W_A + W_B]
    yb = (ob * lax.rsqrt(_segsum(ob * ob, r["m96"][...]) * (1.0 / DV_B) + 1e-5)
          * r["gla_norm_w"][...] * _silu(p[:, GATE_B[0]:GATE_B[1]]))
    oc = o[:, W_A + W_B:D_MODEL]
    m64c = r["m64c"][...]
    dc = oc - _segsum(oc, m64c) * (1.0 / N_C)
    ocn = dc * lax.rsqrt(_segsum(dc * dc, m64c) * (1.0 / N_C) + GN_EPS) * r["ln_w"][...] + r["ln_b"][...]
    bonus = _segsum(r["rc"][...] * r["kc"][...] * r["r_k"][...], m64c) * r["vc"][...]
    yc = (ocn + bonus) * _silu(p[:, GATE_C[0]:GATE_C[1]])
    y = jnp.concatenate([ya, yb, yc], axis=-1)
    out = r["x"][...] + jnp.dot(y.astype(BF16), r["w_out"][...], preferred_element_type=F32)
    if cfg.final:
        out = out * lax.rsqrt(jnp.mean(out * out, axis=-1, keepdims=True) + 1e-6) * r["final_norm_w"][...]
    r["xo"][...] = out


def _block_diag_ones(width, seg):
    i = jnp.arange(width) // seg
    return (i[:, None] == i[None, :]).astype(BF16)


def _run_layer(cfg, nb, nt, x2d, state, params):
    l, rows = cfg.layer, cfg.rows
    nseq = rows // cfg.tseq
    nseq_total = nb * nseq
    ins, specs = [], []

    def add(name, arr, spec):
        ins.append((name, arr))
        specs.append(spec)

    def layer_row(name, arr):
        w = arr.shape[-1]
        add(name, arr.reshape(DEPTH, 1, w), pl.BlockSpec((None, 1, w), lambda b, t: (l, 0, 0)))

    def layer_mat(name, arr):
        add(name, arr, pl.BlockSpec((None,) + arr.shape[1:], lambda b, t: (l, 0, 0)))

    def whole(name, arr):
        add(name, arr, pl.BlockSpec(arr.shape, lambda b, t: (0,) * arr.ndim))

    add("x", x2d, pl.BlockSpec((rows, D_MODEL), lambda b, t: (b * nt + t, 0)))
    if cfg.has_state:
        st_a, st_b, st_c, shift_rows = state
        add("shift_rows", shift_rows, pl.BlockSpec((None, rows, D_SHIFT), lambda b, t: (l, b, 0)))
        add("sa_in", st_a, pl.BlockSpec((None, nseq, H_A, DK_A, DV_A), lambda b, t: (l, b, 0, 0, 0)))
        add("sb_in", st_b, pl.BlockSpec((None, nseq, H_B, DK_B, DV_B), lambda b, t: (l, b, 0, 0, 0)))
        add("sc_in", st_c, pl.BlockSpec((None, nseq, H_C, N_C, N_C), lambda b, t: (l, b, 0, 0, 0)))
    layer_row("norm_w", params["norm_w"])
    layer_mat("w_in", params["w_in"])
    whole("hgrn_lb", params["hgrn_lb"])
    layer_row("hgrn_norm_w", params["hgrn_norm_w"])
    layer_mat("gk_w2", params["gla_gk_w2"])
    layer_row("gk_b", params["gla_gk_b"])
    layer_row("gla_norm_w", params["gla_norm_w"])
    layer_row("mu", params["rwkv_mu"])
    layer_row("w0", params["rwkv_w0"])
    layer_mat("w2", params["rwkv_w2"])
    layer_row("a0", params["rwkv_a0"])
    layer_mat("a2", params["rwkv_a2"])
    layer_row("k_k", params["rwkv_k_k"])
    layer_row("k_a", params["rwkv_k_a"])
    layer_row("r_k", params["rwkv_r_k"])
    layer_row("ln_w", params["rwkv_ln_w"])
    layer_row("ln_b", params["rwkv_ln_b"])
    layer_mat("w_out", params["w_out"])
    if cfg.final:
        whole("final_norm_w", params["final_norm_w"].reshape(1, D_MODEL))
    whole("m64a", params["m64a"])
    whole("m96", params["m96"])
    whole("m64c", params["m64c"])

    pct_rows = rows if cfg.has_state else SUBLANES
    outs = [
        ("xo", jax.ShapeDtypeStruct(x2d.shape, F32), pl.BlockSpec((rows, D_MODEL), lambda b, t: (b * nt + t, 0))),
        ("sa", jax.ShapeDtypeStruct((nseq_total, H_A, DK_A, DV_A), F32),
         pl.BlockSpec((nseq, H_A, DK_A, DV_A), lambda b, t: (b, 0, 0, 0))),
        ("sb", jax.ShapeDtypeStruct((nseq_total, H_B, DK_B, DV_B), F32),
         pl.BlockSpec((nseq, H_B, DK_B, DV_B), lambda b, t: (b, 0, 0, 0))),
        ("sc", jax.ShapeDtypeStruct((nseq_total, H_C, N_C, N_C), F32),
         pl.BlockSpec((nseq, H_C, N_C, N_C), lambda b, t: (b, 0, 0, 0))),
        ("pct", jax.ShapeDtypeStruct((nb * pct_rows, D_SHIFT), F32),
         pl.BlockSpec((pct_rows, D_SHIFT), lambda b, t: (b, 0))),
    ]
    scratch = [
        ("p", D_IN), ("xs", D_SHIFT), ("o", D_MODEL),
        ("qa", H_A * DK_A), ("ka", H_A * DK_A), ("ga", H_A * DK_A),
        ("qb", H_B * DK_B), ("gb", H_B * DK_B),
        ("rc", W_C), ("vc", W_C), ("kc", W_C), ("kkc", W_C), ("bc", W_C), ("lc", W_C), ("lwc", W_C),
    ]
    names = tuple(n for n, _ in ins) + tuple(n for n, _, _ in outs) + tuple(n for n, _ in scratch)
    return pl.pallas_call(
        functools.partial(_layer_kernel, names, cfg),
        grid=(nb, nt),
        in_specs=specs,
        out_specs=[s for _, _, s in outs],
        out_shape=[s for _, s, _ in outs],
        scratch_shapes=[pltpu.VMEM((rows, w), F32) for _, w in scratch],
        compiler_params=pltpu.CompilerParams(
            dimension_semantics=("arbitrary", "arbitrary"),
            vmem_limit_bytes=VMEM_LIMIT_BYTES),
        name=f"layer{l}_{'sample' if cfg.has_state else 'prompt'}",
    )(*[a for _, a in ins])


def _trunk(x, state, params, rows, chunk):
    bsz, tlen, _ = x.shape
    has_state = state is not None
    if has_state:
        assert tlen == chunk and rows % tlen == 0 and (bsz * tlen) % rows == 0
        tseq, nb, nt = tlen, (bsz * tlen) // rows, 1
        st_a, st_b, st_c, st_s = state
        shift_rows = jnp.pad(st_s[:, :, None, :], ((0, 0), (0, 0), (0, tlen - 1), (0, 0)))
        state = (st_a, st_b, st_c, shift_rows.reshape(DEPTH, bsz * tlen, D_SHIFT))
    else:
        assert tlen % rows == 0 and rows % chunk == 0
        tseq, nb, nt = rows, bsz, tlen // rows
    x2d = x.reshape(bsz * tlen, D_MODEL)
    new_a, new_b, new_c, new_s = [], [], [], []
    for l in range(DEPTH):
        cfg = Cfg(layer=l, rows=rows, tseq=tseq, chunk=chunk, has_state=has_state, final=(l == DEPTH - 1))
        x2d, sa, sb, sc, pct = _run_layer(cfg, nb, nt, x2d, state, params)
        new_a.append(sa)
        new_b.append(sb)
        new_c.append(sc)
        if has_state:
            new_s.append(pct.reshape(bsz, tlen, D_SHIFT)[:, tlen - 1])
        else:
            new_s.append(pct.reshape(bsz, SUBLANES, D_SHIFT)[:, SUBLANES - 1])
    return (x2d.reshape(bsz, tlen, D_MODEL), jnp.stack(new_a), jnp.stack(new_b),
            jnp.stack(new_c), jnp.stack(new_s))


PROMPT_ROWS = 256
SAMPLE_ROWS = 32


def kernel(x_prompt, x_sample, state_hgrn, state_gla, state_rwkv, state_shift, norm_w, w_in, hgrn_lb, hgrn_norm_w, gla_gk_w2, gla_gk_b, gla_norm_w, rwkv_mu, rwkv_w0, rwkv_w2, rwkv_a0, rwkv_a2, rwkv_k_k, rwkv_k_a, rwkv_r_k, rwkv_ln_w, rwkv_ln_b, w_out, final_norm_w):
    params = dict(
        norm_w=norm_w, w_in=w_in.astype(BF16), hgrn_lb=hgrn_lb, hgrn_norm_w=hgrn_norm_w,
        gla_gk_w2=gla_gk_w2, gla_gk_b=gla_gk_b, gla_norm_w=gla_norm_w, rwkv_mu=rwkv_mu,
        rwkv_w0=rwkv_w0, rwkv_w2=rwkv_w2, rwkv_a0=rwkv_a0, rwkv_a2=rwkv_a2, rwkv_k_k=rwkv_k_k,
        rwkv_k_a=rwkv_k_a, rwkv_r_k=rwkv_r_k, rwkv_ln_w=rwkv_ln_w, rwkv_ln_b=rwkv_ln_b,
        w_out=w_out.astype(BF16), final_norm_w=final_norm_w,
        m64a=_block_diag_ones(W_A, DV_A), m96=_block_diag_ones(W_B, DV_B),
        m64c=_block_diag_ones(W_C, N_C),
    )
    y_p, hgrn_p, gla_p, rwkv_p, shift_p = _trunk(x_prompt, None, params, PROMPT_ROWS, CHUNK)
    y_s, hgrn_s, gla_s, rwkv_s, shift_s = _trunk(
        x_sample, (state_hgrn, state_gla, state_rwkv, state_shift), params, SAMPLE_ROWS,
        x_sample.shape[1])
    return (y_p, y_s, hgrn_p, gla_p, rwkv_p, shift_p, hgrn_s, gla_s, rwkv_s, shift_s)
```

```python
import collections
import functools

import jax
import jax.numpy as jnp
from jax import lax
from jax.experimental import pallas as pl
from jax.experimental.pallas import tpu as pltpu

F32 = jnp.float32
BF16 = jnp.bfloat16

D_MODEL = 1024
DEPTH = 4
HEAD = 64
W_A = D_MODEL // 4
H_A = W_A // HEAD
DK_A = HEAD
DV_A = W_A // H_A
W_B = (D_MODEL - W_A) // 2
H_B = 4
DV_B = W_B // H_B
DK_B = DV_B // 2
GLA_LR = 16
GLA_NORM = 16.0
W_C = D_MODEL - W_A - W_B
H_C = W_C // HEAD
N_C = HEAD
DECAY_LR = 32
AAA_LR = 32
GN_EPS = 64e-5
CHUNK = 16
TINY = 1e-30
D_A_IN = 2 * H_A * DK_A + 2 * W_A
D_B_IN = 2 * H_B * DK_B + 2 * W_B + GLA_LR
D_SHIFT = 3 * W_C + DECAY_LR + AAA_LR
D_IN = D_A_IN + D_B_IN + D_SHIFT + W_C

QA = (0, H_A * DK_A)
ZA = (QA[1], QA[1] + H_A * DK_A)
IA = (ZA[1], ZA[1] + W_A)
GATE_A = (IA[1], IA[1] + W_A)
QB = (D_A_IN, D_A_IN + H_B * DK_B)
KB = (QB[1], QB[1] + H_B * DK_B)
VB = (KB[1], KB[1] + W_B)
GKL = (VB[1], VB[1] + GLA_LR)
GATE_B = (GKL[1], GKL[1] + W_B)
PC = (D_A_IN + D_B_IN, D_A_IN + D_B_IN + D_SHIFT)
GATE_C = (PC[1], PC[1] + W_C)
SH_R = (0, W_C)
SH_WL = (SH_R[1], SH_R[1] + DECAY_LR)
SH_K = (SH_WL[1], SH_WL[1] + W_C)
SH_V = (SH_K[1], SH_K[1] + W_C)
SH_AL = (SH_V[1], SH_V[1] + AAA_LR)
OUT_A = 0
OUT_B = W_A
OUT_C = W_A + W_B

SUBLANES = 8
VMEM_LIMIT_BYTES = 52 * 1024 * 1024

Cfg = collections.namedtuple("Cfg", "layer rows tseq chunk has_state final")


def _dot(a, b):
    return jnp.dot(a.astype(BF16), b.astype(BF16), preferred_element_type=F32)


def _dot_nt(a, b):
    return lax.dot_general(a.astype(BF16), b.astype(BF16), (((1,), (1,)), ((), ())),
                           preferred_element_type=F32)


def _dot_tn(a, b):
    return lax.dot_general(a.astype(BF16), b.astype(BF16), (((0,), (0,)), ((), ())),
                           preferred_element_type=F32)


def _split(a):
    hi = a.astype(BF16)
    return hi, (a - hi.astype(F32)).astype(BF16)


def _dot3(a, b, dims):
    (ah, al), (bh, bl) = _split(a), _split(b)
    return (lax.dot_general(ah, bh, dims, preferred_element_type=F32)
            + lax.dot_general(ah, bl, dims, preferred_element_type=F32)
            + lax.dot_general(al, bh, dims, preferred_element_type=F32))


def _dot_nt3(a, b):
    return _dot3(a, b, (((1,), (1,)), ((), ())))


def _dot_nn3(a, b):
    return _dot3(a, b, (((1,), (0,)), ((), ())))


def _dot_hi(a, b):
    return jnp.dot(a, b, precision=lax.Precision.HIGHEST, preferred_element_type=F32)


def _segsum(x, m):
    hi, lo = _split(x)
    return (jnp.dot(hi, m, preferred_element_type=F32)
            + jnp.dot(lo, m, preferred_element_type=F32))


def _softplus(x):
    return jnp.maximum(x, 0.0) + jnp.log1p(jnp.exp(-jnp.abs(x)))


def _logsig(x):
    return jnp.minimum(x, 0.0) - jnp.log1p(jnp.exp(-jnp.abs(x)))


def _silu(x):
    return x * jax.nn.sigmoid(x)


def _chunk_scan(x, chunk, pos, reverse=False):
    n = x.shape[0]
    s = 1
    while s < chunk:
        if reverse:
            x = x + jnp.where(pos < chunk - s, pltpu.roll(x, n - s, axis=0), 0.0)
        else:
            x = x + jnp.where(pos >= s, pltpu.roll(x, s, axis=0), 0.0)
        s *= 2
    return x


def _tri_mask(c, strict):
    i = lax.broadcasted_iota(jnp.int32, (c, c), 0)
    j = lax.broadcasted_iota(jnp.int32, (c, c), 1)
    return (j < i) if strict else (j <= i)


def _inv_unit_lower(a, eye, c, matmul):
    w = eye - a
    p = a
    n = 2
    while n < c:
        p = matmul(p, p)
        w = w + matmul(w, p)
        n *= 2
    return w


def _column(row, width):
    return jnp.broadcast_to(row, (SUBLANES, width)).T[:, 0:1]


def _gla_chunk(q, k, v, g, s_ref, seq, heads, dk, dv, c):
    mid = c // 2 - 1
    g_mid = g[mid:mid + 1]
    g_last = g[c - 1:c]
    q_rel = q * jnp.exp(g - g_mid)
    k_rel = k * jnp.exp(g_mid - g)
    q_abs = q * jnp.exp(g)
    k_end = k * jnp.exp(g_last - g)
    decay_col = _column(jnp.exp(g_last), heads * dk)
    causal = _tri_mask(c, False)
    outs = []
    for h in range(heads):
        ks = slice(h * dk, (h + 1) * dk)
        vs = slice(h * dv, (h + 1) * dv)
        s = s_ref[seq, h]
        a = jnp.where(causal, _dot_nt3(q_rel[:, ks], k_rel[:, ks]), 0.0)
        outs.append(_dot(a, v[:, vs]) + _dot(q_abs[:, ks], s))
        s_ref[seq, h] = s * decay_col[ks] + _dot_tn(k_end[:, ks], v[:, vs])
    return outs


def _rwkv_chunk(r, k, v, kk, b, lc, lw, s_ref, seq, c):
    l_last = lc[c - 1:c]
    inv_cum = jnp.exp(-lc)
    kk_in = kk * jnp.exp(lc - lw)
    b_out = b * inv_cum
    k_out = k * inv_cum
    r_in = r * jnp.exp(lc)
    to_end = jnp.exp(l_last - lc)
    b_end = b * to_end
    k_end = k * to_end
    decay_row = jnp.exp(l_last)
    strict = _tri_mask(c, True)
    causal = _tri_mask(c, False)
    eye = jnp.where(causal & ~strict, 1.0, 0.0)
    outs = []
    for h in range(H_C):
        cs = slice(h * N_C, (h + 1) * N_C)
        s = s_ref[seq, h]
        vh = v[:, cs]
        a = jnp.where(strict, _dot_nt3(kk_in[:, cs], b_out[:, cs]), 0.0)
        bm = jnp.where(strict, _dot_nt3(kk_in[:, cs], k_out[:, cs]), 0.0)
        rhs = _dot_nt(kk_in[:, cs], s) + _dot(bm, vh)
        u = -_dot_hi(_inv_unit_lower(a, eye, c, _dot_hi), rhs)
        rb = jnp.where(causal, _dot_nt3(r_in[:, cs], b_out[:, cs]), 0.0)
        rk = jnp.where(causal, _dot_nt3(r_in[:, cs], k_out[:, cs]), 0.0)
        outs.append(_dot_nt(r_in[:, cs], s) + _dot(rb, u) + _dot(rk, vh))
        s_ref[seq, h] = s * decay_row[:, cs] + _dot_tn(u, b_end[:, cs]) + _dot_tn(vh, k_end[:, cs])
    return outs


def _recur_per_sequence(r, p, cfg, g_a, g_b, lw, kk, b):
    rows, c = cfg.rows, cfg.chunk
    pos = lax.broadcasted_iota(jnp.int32, (rows, 1), 0) & (c - 1)
    r["ga"][...] = _chunk_scan(g_a, c, pos)
    r["gb"][...] = _chunk_scan(g_b, c, pos)
    r["lc"][...] = _chunk_scan(lw, c, pos)
    r["lwc"][...] = lw
    r["kkc"][...] = kk
    r["bc"][...] = b
    r["qa"][...] = p[:, QA[0]:QA[1]] * (DK_A ** -0.5)
    r["qb"][...] = p[:, QB[0]:QB[1]] * (DK_B ** -0.5)
    rpi = max(c, SUBLANES)
    spi = rpi // c

    def body(n, carry):
        rs = pl.ds(pl.multiple_of(n * rpi, rpi), rpi)
        qa, ka, ga, va = r["qa"][rs, :], r["ka"][rs, :], r["ga"][rs, :], p[rs, IA[0]:IA[1]]
        qb, kb, gb, vb = r["qb"][rs, :], p[rs, KB[0]:KB[1]], r["gb"][rs, :], p[rs, VB[0]:VB[1]]
        rc, kcm, vc, kkc = r["rc"][rs, :], r["kc"][rs, :], r["vc"][rs, :], r["kkc"][rs, :]
        bc, lc, lwc = r["bc"][rs, :], r["lc"][rs, :], r["lwc"][rs, :]
        oa, ob, oc = [], [], []
        for j in range(spi):
            sl = slice(j * c, (j + 1) * c)
            seq = n * spi + j
            oa.append(_gla_chunk(qa[sl], ka[sl], va[sl], ga[sl], r["sa"], seq, H_A, DK_A, DV_A, c))
            ob.append(_gla_chunk(qb[sl], kb[sl], vb[sl], gb[sl], r["sb"], seq, H_B, DK_B, DV_B, c))
            oc.append(_rwkv_chunk(rc[sl], kcm[sl], vc[sl], kkc[sl], bc[sl], lc[sl], lwc[sl],
                                  r["sc"], seq, c))
        for outs, base, heads, dv in ((oa, OUT_A, H_A, DV_A), (ob, OUT_B, H_B, DV_B), (oc, OUT_C, H_C, N_C)):
            for h in range(heads):
                tile = outs[0][h] if spi == 1 else jnp.concatenate([o[h] for o in outs], axis=0)
                r["o"][rs, base + h * dv:base + (h + 1) * dv] = tile
        return carry

    lax.fori_loop(0, rows // rpi, body, 0)


def _gla_block_prep(r, q, k, v, g, pos, masks, names, heads, dk, dv, out_base, c):
    qx_name, kx_name, eg_name = names
    causal = masks[0]
    mid = c // 2 - 1
    after = jnp.where(pos > mid, g, 0.0)
    upto = jnp.where(pos <= mid, g, 0.0)
    rel = _chunk_scan(after, c, pos) - (_chunk_scan(upto, c, pos, reverse=True) - upto)
    q_rel = q * jnp.exp(rel)
    k_rel = k * jnp.exp(-rel)
    eg = jnp.exp(_chunk_scan(g, c, pos))
    r[eg_name][...] = eg
    r[qx_name][...] = q * eg
    r[kx_name][...] = k * jnp.exp(_chunk_scan(g, c, pos, reverse=True) - g)
    for h in range(heads):
        ks = slice(h * dk, (h + 1) * dk)
        sc = jnp.where(causal, _dot_nt3(q_rel[:, ks], k_rel[:, ks]), 0.0)
        r["o"][:, out_base + h * dv:out_base + (h + 1) * dv] = _dot(sc, v[:, h * dv:(h + 1) * dv])


def _rwkv_block_prep(r, lw, kk, b, pos, masks, c):
    causal, strict, eye = masks
    rr, kmod, v = r["rc"][...], r["kc"][...], r["vc"][...]
    lc = _chunk_scan(lw, c, pos)
    el = jnp.exp(lc)
    inv_cum = jnp.exp(-lc)
    kk_in = kk * jnp.exp(lc - lw)
    b_out = b * inv_cum
    k_out = kmod * inv_cum
    r_in = rr * el
    to_end = jnp.exp(_chunk_scan(lw, c, pos, reverse=True) - lw)
    r["elc"][...] = el
    r["rin"][...] = r_in
    r["bend"][...] = b * to_end
    r["kend"][...] = kmod * to_end
    for h in range(H_C):
        cs = slice(h * N_C, (h + 1) * N_C)
        vh = v[:, cs]
        a = jnp.where(strict, _dot_nt3(kk_in[:, cs], b_out[:, cs]), 0.0)
        bm = jnp.where(strict, _dot_nt3(kk_in[:, cs], k_out[:, cs]), 0.0)
        rb = jnp.where(causal, _dot_nt3(r_in[:, cs], b_out[:, cs]), 0.0)
        rk = jnp.where(causal, _dot_nt3(r_in[:, cs], k_out[:, cs]), 0.0)
        w = _inv_unit_lower(a, eye, c, _dot_nn3)
        r["kh"][:, cs] = _dot_nn3(w, kk_in[:, cs])
        r["u0"][:, cs] = _dot_nn3(w, _dot(bm, vh))
        r["rb"][h] = rb.astype(BF16)
        r["o"][:, OUT_C + h * N_C:OUT_C + (h + 1) * N_C] = _dot(rk, vh)


def _recur_blocked(r, p, cfg, g_a, g_b, lw, kk, b):
    rows, c = cfg.rows, cfg.chunk
    t = pl.program_id(1)
    pos = lax.broadcasted_iota(jnp.int32, (rows, 1), 0) & (c - 1)
    shift = c.bit_length() - 1
    i = lax.broadcasted_iota(jnp.int32, (rows, rows), 0)
    j = lax.broadcasted_iota(jnp.int32, (rows, rows), 1)
    same = (i >> shift) == (j >> shift)
    masks = (same & (j <= i), same & (j < i), jnp.where(i == j, 1.0, 0.0))

    @pl.when(t == 0)
    def _init():
        r["sat"][...] = jnp.zeros(r["sat"].shape, F32)
        r["sbt"][...] = jnp.zeros(r["sbt"].shape, F32)

    _gla_block_prep(r, p[:, QA[0]:QA[1]] * (DK_A ** -0.5), r["ka"][...], p[:, IA[0]:IA[1]], g_a, pos, masks,
                    ("qa", "ka", "ga"), H_A, DK_A, DV_A, OUT_A, c)
    _gla_block_prep(r, p[:, QB[0]:QB[1]] * (DK_B ** -0.5), p[:, KB[0]:KB[1]], p[:, VB[0]:VB[1]], g_b, pos, masks,
                    ("qb", "kb", "gb"), H_B, DK_B, DV_B, OUT_B, c)
    _rwkv_block_prep(r, lw, kk, b, pos, masks, c)

    def body(n, carry):
        rs = pl.ds(pl.multiple_of(n * c, c), c)
        for (qn, kn, gn, st, v, base, heads, dk, dv) in (
                ("qa", "ka", "ga", r["sat"], p[rs, IA[0]:IA[1]], OUT_A, H_A, DK_A, DV_A),
                ("qb", "kb", "gb", r["sbt"], p[rs, VB[0]:VB[1]], OUT_B, H_B, DK_B, DV_B)):
            qx, kx = r[qn][rs, :], r[kn][rs, :]
            decay = r[gn][rs, :][c - 1:c, :]
            for h in range(heads):
                ks = slice(h * dk, (h + 1) * dk)
                vs = slice(h * dv, (h + 1) * dv)
                s = st[h]
                r["o"][rs, base + h * dv:base + (h + 1) * dv] += _dot_nt(qx[:, ks], s)
                st[h] = s * decay[:, ks] + _dot_tn(v[:, vs], kx[:, ks])
        kh, rin, u0, v = r["kh"][rs, :], r["rin"][rs, :], r["u0"][rs, :], r["vc"][rs, :]
        bend, kend = r["bend"][rs, :], r["kend"][rs, :]
        decay = r["elc"][rs, :][c - 1:c, :]
        for h in range(H_C):
            cs = slice(h * N_C, (h + 1) * N_C)
            s = r["sc"][0, h]
            x = _dot_nt(jnp.concatenate([kh[:, cs], rin[:, cs]], axis=0), s)
            u = -(x[0:c] + u0[:, cs])
            r["u"][rs, cs] = u
            r["o"][rs, OUT_C + h * N_C:OUT_C + (h + 1) * N_C] += x[c:2 * c]
            r["sc"][0, h] = s * decay[:, cs] + _dot_tn(
                jnp.concatenate([u, v[:, cs]], axis=0),
                jnp.concatenate([bend[:, cs], kend[:, cs]], axis=0))
        return carry

    lax.fori_loop(0, rows // c, body, 0)

    for h in range(H_C):
        cs = slice(h * N_C, (h + 1) * N_C)
        r["o"][:, OUT_C + h * N_C:OUT_C + (h + 1) * N_C] += jnp.dot(
            r["rb"][h], r["u"][:, cs].astype(BF16), preferred_element_type=F32)

    @pl.when(t == pl.num_programs(1) - 1)
    def _emit():
        for h in range(H_A):
            r["sa"][0, h] = r["sat"][h].T
        for h in range(H_B):
            r["sb"][0, h] = r["sbt"][h].T


def _layer_kernel(names, cfg, *refs):
    r = dict(zip(names, refs))
    rows, tseq = cfg.rows, cfg.tseq
    t = pl.program_id(1)
    p = r["p"]

    @pl.when(t == 0)
    def _init():
        if cfg.has_state:
            r["sa"][...] = r["sa_in"][...]
            r["sb"][...] = r["sb_in"][...]
            r["sc"][...] = r["sc_in"][...]
        else:
            r["sc"][...] = jnp.zeros(r["sc"].shape, F32)
            r["pct"][...] = jnp.zeros(r["pct"].shape, F32)

    x = r["x"][...]
    hn = x * lax.rsqrt(jnp.mean(x * x, axis=-1, keepdims=True) + 1e-6) * r["norm_w"][...]
    p[...] = jnp.dot(hn.astype(BF16), r["w_in"][...], preferred_element_type=F32)

    hl = r["hgrn_lb"][...]
    e = jnp.exp(hl - jnp.max(hl, axis=0, keepdims=True))
    sm = e / jnp.sum(e, axis=0, keepdims=True)
    cum = sm[0:1]
    for j in range(1, cfg.layer + 1):
        cum = cum + sm[j:j + 1]
    lb = cum - sm[0:1]
    za = p[:, ZA[0]:ZA[1]]
    ls = _logsig(za)
    la = jnp.log(jnp.maximum(lb, TINY))
    bb = jnp.log1p(-lb) + ls
    lae = jnp.maximum(la, bb) + jnp.log1p(jnp.exp(-jnp.abs(la - bb)))
    g_a = jnp.where(lb > 0.0, lae, ls)
    r["ka"][...] = (1.0 - lb) * jax.nn.sigmoid(-za)

    g_b = _logsig(_dot_hi(p[:, GKL[0]:GKL[1]], r["gk_w2"][...]) + r["gk_b"][...]) * (1.0 / GLA_NORM)

    pc = p[:, PC[0]:PC[1]]
    row = lax.broadcasted_iota(jnp.int32, (rows, 1), 0)
    first = (row & (tseq - 1)) == 0
    if cfg.has_state:
        prev0 = r["shift_rows"][...]
    else:
        prev0 = r["pct"][SUBLANES - 1:SUBLANES, :]
    prev = jnp.where(first, prev0, pltpu.roll(pc, 1, axis=0))
    r["xs"][...] = pc + (prev - pc) * r["mu"][...]
    if cfg.has_state:
        r["pct"][...] = pc
    else:
        r["pct"][...] = pc[rows - SUBLANES:rows]
    xs = r["xs"]
    kc = xs[:, SH_K[0]:SH_K[1]]
    wlog = -_softplus(-(r["w0"][...] + _dot_hi(jnp.tanh(xs[:, SH_WL[0]:SH_WL[1]]), r["w2"][...]))) - 0.5
    lw = -jnp.exp(wlog)
    av = jax.nn.sigmoid(r["a0"][...] + _dot_hi(xs[:, SH_AL[0]:SH_AL[1]], r["a2"][...]))
    kk = kc * r["k_k"][...]
    kk = kk * lax.rsqrt(jnp.maximum(_segsum(kk * kk, r["m64c"][...]), 1e-24))
    r["rc"][...] = xs[:, SH_R[0]:SH_R[1]]
    r["vc"][...] = xs[:, SH_V[0]:SH_V[1]]
    r["kc"][...] = kc * (1.0 + (av - 1.0) * r["k_a"][...])

    if cfg.has_state:
        _recur_per_sequence(r, p, cfg, g_a, g_b, lw, kk, kk * av)
    else:
        _recur_blocked(r, p, cfg, g_a, g_b, lw, kk, kk * av)

    o = r["o"]
    oa = o[:, OUT_A:OUT_A + W_A]
    ya = (oa * lax.rsqrt(_segsum(oa * oa, r["m64a"][...]) * (1.0 / DV_A) + 1e-5)
          * r["hgrn_norm_w"][...] * _silu(p[:, GATE_A[0]:GATE_A[1]]))
    ob = o[:, OUT_B:OUT_B + W_B]
    yb = (ob * lax.rsqrt(_segsum(ob * ob, r["m96"][...]) * (1.0 / DV_B) + 1e-5)
          * r["gla_norm_w"][...] * _silu(p[:, GATE_B[0]:GATE_B[1]]))
    oc = o[:, OUT_C:D_MODEL]
    m64c = r["m64c"][...]
    dc = oc - _segsum(oc, m64c) * (1.0 / N_C)
    ocn = dc * lax.rsqrt(_segsum(dc * dc, m64c) * (1.0 / N_C) + GN_EPS) * r["ln_w"][...] + r["ln_b"][...]
    bonus = _segsum(r["rc"][...] * r["kc"][...] * r["r_k"][...], m64c) * r["vc"][...]
    yc = (ocn + bonus) * _silu(p[:, GATE_C[0]:GATE_C[1]])
    y = jnp.concatenate([ya, yb, yc], axis=-1)
    out = r["x"][...] + jnp.dot(y.astype(BF16), r["w_out"][...], preferred_element_type=F32)
    if cfg.final:
        out = out * lax.rsqrt(jnp.mean(out * out, axis=-1, keepdims=True) + 1e-6) * r["final_norm_w"][...]
    r["xo"][...] = out


def _block_diag_ones(width, seg):
    i = jnp.arange(width) // seg
    return (i[:, None] == i[None, :]).astype(BF16)


def _run_layer(cfg, nb, nt, x2d, state, params):
    l, rows = cfg.layer, cfg.rows
    nseq = rows // cfg.tseq
    nseq_total = nb * nseq
    assert cfg.has_state == (cfg.tseq == cfg.chunk) and (cfg.has_state or nseq == 1)
    ins, specs = [], []

    def add(name, arr, spec):
        ins.append((name, arr))
        specs.append(spec)

    def layer_row(name, arr):
        w = arr.shape[-1]
        add(name, arr.reshape(DEPTH, 1, w), pl.BlockSpec((None, 1, w), lambda b, t: (l, 0, 0)))

    def layer_mat(name, arr):
        add(name, arr, pl.BlockSpec((None,) + arr.shape[1:], lambda b, t: (l, 0, 0)))

    def whole(name, arr):
        add(name, arr, pl.BlockSpec(arr.shape, lambda b, t: (0,) * arr.ndim))

    add("x", x2d, pl.BlockSpec((rows, D_MODEL), lambda b, t: (b * nt + t, 0)))
    if cfg.has_state:
        st_a, st_b, st_c, shift_rows = state
        add("shift_rows", shift_rows, pl.BlockSpec((None, rows, D_SHIFT), lambda b, t: (l, b, 0)))
        add("sa_in", st_a, pl.BlockSpec((None, nseq, H_A, DK_A, DV_A), lambda b, t: (l, b, 0, 0, 0)))
        add("sb_in", st_b, pl.BlockSpec((None, nseq, H_B, DK_B, DV_B), lambda b, t: (l, b, 0, 0, 0)))
        add("sc_in", st_c, pl.BlockSpec((None, nseq, H_C, N_C, N_C), lambda b, t: (l, b, 0, 0, 0)))
    layer_row("norm_w", params["norm_w"])
    layer_mat("w_in", params["w_in"])
    whole("hgrn_lb", params["hgrn_lb"])
    layer_row("hgrn_norm_w", params["hgrn_norm_w"])
    layer_mat("gk_w2", params["gla_gk_w2"])
    layer_row("gk_b", params["gla_gk_b"])
    layer_row("gla_norm_w", params["gla_norm_w"])
    layer_row("mu", params["rwkv_mu"])
    layer_row("w0", params["rwkv_w0"])
    layer_mat("w2", params["rwkv_w2"])
    layer_row("a0", params["rwkv_a0"])
    layer_mat("a2", params["rwkv_a2"])
    layer_row("k_k", params["rwkv_k_k"])
    layer_row("k_a", params["rwkv_k_a"])
    layer_row("r_k", params["rwkv_r_k"])
    layer_row("ln_w", params["rwkv_ln_w"])
    layer_row("ln_b", params["rwkv_ln_b"])
    layer_mat("w_out", params["w_out"])
    if cfg.final:
        whole("final_norm_w", params["final_norm_w"].reshape(1, D_MODEL))
    whole("m64a", params["m64a"])
    whole("m96", params["m96"])
    whole("m64c", params["m64c"])

    pct_rows = rows if cfg.has_state else SUBLANES
    outs = [
        ("xo", jax.ShapeDtypeStruct(x2d.shape, F32), pl.BlockSpec((rows, D_MODEL), lambda b, t: (b * nt + t, 0))),
        ("sa", jax.ShapeDtypeStruct((nseq_total, H_A, DK_A, DV_A), F32),
         pl.BlockSpec((nseq, H_A, DK_A, DV_A), lambda b, t: (b, 0, 0, 0))),
        ("sb", jax.ShapeDtypeStruct((nseq_total, H_B, DK_B, DV_B), F32),
         pl.BlockSpec((nseq, H_B, DK_B, DV_B), lambda b, t: (b, 0, 0, 0))),
        ("sc", jax.ShapeDtypeStruct((nseq_total, H_C, N_C, N_C), F32),
         pl.BlockSpec((nseq, H_C, N_C, N_C), lambda b, t: (b, 0, 0, 0))),
        ("pct", jax.ShapeDtypeStruct((nb * pct_rows, D_SHIFT), F32),
         pl.BlockSpec((pct_rows, D_SHIFT), lambda b, t: (b, 0))),
    ]
    a_w, b_w = H_A * DK_A, H_B * DK_B
    scratch = [("p", (rows, D_IN), F32), ("xs", (rows, D_SHIFT), F32), ("o", (rows, D_MODEL), F32),
               ("qa", (rows, a_w), F32), ("ka", (rows, a_w), F32), ("ga", (rows, a_w), F32),
               ("qb", (rows, b_w), F32), ("gb", (rows, b_w), F32),
               ("rc", (rows, W_C), F32), ("vc", (rows, W_C), F32), ("kc", (rows, W_C), F32)]
    if cfg.has_state:
        scratch += [(n, (rows, W_C), F32) for n in ("kkc", "bc", "lc", "lwc")]
    else:
        scratch += [("kb", (rows, b_w), F32)]
        scratch += [(n, (rows, W_C), F32) for n in ("elc", "rin", "bend", "kend", "kh", "u0", "u")]
        scratch += [("rb", (H_C, rows, rows), BF16),
                    ("sat", (H_A, DV_A, DK_A), F32), ("sbt", (H_B, DV_B, DK_B), F32)]
    names = tuple(n for n, _ in ins) + tuple(n for n, _, _ in outs) + tuple(n for n, _, _ in scratch)
    return pl.pallas_call(
        functools.partial(_layer_kernel, names, cfg),
        grid=(nb, nt),
        in_specs=specs,
        out_specs=[s for _, _, s in outs],
        out_shape=[s for _, s, _ in outs],
        scratch_shapes=[pltpu.VMEM(shape, dt) for _, shape, dt in scratch],
        compiler_params=pltpu.CompilerParams(
            dimension_semantics=("arbitrary", "arbitrary"),
            vmem_limit_bytes=VMEM_LIMIT_BYTES),
        name=f"layer{l}_{'sample' if cfg.has_state else 'prompt'}",
    )(*[a for _, a in ins])


def _trunk(x, state, params, rows, chunk):
    bsz, tlen, _ = x.shape
    has_state = state is not None
    if has_state:
        assert tlen == chunk and rows % tlen == 0 and (bsz * tlen) % rows == 0
        tseq, nb, nt = tlen, (bsz * tlen) // rows, 1
        st_a, st_b, st_c, st_s = state
        shift_rows = jnp.pad(st_s[:, :, None, :], ((0, 0), (0, 0), (0, tlen - 1), (0, 0)))
        state = (st_a, st_b, st_c, shift_rows.reshape(DEPTH, bsz * tlen, D_SHIFT))
    else:
        assert tlen % rows == 0 and rows % chunk == 0
        tseq, nb, nt = rows, bsz, tlen // rows
    x2d = x.reshape(bsz * tlen, D_MODEL)
    new_a, new_b, new_c, new_s = [], [], [], []
    for l in range(DEPTH):
        cfg = Cfg(layer=l, rows=rows, tseq=tseq, chunk=chunk, has_state=has_state, final=(l == DEPTH - 1))
        x2d, sa, sb, sc, pct = _run_layer(cfg, nb, nt, x2d, state, params)
        new_a.append(sa)
        new_b.append(sb)
        new_c.append(sc)
        if has_state:
            new_s.append(pct.reshape(bsz, tlen, D_SHIFT)[:, tlen - 1])
        else:
            new_s.append(pct.reshape(bsz, SUBLANES, D_SHIFT)[:, SUBLANES - 1])
    return (x2d.reshape(bsz, tlen, D_MODEL), jnp.stack(new_a), jnp.stack(new_b),
            jnp.stack(new_c), jnp.stack(new_s))


PROMPT_ROWS = 256
SAMPLE_ROWS = 32


def kernel(x_prompt, x_sample, state_hgrn, state_gla, state_rwkv, state_shift, norm_w, w_in, hgrn_lb, hgrn_norm_w, gla_gk_w2, gla_gk_b, gla_norm_w, rwkv_mu, rwkv_w0, rwkv_w2, rwkv_a0, rwkv_a2, rwkv_k_k, rwkv_k_a, rwkv_r_k, rwkv_ln_w, rwkv_ln_b, w_out, final_norm_w):
    params = dict(
        norm_w=norm_w, w_in=w_in.astype(BF16), hgrn_lb=hgrn_lb, hgrn_norm_w=hgrn_norm_w,
        gla_gk_w2=gla_gk_w2, gla_gk_b=gla_gk_b, gla_norm_w=gla_norm_w, rwkv_mu=rwkv_mu,
        rwkv_w0=rwkv_w0, rwkv_w2=rwkv_w2, rwkv_a0=rwkv_a0, rwkv_a2=rwkv_a2, rwkv_k_k=rwkv_k_k,
        rwkv_k_a=rwkv_k_a, rwkv_r_k=rwkv_r_k, rwkv_ln_w=rwkv_ln_w, rwkv_ln_b=rwkv_ln_b,
        w_out=w_out.astype(BF16), final_norm_w=final_norm_w,
        m64a=_block_diag_ones(W_A, DV_A), m96=_block_diag_ones(W_B, DV_B),
        m64c=_block_diag_ones(W_C, N_C),
    )
    y_p, hgrn_p, gla_p, rwkv_p, shift_p = _trunk(x_prompt, None, params, PROMPT_ROWS, CHUNK)
    y_s, hgrn_s, gla_s, rwkv_s, shift_s = _trunk(
        x_sample, (state_hgrn, state_gla, state_rwkv, state_shift), params, SAMPLE_ROWS,
        x_sample.shape[1])
    return (y_p, y_s, hgrn_p, gla_p, rwkv_p, shift_p, hgrn_s, gla_s, rwkv_s, shift_s)
```

```python
import collections
import functools

import jax
import jax.numpy as jnp
from jax import lax
from jax.experimental import pallas as pl
from jax.experimental.pallas import tpu as pltpu

F32 = jnp.float32
BF16 = jnp.bfloat16

D_MODEL = 1024
DEPTH = 4
HEAD = 64
W_A = D_MODEL // 4
H_A = W_A // HEAD
DK_A = HEAD
DV_A = W_A // H_A
W_B = (D_MODEL - W_A) // 2
H_B = 4
DV_B = W_B // H_B
DK_B = DV_B // 2
GLA_LR = 16
GLA_NORM = 16.0
W_C = D_MODEL - W_A - W_B
H_C = W_C // HEAD
N_C = HEAD
DECAY_LR = 32
AAA_LR = 32
GN_EPS = 64e-5
CHUNK = 16
TINY = 1e-30
D_A_IN = 2 * H_A * DK_A + 2 * W_A
D_B_IN = 2 * H_B * DK_B + 2 * W_B + GLA_LR
D_SHIFT = 3 * W_C + DECAY_LR + AAA_LR
D_IN = D_A_IN + D_B_IN + D_SHIFT + W_C

QA = (0, H_A * DK_A)
ZA = (QA[1], QA[1] + H_A * DK_A)
IA = (ZA[1], ZA[1] + W_A)
GATE_A = (IA[1], IA[1] + W_A)
QB = (D_A_IN, D_A_IN + H_B * DK_B)
KB = (QB[1], QB[1] + H_B * DK_B)
VB = (KB[1], KB[1] + W_B)
GKL = (VB[1], VB[1] + GLA_LR)
GATE_B = (GKL[1], GKL[1] + W_B)
PC = (D_A_IN + D_B_IN, D_A_IN + D_B_IN + D_SHIFT)
GATE_C = (PC[1], PC[1] + W_C)
SH_R = (0, W_C)
SH_WL = (SH_R[1], SH_R[1] + DECAY_LR)
SH_K = (SH_WL[1], SH_WL[1] + W_C)
SH_V = (SH_K[1], SH_K[1] + W_C)
SH_AL = (SH_V[1], SH_V[1] + AAA_LR)
OUT_A = 0
OUT_B = W_A
OUT_C = W_A + W_B

SUBLANES = 8
VMEM_LIMIT_BYTES = 52 * 1024 * 1024

Cfg = collections.namedtuple("Cfg", "layer rows tseq chunk has_state final")


def _dot(a, b):
    return jnp.dot(a.astype(BF16), b.astype(BF16), preferred_element_type=F32)


def _dot_nt(a, b):
    return lax.dot_general(a.astype(BF16), b.astype(BF16), (((1,), (1,)), ((), ())),
                           preferred_element_type=F32)


def _dot_tn(a, b):
    return lax.dot_general(a.astype(BF16), b.astype(BF16), (((0,), (0,)), ((), ())),
                           preferred_element_type=F32)


def _split(a):
    hi = a.astype(BF16)
    return hi, (a - hi.astype(F32)).astype(BF16)


def _dot3s(a_split, b_split, dims):
    (ah, al), (bh, bl) = a_split, b_split
    return (lax.dot_general(ah, bh, dims, preferred_element_type=F32)
            + lax.dot_general(ah, bl, dims, preferred_element_type=F32)
            + lax.dot_general(al, bh, dims, preferred_element_type=F32))


def _dot3(a, b, dims):
    return _dot3s(_split(a), _split(b), dims)


def _dot_nt3(a, b):
    return _dot3(a, b, (((1,), (1,)), ((), ())))


def _dot_nn3(a, b):
    return _dot3(a, b, (((1,), (0,)), ((), ())))


def _dot_hi(a, b):
    return jnp.dot(a, b, precision=lax.Precision.HIGHEST, preferred_element_type=F32)


def _segsum(x, m):
    hi, lo = _split(x)
    return (jnp.dot(hi, m, preferred_element_type=F32)
            + jnp.dot(lo, m, preferred_element_type=F32))


def _softplus(x):
    return jnp.maximum(x, 0.0) + jnp.log1p(jnp.exp(-jnp.abs(x)))


def _logsig(x):
    return jnp.minimum(x, 0.0) - jnp.log1p(jnp.exp(-jnp.abs(x)))


def _silu(x):
    return x * jax.nn.sigmoid(x)


def _chunk_scan(x, chunk, pos, reverse=False):
    n = x.shape[0]
    s = 1
    while s < chunk:
        if reverse:
            x = x + jnp.where(pos < chunk - s, pltpu.roll(x, n - s, axis=0), 0.0)
        else:
            x = x + jnp.where(pos >= s, pltpu.roll(x, s, axis=0), 0.0)
        s *= 2
    return x


def _tri_mask(c, strict):
    i = lax.broadcasted_iota(jnp.int32, (c, c), 0)
    j = lax.broadcasted_iota(jnp.int32, (c, c), 1)
    return (j < i) if strict else (j <= i)


def _inv_unit_lower(a, eye, c, matmul):
    w = eye - a
    p = a
    n = 2
    while n < c:
        p = matmul(p, p)
        w = w + matmul(w, p)
        n *= 2
    return w


def _column(row, width):
    return jnp.broadcast_to(row, (SUBLANES, width)).T[:, 0:1]


def _gla_chunk(q, k, v, g, s_ref, seq, heads, dk, dv, c):
    mid = c // 2 - 1
    g_mid = g[mid:mid + 1]
    g_last = g[c - 1:c]
    q_rel = q * jnp.exp(g - g_mid)
    k_rel = k * jnp.exp(g_mid - g)
    q_abs = q * jnp.exp(g)
    k_end = k * jnp.exp(g_last - g)
    decay_col = _column(jnp.exp(g_last), heads * dk)
    causal = _tri_mask(c, False)
    outs = []
    for h in range(heads):
        ks = slice(h * dk, (h + 1) * dk)
        vs = slice(h * dv, (h + 1) * dv)
        s = s_ref[seq, h]
        a = jnp.where(causal, _dot_nt3(q_rel[:, ks], k_rel[:, ks]), 0.0)
        outs.append(_dot(a, v[:, vs]) + _dot(q_abs[:, ks], s))
        s_ref[seq, h] = s * decay_col[ks] + _dot_tn(k_end[:, ks], v[:, vs])
    return outs


def _rwkv_chunk(r, k, v, kk, b, lc, lw, s_ref, seq, c):
    l_last = lc[c - 1:c]
    inv_cum = jnp.exp(-lc)
    kk_in = kk * jnp.exp(lc - lw)
    b_out = b * inv_cum
    k_out = k * inv_cum
    r_in = r * jnp.exp(lc)
    to_end = jnp.exp(l_last - lc)
    b_end = b * to_end
    k_end = k * to_end
    decay_row = jnp.exp(l_last)
    strict = _tri_mask(c, True)
    causal = _tri_mask(c, False)
    eye = jnp.where(causal & ~strict, 1.0, 0.0)
    outs = []
    for h in range(H_C):
        cs = slice(h * N_C, (h + 1) * N_C)
        s = s_ref[seq, h]
        vh = v[:, cs]
        a = jnp.where(strict, _dot_nt3(kk_in[:, cs], b_out[:, cs]), 0.0)
        bm = jnp.where(strict, _dot_nt3(kk_in[:, cs], k_out[:, cs]), 0.0)
        rhs = _dot_nt(kk_in[:, cs], s) + _dot(bm, vh)
        u = -_dot_hi(_inv_unit_lower(a, eye, c, _dot_hi), rhs)
        rb = jnp.where(causal, _dot_nt3(r_in[:, cs], b_out[:, cs]), 0.0)
        rk = jnp.where(causal, _dot_nt3(r_in[:, cs], k_out[:, cs]), 0.0)
        outs.append(_dot_nt(r_in[:, cs], s) + _dot(rb, u) + _dot(rk, vh))
        s_ref[seq, h] = s * decay_row[:, cs] + _dot_tn(u, b_end[:, cs]) + _dot_tn(vh, k_end[:, cs])
    return outs


def _recur_per_sequence(r, p, cfg, g_a, g_b, lw, kk, b):
    rows, c = cfg.rows, cfg.chunk
    pos = lax.broadcasted_iota(jnp.int32, (rows, 1), 0) & (c - 1)
    r["ga"][...] = _chunk_scan(g_a, c, pos)
    r["gb"][...] = _chunk_scan(g_b, c, pos)
    r["lc"][...] = _chunk_scan(lw, c, pos)
    r["lwc"][...] = lw
    r["kkc"][...] = kk
    r["bc"][...] = b
    r["qa"][...] = p[:, QA[0]:QA[1]] * (DK_A ** -0.5)
    r["qb"][...] = p[:, QB[0]:QB[1]] * (DK_B ** -0.5)
    rpi = max(c, SUBLANES)
    spi = rpi // c

    def body(n, carry):
        rs = pl.ds(pl.multiple_of(n * rpi, rpi), rpi)
        qa, ka, ga, va = r["qa"][rs, :], r["ka"][rs, :], r["ga"][rs, :], p[rs, IA[0]:IA[1]]
        qb, kb, gb, vb = r["qb"][rs, :], p[rs, KB[0]:KB[1]], r["gb"][rs, :], p[rs, VB[0]:VB[1]]
        rc, kcm, vc, kkc = r["rc"][rs, :], r["kc"][rs, :], r["vc"][rs, :], r["kkc"][rs, :]
        bc, lc, lwc = r["bc"][rs, :], r["lc"][rs, :], r["lwc"][rs, :]
        oa, ob, oc = [], [], []
        for j in range(spi):
            sl = slice(j * c, (j + 1) * c)
            seq = n * spi + j
            oa.append(_gla_chunk(qa[sl], ka[sl], va[sl], ga[sl], r["sa"], seq, H_A, DK_A, DV_A, c))
            ob.append(_gla_chunk(qb[sl], kb[sl], vb[sl], gb[sl], r["sb"], seq, H_B, DK_B, DV_B, c))
            oc.append(_rwkv_chunk(rc[sl], kcm[sl], vc[sl], kkc[sl], bc[sl], lc[sl], lwc[sl],
                                  r["sc"], seq, c))
        for outs, base, heads, dv in ((oa, OUT_A, H_A, DV_A), (ob, OUT_B, H_B, DV_B), (oc, OUT_C, H_C, N_C)):
            for h in range(heads):
                tile = outs[0][h] if spi == 1 else jnp.concatenate([o[h] for o in outs], axis=0)
                r["o"][rs, base + h * dv:base + (h + 1) * dv] = tile
        return carry

    lax.fori_loop(0, rows // rpi, body, 0)


def _store_chunk_columns(kt_ref, lane0, k_end, c):
    for ch in range(k_end.shape[0] // c):
        kt_ref[ch, :, lane0:lane0 + c] = k_end[ch * c:(ch + 1) * c, :].T


def _store_chunk_decay(d_ref, decay, c):
    for ch in range(decay.shape[0] // c):
        col = decay[(ch + 1) * c - SUBLANES:(ch + 1) * c, :].T[:, SUBLANES - 1:SUBLANES]
        d_ref[ch] = jnp.broadcast_to(col, d_ref.shape[1:])


def _gla_block_prep(r, q, k, v, g, pos, masks, names, heads, dk, dv, out_base, c):
    qx_name, kt_name, d_name = names
    causal = masks[0]
    mid = c // 2 - 1
    after = jnp.where(pos > mid, g, 0.0)
    upto = jnp.where(pos <= mid, g, 0.0)
    rel = _chunk_scan(after, c, pos) - (_chunk_scan(upto, c, pos, reverse=True) - upto)
    q_rel = q * jnp.exp(rel)
    k_rel = k * jnp.exp(-rel)
    eg = jnp.exp(_chunk_scan(g, c, pos))
    r[qx_name][...] = q * eg
    k_end = k * jnp.exp(_chunk_scan(g, c, pos, reverse=True) - g)
    _store_chunk_columns(r[kt_name], 0, k_end.astype(BF16), c)
    _store_chunk_decay(r[d_name], eg, c)
    for h in range(heads):
        ks = slice(h * dk, (h + 1) * dk)
        sc = jnp.where(causal, _dot_nt3(q_rel[:, ks], k_rel[:, ks]), 0.0)
        r["o"][:, out_base + h * dv:out_base + (h + 1) * dv] = _dot(sc, v[:, h * dv:(h + 1) * dv])


def _rwkv_block_prep(r, lw, kk, b, pos, masks, c):
    causal, strict, eye = masks
    rr, kmod, v = r["rc"][...], r["kc"][...], r["vc"][...]
    lc = _chunk_scan(lw, c, pos)
    el = jnp.exp(lc)
    inv_cum = jnp.exp(-lc)
    kk_in = kk * jnp.exp(lc - lw)
    b_out = b * inv_cum
    k_out = kmod * inv_cum
    r_in = rr * el
    to_end = jnp.exp(_chunk_scan(lw, c, pos, reverse=True) - lw)
    _store_chunk_columns(r["bkt"], 0, (b * to_end).astype(BF16), c)
    _store_chunk_columns(r["bkt"], c, (kmod * to_end).astype(BF16), c)
    _store_chunk_decay(r["dc"], el, c)
    for ch in range(lw.shape[0] // c):
        r["khr"][ch, c:2 * c, :] = r_in[ch * c:(ch + 1) * c, :]
    cols = [slice(h * N_C, (h + 1) * N_C) for h in range(H_C)]
    nn = (((1,), (0,)), ((), ()))
    pw = [jnp.where(strict, _dot_nt3(kk_in[:, cs], b_out[:, cs]), 0.0) for cs in cols]
    w = [eye - a for a in pw]
    n = 2
    while n < c:
        pw_s = [_split(x) for x in pw]
        pw = [_dot3s(s, s, nn) for s in pw_s]
        w = [x + _dot3s(_split(x), _split(y), nn) for x, y in zip(w, pw)]
        n *= 2
    w_s = [_split(x) for x in w]
    for h, cs in enumerate(cols):
        kh = _dot3s(w_s[h], _split(kk_in[:, cs]), nn)
        for ch in range(lw.shape[0] // c):
            r["khr"][ch, 0:c, cs] = kh[ch * c:(ch + 1) * c, :]
    bv = [_dot(jnp.where(strict, _dot_nt3(kk_in[:, cs], k_out[:, cs]), 0.0), v[:, cs]) for cs in cols]
    for h, cs in enumerate(cols):
        r["u0"][:, cs] = _dot3s(w_s[h], _split(bv[h]), nn)
    for h, cs in enumerate(cols):
        r["rb"][h] = jnp.where(causal, _dot_nt3(r_in[:, cs], b_out[:, cs]), 0.0).astype(BF16)
        rk = jnp.where(causal, _dot_nt3(r_in[:, cs], k_out[:, cs]), 0.0)
        r["o"][:, OUT_C + h * N_C:OUT_C + (h + 1) * N_C] = _dot(rk, v[:, cs])


def _recur_blocked(r, p, cfg, g_a, g_b, lw, kk, b):
    rows, c, tb = cfg.rows, cfg.chunk, cfg.tseq
    nseq = rows // tb
    steps = tb // c
    t = pl.program_id(1)
    pos = lax.broadcasted_iota(jnp.int32, (rows, 1), 0) & (c - 1)
    shift = c.bit_length() - 1
    i = lax.broadcasted_iota(jnp.int32, (rows, rows), 0)
    j = lax.broadcasted_iota(jnp.int32, (rows, rows), 1)
    same = (i >> shift) == (j >> shift)
    masks = (same & (j <= i), same & (j < i), jnp.where(i == j, 1.0, 0.0))

    @pl.when(t == 0)
    def _init():
        r["sa"][...] = jnp.zeros(r["sa"].shape, F32)
        r["sb"][...] = jnp.zeros(r["sb"].shape, F32)
        r["sct"][...] = jnp.zeros(r["sct"].shape, F32)

    _gla_block_prep(r, p[:, QA[0]:QA[1]] * (DK_A ** -0.5), r["ka"][...], p[:, IA[0]:IA[1]], g_a, pos, masks,
                    ("qa", "kta", "da"), H_A, DK_A, DV_A, OUT_A, c)
    _gla_block_prep(r, p[:, QB[0]:QB[1]] * (DK_B ** -0.5), p[:, KB[0]:KB[1]], p[:, VB[0]:VB[1]], g_b, pos, masks,
                    ("qb", "ktb", "db"), H_B, DK_B, DV_B, OUT_B, c)
    _rwkv_block_prep(r, lw, kk, b, pos, masks, c)

    def body(n, carry):
        for q in range(nseq):
            ch = q * steps + n
            rs = pl.ds(pl.multiple_of(q * tb + n * c, c), c)
            for (qn, kn, dn, st, v, base, heads, dk, dv) in (
                    ("qa", "kta", "da", r["sa"], p[rs, IA[0]:IA[1]], OUT_A, H_A, DK_A, DV_A),
                    ("qb", "ktb", "db", r["sb"], p[rs, VB[0]:VB[1]], OUT_B, H_B, DK_B, DV_B)):
                qx, kt, d = r[qn][rs, :], r[kn][ch], r[dn][ch]
                for h in range(heads):
                    ks = slice(h * dk, (h + 1) * dk)
                    vs = slice(h * dv, (h + 1) * dv)
                    s = st[q, h]
                    r["oi"][rs, base + h * dv:base + (h + 1) * dv] = _dot(qx[:, ks], s)
                    st[q, h] = s * d[ks, 0:dv] + jnp.dot(kt[ks, :], v[:, vs].astype(BF16),
                                                         preferred_element_type=F32)
            khr, bkt, d = r["khr"][ch], r["bkt"][ch], r["dc"][ch]
            u0, v = r["u0"][rs, :], r["vc"][rs, :]
            for h in range(H_C):
                cs = slice(h * N_C, (h + 1) * N_C)
                s = r["sct"][q, h]
                x = _dot(khr[:, cs], s)
                u = -(x[0:c] + u0[:, cs])
                r["u"][rs, cs] = u
                r["oi"][rs, OUT_C + h * N_C:OUT_C + (h + 1) * N_C] = x[c:2 * c]
                r["sct"][q, h] = s * d[cs, 0:N_C] + jnp.dot(
                    bkt[cs, :], jnp.concatenate([u, v[:, cs]], axis=0).astype(BF16),
                    preferred_element_type=F32)
        return carry

    lax.fori_loop(0, steps, body, 0)

    r["o"][...] += r["oi"][...]
    for h in range(H_C):
        cs = slice(h * N_C, (h + 1) * N_C)
        r["o"][:, OUT_C + h * N_C:OUT_C + (h + 1) * N_C] += jnp.dot(
            r["rb"][h], r["u"][:, cs].astype(BF16), preferred_element_type=F32)

    @pl.when(t == pl.num_programs(1) - 1)
    def _emit():
        for q in range(nseq):
            for h in range(H_C):
                r["sc"][q, h] = r["sct"][q, h].T


def _layer_kernel(names, cfg, *refs):
    r = dict(zip(names, refs))
    rows, tseq = cfg.rows, cfg.tseq
    t = pl.program_id(1)
    p = r["p"]

    @pl.when(t == 0)
    def _init():
        if cfg.has_state:
            r["sa"][...] = r["sa_in"][...]
            r["sb"][...] = r["sb_in"][...]
            r["sc"][...] = r["sc_in"][...]
        else:
            r["pct"][...] = jnp.zeros(r["pct"].shape, F32)

    x = r["x"][...].reshape(rows, D_MODEL)
    hn = x * lax.rsqrt(jnp.mean(x * x, axis=-1, keepdims=True) + 1e-6) * r["norm_w"][...]
    p[...] = jnp.dot(hn.astype(BF16), r["w_in"][...], preferred_element_type=F32)

    hl = r["hgrn_lb"][...]
    e = jnp.exp(hl - jnp.max(hl, axis=0, keepdims=True))
    sm = e / jnp.sum(e, axis=0, keepdims=True)
    cum = sm[0:1]
    for j in range(1, cfg.layer + 1):
        cum = cum + sm[j:j + 1]
    lb = cum - sm[0:1]
    za = p[:, ZA[0]:ZA[1]]
    ls = _logsig(za)
    la = jnp.log(jnp.maximum(lb, TINY))
    bb = jnp.log1p(-lb) + ls
    lae = jnp.maximum(la, bb) + jnp.log1p(jnp.exp(-jnp.abs(la - bb)))
    g_a = jnp.where(lb > 0.0, lae, ls)
    r["ka"][...] = (1.0 - lb) * jax.nn.sigmoid(-za)

    g_b = _logsig(_dot_hi(p[:, GKL[0]:GKL[1]], r["gk_w2"][...]) + r["gk_b"][...]) * (1.0 / GLA_NORM)

    pc = p[:, PC[0]:PC[1]]
    xs = r["xs"]
    if cfg.has_state:
        row = lax.broadcasted_iota(jnp.int32, (rows, 1), 0)
        prev = jnp.where((row & (tseq - 1)) == 0, r["shift_rows"][...], pltpu.roll(pc, 1, axis=0))
        r["pct"][...] = pc
    else:
        xs[...] = pltpu.roll(pc, 1, axis=0)
        for q in range(rows // tseq):
            xs[q * tseq:q * tseq + 1, :] = r["pct"][(q + 1) * SUBLANES - 1:(q + 1) * SUBLANES, :]
            r["pct"][q * SUBLANES:(q + 1) * SUBLANES, :] = p[(q + 1) * tseq - SUBLANES:(q + 1) * tseq, PC[0]:PC[1]]
        prev = xs[...]
    xs[...] = pc + (prev - pc) * r["mu"][...]
    kc = xs[:, SH_K[0]:SH_K[1]]
    wlog = -_softplus(-(r["w0"][...] + _dot_hi(jnp.tanh(xs[:, SH_WL[0]:SH_WL[1]]), r["w2"][...]))) - 0.5
    lw = -jnp.exp(wlog)
    av = jax.nn.sigmoid(r["a0"][...] + _dot_hi(xs[:, SH_AL[0]:SH_AL[1]], r["a2"][...]))
    kk = kc * r["k_k"][...]
    kk = kk * lax.rsqrt(jnp.maximum(_segsum(kk * kk, r["m64c"][...]), 1e-24))
    r["rc"][...] = xs[:, SH_R[0]:SH_R[1]]
    r["vc"][...] = xs[:, SH_V[0]:SH_V[1]]
    r["kc"][...] = kc * (1.0 + (av - 1.0) * r["k_a"][...])

    if cfg.has_state:
        _recur_per_sequence(r, p, cfg, g_a, g_b, lw, kk, kk * av)
    else:
        _recur_blocked(r, p, cfg, g_a, g_b, lw, kk, kk * av)

    o = r["o"]
    oa = o[:, OUT_A:OUT_A + W_A]
    ya = (oa * lax.rsqrt(_segsum(oa * oa, r["m64a"][...]) * (1.0 / DV_A) + 1e-5)
          * r["hgrn_norm_w"][...] * _silu(p[:, GATE_A[0]:GATE_A[1]]))
    ob = o[:, OUT_B:OUT_B + W_B]
    yb = (ob * lax.rsqrt(_segsum(ob * ob, r["m96"][...]) * (1.0 / DV_B) + 1e-5)
          * r["gla_norm_w"][...] * _silu(p[:, GATE_B[0]:GATE_B[1]]))
    oc = o[:, OUT_C:D_MODEL]
    m64c = r["m64c"][...]
    dc = oc - _segsum(oc, m64c) * (1.0 / N_C)
    ocn = dc * lax.rsqrt(_segsum(dc * dc, m64c) * (1.0 / N_C) + GN_EPS) * r["ln_w"][...] + r["ln_b"][...]
    bonus = _segsum(r["rc"][...] * r["kc"][...] * r["r_k"][...], m64c) * r["vc"][...]
    yc = (ocn + bonus) * _silu(p[:, GATE_C[0]:GATE_C[1]])
    y = jnp.concatenate([ya, yb, yc], axis=-1)
    out = r["x"][...].reshape(rows, D_MODEL) + jnp.dot(y.astype(BF16), r["w_out"][...], preferred_element_type=F32)
    if cfg.final:
        out = out * lax.rsqrt(jnp.mean(out * out, axis=-1, keepdims=True) + 1e-6) * r["final_norm_w"][...]
    r["xo"][...] = out.reshape(r["xo"].shape)


def _block_diag_ones(width, seg):
    i = jnp.arange(width) // seg
    return (i[:, None] == i[None, :]).astype(BF16)


def _run_layer(cfg, nb, nt, x3, state, params):
    l, rows = cfg.layer, cfg.rows
    nseq = rows // cfg.tseq
    nseq_total = nb * nseq
    assert not cfg.has_state or cfg.tseq == cfg.chunk
    x_block = (1, rows, D_MODEL) if cfg.has_state else (nseq, cfg.tseq, D_MODEL)
    ins, specs = [], []

    def add(name, arr, spec):
        ins.append((name, arr))
        specs.append(spec)

    def layer_row(name, arr):
        w = arr.shape[-1]
        add(name, arr.reshape(DEPTH, 1, w), pl.BlockSpec((None, 1, w), lambda b, t: (l, 0, 0)))

    once = pl.Buffered(1)

    def layer_mat(name, arr):
        add(name, arr, pl.BlockSpec((None,) + arr.shape[1:], lambda b, t: (l, 0, 0), pipeline_mode=once))

    def whole(name, arr):
        add(name, arr, pl.BlockSpec(arr.shape, lambda b, t: (0,) * arr.ndim, pipeline_mode=once))

    add("x", x3, pl.BlockSpec(x_block, lambda b, t: (b, t, 0)))
    if cfg.has_state:
        st_a, st_b, st_c, shift_rows = state
        add("shift_rows", shift_rows, pl.BlockSpec((None, rows, D_SHIFT), lambda b, t: (l, b, 0)))
        add("sa_in", st_a, pl.BlockSpec((None, nseq, H_A, DK_A, DV_A), lambda b, t: (l, b, 0, 0, 0)))
        add("sb_in", st_b, pl.BlockSpec((None, nseq, H_B, DK_B, DV_B), lambda b, t: (l, b, 0, 0, 0)))
        add("sc_in", st_c, pl.BlockSpec((None, nseq, H_C, N_C, N_C), lambda b, t: (l, b, 0, 0, 0)))
    layer_row("norm_w", params["norm_w"])
    layer_mat("w_in", params["w_in"])
    whole("hgrn_lb", params["hgrn_lb"])
    layer_row("hgrn_norm_w", params["hgrn_norm_w"])
    layer_mat("gk_w2", params["gla_gk_w2"])
    layer_row("gk_b", params["gla_gk_b"])
    layer_row("gla_norm_w", params["gla_norm_w"])
    layer_row("mu", params["rwkv_mu"])
    layer_row("w0", params["rwkv_w0"])
    layer_mat("w2", params["rwkv_w2"])
    layer_row("a0", params["rwkv_a0"])
    layer_mat("a2", params["rwkv_a2"])
    layer_row("k_k", params["rwkv_k_k"])
    layer_row("k_a", params["rwkv_k_a"])
    layer_row("r_k", params["rwkv_r_k"])
    layer_row("ln_w", params["rwkv_ln_w"])
    layer_row("ln_b", params["rwkv_ln_b"])
    layer_mat("w_out", params["w_out"])
    if cfg.final:
        whole("final_norm_w", params["final_norm_w"].reshape(1, D_MODEL))
    whole("m64a", params["m64a"])
    whole("m96", params["m96"])
    whole("m64c", params["m64c"])

    pct_rows = rows if cfg.has_state else nseq * SUBLANES
    outs = [
        ("xo", jax.ShapeDtypeStruct(x3.shape, F32), pl.BlockSpec(x_block, lambda b, t: (b, t, 0))),
        ("sa", jax.ShapeDtypeStruct((nseq_total, H_A, DK_A, DV_A), F32),
         pl.BlockSpec((nseq, H_A, DK_A, DV_A), lambda b, t: (b, 0, 0, 0))),
        ("sb", jax.ShapeDtypeStruct((nseq_total, H_B, DK_B, DV_B), F32),
         pl.BlockSpec((nseq, H_B, DK_B, DV_B), lambda b, t: (b, 0, 0, 0))),
        ("sc", jax.ShapeDtypeStruct((nseq_total, H_C, N_C, N_C), F32),
         pl.BlockSpec((nseq, H_C, N_C, N_C), lambda b, t: (b, 0, 0, 0))),
        ("pct", jax.ShapeDtypeStruct((nb * pct_rows, D_SHIFT), F32),
         pl.BlockSpec((pct_rows, D_SHIFT), lambda b, t: (b, 0))),
    ]
    a_w, b_w, c = H_A * DK_A, H_B * DK_B, cfg.chunk
    nch = rows // c
    scratch = [("p", (rows, D_IN), F32), ("xs", (rows, D_SHIFT), F32), ("o", (rows, D_MODEL), F32),
               ("qa", (rows, a_w), F32), ("ka", (rows, a_w), F32), ("qb", (rows, b_w), F32),
               ("rc", (rows, W_C), F32), ("vc", (rows, W_C), F32), ("kc", (rows, W_C), F32)]
    if cfg.has_state:
        scratch += [("ga", (rows, a_w), F32), ("gb", (rows, b_w), F32)]
        scratch += [(n, (rows, W_C), F32) for n in ("kkc", "bc", "lc", "lwc")]
    else:
        lanes = 128
        scratch += [("oi", (rows, D_MODEL), F32), ("u0", (rows, W_C), F32), ("u", (rows, W_C), F32),
                    ("kta", (nch, a_w, c), BF16), ("da", (nch, a_w, lanes), F32),
                    ("ktb", (nch, b_w, c), BF16), ("db", (nch, b_w, lanes), F32),
                    ("bkt", (nch, W_C, 2 * c), BF16), ("dc", (nch, W_C, lanes), F32),
                    ("khr", (nch, 2 * c, W_C), F32), ("rb", (H_C, rows, rows), BF16),
                    ("sct", (nseq, H_C, N_C, N_C), F32)]
    names = tuple(n for n, _ in ins) + tuple(n for n, _, _ in outs) + tuple(n for n, _, _ in scratch)
    return pl.pallas_call(
        functools.partial(_layer_kernel, names, cfg),
        grid=(nb, nt),
        in_specs=specs,
        out_specs=[s for _, _, s in outs],
        out_shape=[s for _, s, _ in outs],
        scratch_shapes=[pltpu.VMEM(shape, dt) for _, shape, dt in scratch],
        compiler_params=pltpu.CompilerParams(
            dimension_semantics=("arbitrary", "arbitrary"),
            vmem_limit_bytes=VMEM_LIMIT_BYTES),
        name=f"layer{l}_{'sample' if cfg.has_state else 'prompt'}",
    )(*[a for _, a in ins])


def _trunk(x, state, params, rows, tseq, chunk):
    bsz, tlen, _ = x.shape
    has_state = state is not None
    nseq = rows // tseq
    assert rows % tseq == 0 and tseq % chunk == 0 and bsz % nseq == 0 and tlen % tseq == 0
    nb, nt = bsz // nseq, tlen // tseq
    if has_state:
        assert tlen == tseq == chunk
        st_a, st_b, st_c, st_s = state
        shift_rows = jnp.pad(st_s[:, :, None, :], ((0, 0), (0, 0), (0, tlen - 1), (0, 0)))
        state = (st_a, st_b, st_c, shift_rows.reshape(DEPTH, bsz * tlen, D_SHIFT))
        x3 = x.reshape(nb, rows, D_MODEL)
    else:
        x3 = x
    new_a, new_b, new_c, new_s = [], [], [], []
    for l in range(DEPTH):
        cfg = Cfg(layer=l, rows=rows, tseq=tseq, chunk=chunk, has_state=has_state, final=(l == DEPTH - 1))
        x3, sa, sb, sc, pct = _run_layer(cfg, nb, nt, x3, state, params)
        new_a.append(sa)
        new_b.append(sb)
        new_c.append(sc)
        if has_state:
            new_s.append(pct.reshape(bsz, tlen, D_SHIFT)[:, tlen - 1])
        else:
            new_s.append(pct.reshape(bsz, SUBLANES, D_SHIFT)[:, SUBLANES - 1])
    return (x3.reshape(bsz, tlen, D_MODEL), jnp.stack(new_a), jnp.stack(new_b),
            jnp.stack(new_c), jnp.stack(new_s))


PROMPT_ROWS = 256
PROMPT_TSEQ = 32
SAMPLE_ROWS = 32


def kernel(x_prompt, x_sample, state_hgrn, state_gla, state_rwkv, state_shift, norm_w, w_in, hgrn_lb, hgrn_norm_w, gla_gk_w2, gla_gk_b, gla_norm_w, rwkv_mu, rwkv_w0, rwkv_w2, rwkv_a0, rwkv_a2, rwkv_k_k, rwkv_k_a, rwkv_r_k, rwkv_ln_w, rwkv_ln_b, w_out, final_norm_w):
    params = dict(
        norm_w=norm_w, w_in=w_in.astype(BF16), hgrn_lb=hgrn_lb, hgrn_norm_w=hgrn_norm_w,
        gla_gk_w2=gla_gk_w2, gla_gk_b=gla_gk_b, gla_norm_w=gla_norm_w, rwkv_mu=rwkv_mu,
        rwkv_w0=rwkv_w0, rwkv_w2=rwkv_w2, rwkv_a0=rwkv_a0, rwkv_a2=rwkv_a2, rwkv_k_k=rwkv_k_k,
        rwkv_k_a=rwkv_k_a, rwkv_r_k=rwkv_r_k, rwkv_ln_w=rwkv_ln_w, rwkv_ln_b=rwkv_ln_b,
        w_out=w_out.astype(BF16), final_norm_w=final_norm_w,
        m64a=_block_diag_ones(W_A, DV_A), m96=_block_diag_ones(W_B, DV_B),
        m64c=_block_diag_ones(W_C, N_C),
    )
    y_p, hgrn_p, gla_p, rwkv_p, shift_p = _trunk(x_prompt, None, params, PROMPT_ROWS, PROMPT_TSEQ, CHUNK)
    y_s, hgrn_s, gla_s, rwkv_s, shift_s = _trunk(
        x_sample, (state_hgrn, state_gla, state_rwkv, state_shift), params, SAMPLE_ROWS,
        x_sample.shape[1], x_sample.shape[1])
    return (y_p, y_s, hgrn_p, gla_p, rwkv_p, shift_p, hgrn_s, gla_s, rwkv_s, shift_s)
```

```python
import collections
import functools

import jax
import jax.numpy as jnp
from jax import lax
from jax.experimental import pallas as pl
from jax.experimental.pallas import tpu as pltpu

F32 = jnp.float32
BF16 = jnp.bfloat16

D_MODEL = 1024
DEPTH = 4
HEAD = 64
W_A = D_MODEL // 4
H_A = W_A // HEAD
DK_A = HEAD
DV_A = W_A // H_A
W_B = (D_MODEL - W_A) // 2
H_B = 4
DV_B = W_B // H_B
DK_B = DV_B // 2
GLA_LR = 16
GLA_NORM = 16.0
W_C = D_MODEL - W_A - W_B
H_C = W_C // HEAD
N_C = HEAD
DECAY_LR = 32
AAA_LR = 32
GN_EPS = 64e-5
CHUNK = 16
TINY = 1e-30
D_A_IN = 2 * H_A * DK_A + 2 * W_A
D_B_IN = 2 * H_B * DK_B + 2 * W_B + GLA_LR
D_SHIFT = 3 * W_C + DECAY_LR + AAA_LR
D_IN = D_A_IN + D_B_IN + D_SHIFT + W_C

QA = (0, H_A * DK_A)
ZA = (QA[1], QA[1] + H_A * DK_A)
IA = (ZA[1], ZA[1] + W_A)
GATE_A = (IA[1], IA[1] + W_A)
QB = (D_A_IN, D_A_IN + H_B * DK_B)
KB = (QB[1], QB[1] + H_B * DK_B)
VB = (KB[1], KB[1] + W_B)
GKL = (VB[1], VB[1] + GLA_LR)
GATE_B = (GKL[1], GKL[1] + W_B)
PC = (D_A_IN + D_B_IN, D_A_IN + D_B_IN + D_SHIFT)
GATE_C = (PC[1], PC[1] + W_C)
SH_R = (0, W_C)
SH_WL = (SH_R[1], SH_R[1] + DECAY_LR)
SH_K = (SH_WL[1], SH_WL[1] + W_C)
SH_V = (SH_K[1], SH_K[1] + W_C)
SH_AL = (SH_V[1], SH_V[1] + AAA_LR)
OUT_A = 0
OUT_B = W_A
OUT_C = W_A + W_B

SUBLANES = 8
LANES = 128
VMEM_LIMIT_BYTES = 56 * 1024 * 1024

Cfg = collections.namedtuple("Cfg", "layer rows tseq chunk has_state final")

NN = (((1,), (0,)), ((), ()))
NT = (((1,), (1,)), ((), ()))


def _dot(a, b):
    return jnp.dot(a.astype(BF16), b.astype(BF16), preferred_element_type=F32)


def _split(a):
    hi = a.astype(BF16)
    return hi, (a - hi.astype(F32)).astype(BF16)


def _dot3s(a_split, b_split, dims):
    (ah, al), (bh, bl) = a_split, b_split
    return (lax.dot_general(ah, bh, dims, preferred_element_type=F32)
            + lax.dot_general(ah, bl, dims, preferred_element_type=F32)
            + lax.dot_general(al, bh, dims, preferred_element_type=F32))


def _dot_nt3(a, b):
    return _dot3s(_split(a), _split(b), NT)


def _dot_hi(a, b):
    return jnp.dot(a, b, precision=lax.Precision.HIGHEST, preferred_element_type=F32)


def _segsum(x, m):
    hi, lo = _split(x)
    return (jnp.dot(hi, m, preferred_element_type=F32)
            + jnp.dot(lo, m, preferred_element_type=F32))


def _softplus(x):
    return jnp.maximum(x, 0.0) + jnp.log1p(jnp.exp(-jnp.abs(x)))


def _logsig(x):
    return jnp.minimum(x, 0.0) - jnp.log1p(jnp.exp(-jnp.abs(x)))


def _silu(x):
    return x * jax.nn.sigmoid(x)


def _chunk_scan(x, chunk, pos, reverse=False):
    n = x.shape[0]
    s = 1
    while s < chunk:
        if reverse:
            x = x + jnp.where(pos < chunk - s, pltpu.roll(x, n - s, axis=0), 0.0)
        else:
            x = x + jnp.where(pos >= s, pltpu.roll(x, s, axis=0), 0.0)
        s *= 2
    return x


def _store_chunk_columns(kt_ref, lane0, x, c):
    tile = max(c, SUBLANES)
    for t0 in range(0, x.shape[0], tile):
        xt = x[t0:t0 + tile, :].T.astype(BF16)
        for j in range(tile // c):
            kt_ref[t0 // c + j, :, lane0:lane0 + c] = xt[:, j * c:(j + 1) * c]


def _store_chunk_decay(d_ref, decay, c):
    for ch in range(decay.shape[0] // c):
        last = (ch + 1) * c - 1
        t0 = last // SUBLANES * SUBLANES
        col = decay[t0:t0 + SUBLANES, :].T[:, last - t0:last - t0 + 1]
        d_ref[ch] = jnp.broadcast_to(col, d_ref.shape[1:])


def _gla_block_prep(r, q, k, v, g, pos, masks, names, heads, dk, dv, out_base, c):
    qx_name, kt_name, d_name = names
    causal = masks[0]
    mid = c // 2 - 1
    after = jnp.where(pos > mid, g, 0.0)
    upto = jnp.where(pos <= mid, g, 0.0)
    rel = _chunk_scan(after, c, pos) - (_chunk_scan(upto, c, pos, reverse=True) - upto)
    q_rel = q * jnp.exp(rel)
    k_rel = k * jnp.exp(-rel)
    eg = jnp.exp(_chunk_scan(g, c, pos))
    r[qx_name][...] = q * eg
    k_end = k * jnp.exp(_chunk_scan(g, c, pos, reverse=True) - g)
    _store_chunk_columns(r[kt_name], 0, k_end, c)
    _store_chunk_decay(r[d_name], eg, c)
    for h in range(heads):
        ks = slice(h * dk, (h + 1) * dk)
        sc = jnp.where(causal, _dot_nt3(q_rel[:, ks], k_rel[:, ks]), 0.0)
        r["o"][:, out_base + h * dv:out_base + (h + 1) * dv] = _dot(sc, v[:, h * dv:(h + 1) * dv])


def _rwkv_block_prep(r, lw, kk, b, pos, masks, c):
    causal, strict, eye = masks
    nch = lw.shape[0] // c
    rr, kmod, v = r["rc"][...], r["kc"][...], r["vc"][...]
    lc = _chunk_scan(lw, c, pos)
    el = jnp.exp(lc)
    inv_cum = jnp.exp(-lc)
    kk_in = kk * jnp.exp(lc - lw)
    b_out = b * inv_cum
    k_out = kmod * inv_cum
    r_in = rr * el
    to_end = jnp.exp(_chunk_scan(lw, c, pos, reverse=True) - lw)
    _store_chunk_columns(r["bkt"], 0, b * to_end, c)
    _store_chunk_columns(r["bkt"], c, kmod * to_end, c)
    _store_chunk_decay(r["dc"], el, c)
    for ch in range(nch):
        r["khr"][ch, c:2 * c, :] = r_in[ch * c:(ch + 1) * c, :]
    cols = [slice(h * N_C, (h + 1) * N_C) for h in range(H_C)]
    pw = [jnp.where(strict, _dot_nt3(kk_in[:, cs], b_out[:, cs]), 0.0) for cs in cols]
    w = [eye - a for a in pw]
    n = 2
    while n < c:
        pw_s = [_split(x) for x in pw]
        pw = [_dot3s(s, s, NN) for s in pw_s]
        w = [x + _dot3s(_split(x), _split(y), NN) for x, y in zip(w, pw)]
        n *= 2
    w_s = [_split(x) for x in w]
    for h, cs in enumerate(cols):
        kh = _dot3s(w_s[h], _split(kk_in[:, cs]), NN)
        for ch in range(nch):
            r["khr"][ch, 0:c, cs] = kh[ch * c:(ch + 1) * c, :]
    bv = [_dot(jnp.where(strict, _dot_nt3(kk_in[:, cs], k_out[:, cs]), 0.0), v[:, cs]) for cs in cols]
    for h, cs in enumerate(cols):
        r["u0"][:, cs] = _dot3s(w_s[h], _split(bv[h]), NN)
    for h, cs in enumerate(cols):
        r["rb"][h] = jnp.where(causal, _dot_nt3(r_in[:, cs], b_out[:, cs]), 0.0).astype(BF16)
        rk = jnp.where(causal, _dot_nt3(r_in[:, cs], k_out[:, cs]), 0.0)
        r["o"][:, OUT_C + h * N_C:OUT_C + (h + 1) * N_C] = _dot(rk, v[:, cs])


def _recurrences(r, p, cfg, g_a, g_b, lw, kk, b):
    rows, c, tb = cfg.rows, cfg.chunk, cfg.tseq
    nseq = rows // tb
    steps = tb // c
    t = pl.program_id(1)
    pos = lax.broadcasted_iota(jnp.int32, (rows, 1), 0) & (c - 1)
    shift = c.bit_length() - 1
    i = lax.broadcasted_iota(jnp.int32, (rows, rows), 0)
    j = lax.broadcasted_iota(jnp.int32, (rows, rows), 1)
    same = (i >> shift) == (j >> shift)
    masks = (same & (j <= i), same & (j < i), jnp.where(i == j, 1.0, 0.0))

    @pl.when(t == 0)
    def _init():
        if cfg.has_state:
            r["sa"][...] = r["sa_in"][...]
            r["sb"][...] = r["sb_in"][...]
            for q in range(nseq):
                for h in range(H_C):
                    r["sct"][q, h] = r["sc_in"][q, h].T
        else:
            r["sa"][...] = jnp.zeros(r["sa"].shape, F32)
            r["sb"][...] = jnp.zeros(r["sb"].shape, F32)
            r["sct"][...] = jnp.zeros(r["sct"].shape, F32)

    _gla_block_prep(r, p[:, QA[0]:QA[1]] * (DK_A ** -0.5), r["ka"][...], p[:, IA[0]:IA[1]], g_a, pos, masks,
                    ("qa", "kta", "da"), H_A, DK_A, DV_A, OUT_A, c)
    _gla_block_prep(r, p[:, QB[0]:QB[1]] * (DK_B ** -0.5), p[:, KB[0]:KB[1]], p[:, VB[0]:VB[1]], g_b, pos, masks,
                    ("qb", "ktb", "db"), H_B, DK_B, DV_B, OUT_B, c)
    _rwkv_block_prep(r, lw, kk, b, pos, masks, c)

    def body(n, carry):
        for q in range(nseq):
            ch = q * steps + n
            row0 = q * tb + n * c
            rs = slice(row0, row0 + c) if isinstance(n, int) else pl.ds(pl.multiple_of(row0, c), c)
            for (qn, kn, dn, st, v, base, heads, dk, dv) in (
                    ("qa", "kta", "da", r["sa"], p[rs, IA[0]:IA[1]], OUT_A, H_A, DK_A, DV_A),
                    ("qb", "ktb", "db", r["sb"], p[rs, VB[0]:VB[1]], OUT_B, H_B, DK_B, DV_B)):
                qx, kt, d = r[qn][rs, :], r[kn][ch], r[dn][ch]
                for h in range(heads):
                    ks = slice(h * dk, (h + 1) * dk)
                    vs = slice(h * dv, (h + 1) * dv)
                    s = st[q, h]
                    r["oi"][rs, base + h * dv:base + (h + 1) * dv] = _dot(qx[:, ks], s)
                    st[q, h] = s * d[ks, 0:dv] + jnp.dot(kt[ks, :], v[:, vs].astype(BF16),
                                                         preferred_element_type=F32)
            khr, bkt, d = r["khr"][ch], r["bkt"][ch], r["dc"][ch]
            u0, v = r["u0"][rs, :], r["vc"][rs, :]
            for h in range(H_C):
                cs = slice(h * N_C, (h + 1) * N_C)
                s = r["sct"][q, h]
                x = _dot(khr[:, cs], s)
                u = -(x[0:c] + u0[:, cs])
                r["u"][rs, cs] = u
                r["oi"][rs, OUT_C + h * N_C:OUT_C + (h + 1) * N_C] = x[c:2 * c]
                r["sct"][q, h] = s * d[cs, 0:N_C] + jnp.dot(
                    bkt[cs, :], jnp.concatenate([u, v[:, cs]], axis=0).astype(BF16),
                    preferred_element_type=F32)
        return carry

    if steps == 1:
        body(0, 0)
    else:
        lax.fori_loop(0, steps, body, 0)

    r["o"][...] += r["oi"][...]
    for h in range(H_C):
        cs = slice(h * N_C, (h + 1) * N_C)
        r["o"][:, OUT_C + h * N_C:OUT_C + (h + 1) * N_C] += jnp.dot(
            r["rb"][h], r["u"][:, cs].astype(BF16), preferred_element_type=F32)

    @pl.when(t == pl.num_programs(1) - 1)
    def _emit():
        for q in range(nseq):
            for h in range(H_C):
                r["sc"][q, h] = r["sct"][q, h].T


def _layer_kernel(names, cfg, *refs):
    r = dict(zip(names, refs))
    rows, tseq = cfg.rows, cfg.tseq
    t = pl.program_id(1)
    p = r["p"]

    x = r["x"][...].reshape(rows, D_MODEL)
    hn = x * lax.rsqrt(jnp.mean(x * x, axis=-1, keepdims=True) + 1e-6) * r["norm_w"][...]
    p[...] = jnp.dot(hn.astype(BF16), r["w_in"][...], preferred_element_type=F32)

    hl = r["hgrn_lb"][...]
    e = jnp.exp(hl - jnp.max(hl, axis=0, keepdims=True))
    sm = e / jnp.sum(e, axis=0, keepdims=True)
    cum = sm[0:1]
    for j in range(1, cfg.layer + 1):
        cum = cum + sm[j:j + 1]
    lb = cum - sm[0:1]
    za = p[:, ZA[0]:ZA[1]]
    ls = _logsig(za)
    la = jnp.log(jnp.maximum(lb, TINY))
    bb = jnp.log1p(-lb) + ls
    lae = jnp.maximum(la, bb) + jnp.log1p(jnp.exp(-jnp.abs(la - bb)))
    g_a = jnp.where(lb > 0.0, lae, ls)
    r["ka"][...] = (1.0 - lb) * jax.nn.sigmoid(-za)

    g_b = _logsig(_dot_hi(p[:, GKL[0]:GKL[1]], r["gk_w2"][...]) + r["gk_b"][...]) * (1.0 / GLA_NORM)

    pc = p[:, PC[0]:PC[1]]
    xs = r["xs"]
    if cfg.has_state:
        row = lax.broadcasted_iota(jnp.int32, (rows, 1), 0)
        prev = jnp.where((row & (tseq - 1)) == 0, r["shift_rows"][...], pltpu.roll(pc, 1, axis=0))
        r["pct"][...] = pc
    else:
        @pl.when(t == 0)
        def _init():
            r["pct"][...] = jnp.zeros(r["pct"].shape, F32)

        xs[...] = pltpu.roll(pc, 1, axis=0)
        for q in range(rows // tseq):
            xs[q * tseq:q * tseq + 1, :] = r["pct"][(q + 1) * SUBLANES - 1:(q + 1) * SUBLANES, :]
            r["pct"][q * SUBLANES:(q + 1) * SUBLANES, :] = p[(q + 1) * tseq - SUBLANES:(q + 1) * tseq, PC[0]:PC[1]]
        prev = xs[...]
    xs[...] = pc + (prev - pc) * r["mu"][...]
    kc = xs[:, SH_K[0]:SH_K[1]]
    wlog = -_softplus(-(r["w0"][...] + _dot_hi(jnp.tanh(xs[:, SH_WL[0]:SH_WL[1]]), r["w2"][...]))) - 0.5
    lw = -jnp.exp(wlog)
    av = jax.nn.sigmoid(r["a0"][...] + _dot_hi(xs[:, SH_AL[0]:SH_AL[1]], r["a2"][...]))
    kk = kc * r["k_k"][...]
    kk = kk * lax.rsqrt(jnp.maximum(_segsum(kk * kk, r["m64c"][...]), 1e-24))
    r["rc"][...] = xs[:, SH_R[0]:SH_R[1]]
    r["vc"][...] = xs[:, SH_V[0]:SH_V[1]]
    r["kc"][...] = kc * (1.0 + (av - 1.0) * r["k_a"][...])

    _recurrences(r, p, cfg, g_a, g_b, lw, kk, kk * av)

    o = r["o"]
    oa = o[:, OUT_A:OUT_A + W_A]
    ya = (oa * lax.rsqrt(_segsum(oa * oa, r["m64a"][...]) * (1.0 / DV_A) + 1e-5)
          * r["hgrn_norm_w"][...] * _silu(p[:, GATE_A[0]:GATE_A[1]]))
    ob = o[:, OUT_B:OUT_B + W_B]
    yb = (ob * lax.rsqrt(_segsum(ob * ob, r["m96"][...]) * (1.0 / DV_B) + 1e-5)
          * r["gla_norm_w"][...] * _silu(p[:, GATE_B[0]:GATE_B[1]]))
    oc = o[:, OUT_C:D_MODEL]
    m64c = r["m64c"][...]
    dc = oc - _segsum(oc, m64c) * (1.0 / N_C)
    ocn = dc * lax.rsqrt(_segsum(dc * dc, m64c) * (1.0 / N_C) + GN_EPS) * r["ln_w"][...] + r["ln_b"][...]
    bonus = _segsum(r["rc"][...] * r["kc"][...] * r["r_k"][...], m64c) * r["vc"][...]
    yc = (ocn + bonus) * _silu(p[:, GATE_C[0]:GATE_C[1]])
    y = jnp.concatenate([ya, yb, yc], axis=-1)
    out = r["x"][...].reshape(rows, D_MODEL) + jnp.dot(y.astype(BF16), r["w_out"][...], preferred_element_type=F32)
    if cfg.final:
        out = out * lax.rsqrt(jnp.mean(out * out, axis=-1, keepdims=True) + 1e-6) * r["final_norm_w"][...]
    r["xo"][...] = out.reshape(r["xo"].shape)


def _block_diag_ones(width, seg):
    i = jnp.arange(width) // seg
    return (i[:, None] == i[None, :]).astype(BF16)


def _run_layer(cfg, nb, nt, x3, state, params):
    l, rows = cfg.layer, cfg.rows
    nseq = rows // cfg.tseq
    nseq_total = nb * nseq
    assert not cfg.has_state or nt == 1
    x_block = (1, rows, D_MODEL) if cfg.has_state else (nseq, cfg.tseq, D_MODEL)
    ins, specs = [], []

    def add(name, arr, spec):
        ins.append((name, arr))
        specs.append(spec)

    def layer_row(name, arr):
        w = arr.shape[-1]
        add(name, arr.reshape(DEPTH, 1, w), pl.BlockSpec((None, 1, w), lambda b, t: (l, 0, 0)))

    once = pl.Buffered(1)

    def layer_mat(name, arr):
        add(name, arr, pl.BlockSpec((None,) + arr.shape[1:], lambda b, t: (l, 0, 0), pipeline_mode=once))

    def whole(name, arr):
        add(name, arr, pl.BlockSpec(arr.shape, lambda b, t: (0,) * arr.ndim, pipeline_mode=once))

    add("x", x3, pl.BlockSpec(x_block, lambda b, t: (b, t, 0)))
    if cfg.has_state:
        st_a, st_b, st_c, shift_rows = state
        add("shift_rows", shift_rows, pl.BlockSpec((None, rows, D_SHIFT), lambda b, t: (l, b, 0)))
        add("sa_in", st_a, pl.BlockSpec((None, nseq, H_A, DK_A, DV_A), lambda b, t: (l, b, 0, 0, 0)))
        add("sb_in", st_b, pl.BlockSpec((None, nseq, H_B, DK_B, DV_B), lambda b, t: (l, b, 0, 0, 0)))
        add("sc_in", st_c, pl.BlockSpec((None, nseq, H_C, N_C, N_C), lambda b, t: (l, b, 0, 0, 0)))
    layer_row("norm_w", params["norm_w"])
    layer_mat("w_in", params["w_in"])
    whole("hgrn_lb", params["hgrn_lb"])
    layer_row("hgrn_norm_w", params["hgrn_norm_w"])
    layer_mat("gk_w2", params["gla_gk_w2"])
    layer_row("gk_b", params["gla_gk_b"])
    layer_row("gla_norm_w", params["gla_norm_w"])
    layer_row("mu", params["rwkv_mu"])
    layer_row("w0", params["rwkv_w0"])
    layer_mat("w2", params["rwkv_w2"])
    layer_row("a0", params["rwkv_a0"])
    layer_mat("a2", params["rwkv_a2"])
    layer_row("k_k", params["rwkv_k_k"])
    layer_row("k_a", params["rwkv_k_a"])
    layer_row("r_k", params["rwkv_r_k"])
    layer_row("ln_w", params["rwkv_ln_w"])
    layer_row("ln_b", params["rwkv_ln_b"])
    layer_mat("w_out", params["w_out"])
    if cfg.final:
        whole("final_norm_w", params["final_norm_w"].reshape(1, D_MODEL))
    whole("m64a", params["m64a"])
    whole("m96", params["m96"])
    whole("m64c", params["m64c"])

    pct_rows = rows if cfg.has_state else nseq * SUBLANES
    outs = [
        ("xo", jax.ShapeDtypeStruct(x3.shape, F32), pl.BlockSpec(x_block, lambda b, t: (b, t, 0))),
        ("sa", jax.ShapeDtypeStruct((nseq_total, H_A, DK_A, DV_A), F32),
         pl.BlockSpec((nseq, H_A, DK_A, DV_A), lambda b, t: (b, 0, 0, 0))),
        ("sb", jax.ShapeDtypeStruct((nseq_total, H_B, DK_B, DV_B), F32),
         pl.BlockSpec((nseq, H_B, DK_B, DV_B), lambda b, t: (b, 0, 0, 0))),
        ("sc", jax.ShapeDtypeStruct((nseq_total, H_C, N_C, N_C), F32),
         pl.BlockSpec((nseq, H_C, N_C, N_C), lambda b, t: (b, 0, 0, 0))),
        ("pct", jax.ShapeDtypeStruct((nb * pct_rows, D_SHIFT), F32),
         pl.BlockSpec((pct_rows, D_SHIFT), lambda b, t: (b, 0))),
    ]
    a_w, b_w, c = H_A * DK_A, H_B * DK_B, cfg.chunk
    nch = rows // c
    scratch = [("p", (rows, D_IN), F32), ("xs", (rows, D_SHIFT), F32), ("o", (rows, D_MODEL), F32),
               ("oi", (rows, D_MODEL), F32),
               ("qa", (rows, a_w), F32), ("ka", (rows, a_w), F32), ("qb", (rows, b_w), F32),
               ("rc", (rows, W_C), F32), ("vc", (rows, W_C), F32), ("kc", (rows, W_C), F32),
               ("u0", (rows, W_C), F32), ("u", (rows, W_C), F32),
               ("kta", (nch, a_w, c), BF16), ("da", (nch, a_w, LANES), F32),
               ("ktb", (nch, b_w, c), BF16), ("db", (nch, b_w, LANES), F32),
               ("bkt", (nch, W_C, 2 * c), BF16), ("dc", (nch, W_C, LANES), F32),
               ("khr", (nch, 2 * c, W_C), F32), ("rb", (H_C, rows, rows), BF16),
               ("sct", (nseq, H_C, N_C, N_C), F32)]
    names = tuple(n for n, _ in ins) + tuple(n for n, _, _ in outs) + tuple(n for n, _, _ in scratch)
    return pl.pallas_call(
        functools.partial(_layer_kernel, names, cfg),
        grid=(nb, nt),
        in_specs=specs,
        out_specs=[s for _, _, s in outs],
        out_shape=[s for _, s, _ in outs],
        scratch_shapes=[pltpu.VMEM(shape, dt) for _, shape, dt in scratch],
        compiler_params=pltpu.CompilerParams(
            dimension_semantics=("arbitrary", "arbitrary"),
            vmem_limit_bytes=VMEM_LIMIT_BYTES),
        name=f"layer{l}_{'sample' if cfg.has_state else 'prompt'}",
    )(*[a for _, a in ins])


def _trunk(x, state, params, rows, tseq, chunk):
    bsz, tlen, _ = x.shape
    has_state = state is not None
    nseq = rows // tseq
    assert rows % tseq == 0 and tseq % chunk == 0 and bsz % nseq == 0 and tlen % tseq == 0
    nb, nt = bsz // nseq, tlen // tseq
    if has_state:
        assert tlen == tseq
        st_a, st_b, st_c, st_s = state
        shift_rows = jnp.pad(st_s[:, :, None, :], ((0, 0), (0, 0), (0, tlen - 1), (0, 0)))
        state = (st_a, st_b, st_c, shift_rows.reshape(DEPTH, bsz * tlen, D_SHIFT))
        x3 = x.reshape(nb, rows, D_MODEL)
    else:
        x3 = x
    new_a, new_b, new_c, new_s = [], [], [], []
    for l in range(DEPTH):
        cfg = Cfg(layer=l, rows=rows, tseq=tseq, chunk=chunk, has_state=has_state, final=(l == DEPTH - 1))
        x3, sa, sb, sc, pct = _run_layer(cfg, nb, nt, x3, state, params)
        new_a.append(sa)
        new_b.append(sb)
        new_c.append(sc)
        if has_state:
            new_s.append(pct.reshape(bsz, tlen, D_SHIFT)[:, tlen - 1])
        else:
            new_s.append(pct.reshape(bsz, SUBLANES, D_SHIFT)[:, SUBLANES - 1])
    return (x3.reshape(bsz, tlen, D_MODEL), jnp.stack(new_a), jnp.stack(new_b),
            jnp.stack(new_c), jnp.stack(new_s))


PROMPT_ROWS = 256
PROMPT_TSEQ = 32
SAMPLE_ROWS = 32


def kernel(x_prompt, x_sample, state_hgrn, state_gla, state_rwkv, state_shift, norm_w, w_in, hgrn_lb, hgrn_norm_w, gla_gk_w2, gla_gk_b, gla_norm_w, rwkv_mu, rwkv_w0, rwkv_w2, rwkv_a0, rwkv_a2, rwkv_k_k, rwkv_k_a, rwkv_r_k, rwkv_ln_w, rwkv_ln_b, w_out, final_norm_w):
    params = dict(
        norm_w=norm_w, w_in=w_in.astype(BF16), hgrn_lb=hgrn_lb, hgrn_norm_w=hgrn_norm_w,
        gla_gk_w2=gla_gk_w2, gla_gk_b=gla_gk_b, gla_norm_w=gla_norm_w, rwkv_mu=rwkv_mu,
        rwkv_w0=rwkv_w0, rwkv_w2=rwkv_w2, rwkv_a0=rwkv_a0, rwkv_a2=rwkv_a2, rwkv_k_k=rwkv_k_k,
        rwkv_k_a=rwkv_k_a, rwkv_r_k=rwkv_r_k, rwkv_ln_w=rwkv_ln_w, rwkv_ln_b=rwkv_ln_b,
        w_out=w_out.astype(BF16), final_norm_w=final_norm_w,
        m64a=_block_diag_ones(W_A, DV_A), m96=_block_diag_ones(W_B, DV_B),
        m64c=_block_diag_ones(W_C, N_C),
    )
    y_p, hgrn_p, gla_p, rwkv_p, shift_p = _trunk(x_prompt, None, params, PROMPT_ROWS, PROMPT_TSEQ, CHUNK)
    tok = x_sample.shape[1]
    y_s, hgrn_s, gla_s, rwkv_s, shift_s = _trunk(
        x_sample, (state_hgrn, state_gla, state_rwkv, state_shift), params, SAMPLE_ROWS, tok, tok)
    return (y_p, y_s, hgrn_p, gla_p, rwkv_p, shift_p, hgrn_s, gla_s, rwkv_s, shift_s)
```

```python
import collections
import functools

import jax
import jax.numpy as jnp
from jax import lax
from jax.experimental import pallas as pl
from jax.experimental.pallas import tpu as pltpu

F32 = jnp.float32
BF16 = jnp.bfloat16

D_MODEL = 1024
DEPTH = 4
HEAD = 64
W_A = D_MODEL // 4
H_A = W_A // HEAD
DK_A = HEAD
DV_A = W_A // H_A
W_B = (D_MODEL - W_A) // 2
H_B = 4
DV_B = W_B // H_B
DK_B = DV_B // 2
GLA_LR = 16
GLA_NORM = 16.0
W_C = D_MODEL - W_A - W_B
H_C = W_C // HEAD
N_C = HEAD
DECAY_LR = 32
AAA_LR = 32
GN_EPS = 64e-5
TINY = 1e-30
D_A_IN = 2 * H_A * DK_A + 2 * W_A
D_B_IN = 2 * H_B * DK_B + 2 * W_B + GLA_LR
D_SHIFT = 3 * W_C + DECAY_LR + AAA_LR
D_IN = D_A_IN + D_B_IN + D_SHIFT + W_C

QA = (0, H_A * DK_A)
ZA = (QA[1], QA[1] + H_A * DK_A)
IA = (ZA[1], ZA[1] + W_A)
GATE_A = (IA[1], IA[1] + W_A)
QB = (D_A_IN, D_A_IN + H_B * DK_B)
KB = (QB[1], QB[1] + H_B * DK_B)
VB = (KB[1], KB[1] + W_B)
GKL = (VB[1], VB[1] + GLA_LR)
GATE_B = (GKL[1], GKL[1] + W_B)
PC = (D_A_IN + D_B_IN, D_A_IN + D_B_IN + D_SHIFT)
GATE_C = (PC[1], PC[1] + W_C)
SH_R = (0, W_C)
SH_WL = (SH_R[1], SH_R[1] + DECAY_LR)
SH_K = (SH_WL[1], SH_WL[1] + W_C)
SH_V = (SH_K[1], SH_K[1] + W_C)
SH_AL = (SH_V[1], SH_V[1] + AAA_LR)
OUT_A = 0
OUT_B = W_A
OUT_C = W_A + W_B

SUBLANES = 8
LANES = 128
VMEM_LIMIT_BYTES = 56 * 1024 * 1024

Cfg = collections.namedtuple("Cfg", "layer rows tseq chunk has_state final")

NN = (((1,), (0,)), ((), ()))
NT = (((1,), (1,)), ((), ()))


def _dot(a, b):
    return jnp.dot(a.astype(BF16), b.astype(BF16), preferred_element_type=F32)


def _split(a):
    hi = a.astype(BF16)
    return hi, (a - hi.astype(F32)).astype(BF16)


def _dot3s(a_split, b_split, dims):
    (ah, al), (bh, bl) = a_split, b_split
    return (lax.dot_general(ah, bh, dims, preferred_element_type=F32)
            + lax.dot_general(ah, bl, dims, preferred_element_type=F32)
            + lax.dot_general(al, bh, dims, preferred_element_type=F32))


def _dot_nt3(a, b):
    return _dot3s(_split(a), _split(b), NT)


def _dot_hi(a, b):
    return jnp.dot(a, b, precision=lax.Precision.HIGHEST, preferred_element_type=F32)


def _segsum(x, m):
    hi, lo = _split(x)
    return (jnp.dot(hi, m, preferred_element_type=F32)
            + jnp.dot(lo, m, preferred_element_type=F32))


def _softplus(x):
    return jnp.maximum(x, 0.0) + jnp.log1p(jnp.exp(-jnp.abs(x)))


def _logsig(x):
    return jnp.minimum(x, 0.0) - jnp.log1p(jnp.exp(-jnp.abs(x)))


def _silu(x):
    return x * jax.nn.sigmoid(x)


def _chunk_scan(x, chunk, pos, reverse=False):
    n = x.shape[0]
    s = 1
    while s < chunk:
        if reverse:
            x = x + jnp.where(pos < chunk - s, pltpu.roll(x, n - s, axis=0), 0.0)
        else:
            x = x + jnp.where(pos >= s, pltpu.roll(x, s, axis=0), 0.0)
        s *= 2
    return x


def _chunk_row(x, m, chunk, pos):
    rows, w = x.shape
    if chunk % SUBLANES == 0:
        x3 = x.reshape(rows // chunk, chunk, w)
        return jnp.broadcast_to(x3[:, m:m + 1, :], x3.shape).reshape(rows, w)
    z = jnp.where(pos == m, x, 0.0)
    return jnp.where(pos >= m, _chunk_scan(z, chunk, pos), _chunk_scan(z, chunk, pos, reverse=True))


def _store_chunk_columns(kt_ref, lane0, x, c):
    tile = max(c, SUBLANES)
    for t0 in range(0, x.shape[0], tile):
        xt = x[t0:t0 + tile, :].T.astype(BF16)
        for j in range(tile // c):
            kt_ref[t0 // c + j, :, lane0:lane0 + c] = xt[:, j * c:(j + 1) * c]


def _store_chunk_decay(d_ref, decay, c):
    for ch in range(decay.shape[0] // c):
        last = (ch + 1) * c - 1
        t0 = last // SUBLANES * SUBLANES
        col = decay[t0:t0 + SUBLANES, :].T[:, last - t0:last - t0 + 1]
        d_ref[ch] = jnp.broadcast_to(col, d_ref.shape[1:])


def _gla_block_prep(r, q, k, v, g, pos, masks, names, heads, dk, dv, out_base, c):
    qx_name, kt_name, d_name = names
    causal = masks[0]
    cum = _chunk_scan(g, c, pos)
    rel = cum - _chunk_row(cum, c // 2 - 1, c, pos)
    q_rel = q * jnp.exp(rel)
    k_rel = k * jnp.exp(-rel)
    eg = jnp.exp(cum)
    r[qx_name][...] = q * eg
    k_end = k * jnp.exp(_chunk_row(cum, c - 1, c, pos) - cum)
    _store_chunk_columns(r[kt_name], 0, k_end, c)
    _store_chunk_decay(r[d_name], eg, c)
    for h in range(heads):
        ks = slice(h * dk, (h + 1) * dk)
        sc = jnp.where(causal, _dot_nt3(q_rel[:, ks], k_rel[:, ks]), 0.0)
        r["o"][:, out_base + h * dv:out_base + (h + 1) * dv] = _dot(sc, v[:, h * dv:(h + 1) * dv])


def _rwkv_block_prep(r, lw, kk, b, pos, masks, c):
    causal, strict, eye = masks
    nch = lw.shape[0] // c
    rr, kmod, v = r["rc"][...], r["kc"][...], r["vc"][...]
    lc = _chunk_scan(lw, c, pos)
    el = jnp.exp(lc)
    inv_cum = jnp.exp(-lc)
    kk_in = kk * jnp.exp(lc - lw)
    b_out = b * inv_cum
    k_out = kmod * inv_cum
    r_in = rr * el
    to_end = jnp.exp(_chunk_row(lc, c - 1, c, pos) - lc)
    _store_chunk_columns(r["bkt"], 0, b * to_end, c)
    _store_chunk_columns(r["bkt"], c, kmod * to_end, c)
    _store_chunk_decay(r["dc"], el, c)
    for ch in range(nch):
        r["khr"][ch, c:2 * c, :] = r_in[ch * c:(ch + 1) * c, :]
    cols = [slice(h * N_C, (h + 1) * N_C) for h in range(H_C)]
    pw = [jnp.where(strict, _dot_nt3(kk_in[:, cs], b_out[:, cs]), 0.0) for cs in cols]
    w = [eye - a for a in pw]
    n = 2
    while n < c:
        pw_s = [_split(x) for x in pw]
        pw = [_dot3s(s, s, NN) for s in pw_s]
        w = [x + _dot3s(_split(x), _split(y), NN) for x, y in zip(w, pw)]
        n *= 2
    w_s = [_split(x) for x in w]
    for h, cs in enumerate(cols):
        kh = _dot3s(w_s[h], _split(kk_in[:, cs]), NN)
        for ch in range(nch):
            r["khr"][ch, 0:c, cs] = kh[ch * c:(ch + 1) * c, :]
    bv = [_dot(jnp.where(strict, _dot_nt3(kk_in[:, cs], k_out[:, cs]), 0.0), v[:, cs]) for cs in cols]
    for h, cs in enumerate(cols):
        r["u0"][:, cs] = _dot3s(w_s[h], _split(bv[h]), NN)
    for h, cs in enumerate(cols):
        r["rb"][h] = jnp.where(causal, _dot_nt3(r_in[:, cs], b_out[:, cs]), 0.0).astype(BF16)
        rk = jnp.where(causal, _dot_nt3(r_in[:, cs], k_out[:, cs]), 0.0)
        r["o"][:, OUT_C + h * N_C:OUT_C + (h + 1) * N_C] = _dot(rk, v[:, cs])


def _recurrences(r, p, cfg):
    rows, c, tb = cfg.rows, cfg.chunk, cfg.tseq
    nseq = rows // tb
    steps = tb // c
    t = pl.program_id(1)
    pos = lax.broadcasted_iota(jnp.int32, (rows, 1), 0) & (c - 1)
    shift = c.bit_length() - 1
    i = lax.broadcasted_iota(jnp.int32, (rows, rows), 0)
    j = lax.broadcasted_iota(jnp.int32, (rows, rows), 1)
    same = (i >> shift) == (j >> shift)
    masks = (same & (j <= i), same & (j < i), jnp.where(i == j, 1.0, 0.0))

    @pl.when(t == 0)
    def _init():
        if cfg.has_state:
            r["sa"][...] = r["sa_in"][...]
            r["sb"][...] = r["sb_in"][...]
            for q in range(nseq):
                for h in range(H_C):
                    r["sct"][q, h] = r["sc_in"][q, h].T
        else:
            r["sa"][...] = jnp.zeros(r["sa"].shape, F32)
            r["sb"][...] = jnp.zeros(r["sb"].shape, F32)
            r["sct"][...] = jnp.zeros(r["sct"].shape, F32)

    lw, kk, b = _rwkv_gates(r, p, cfg)
    _rwkv_block_prep(r, lw, kk, b, pos, masks, c)
    g_a, g_b = _gla_gates(r, p, cfg)
    _gla_block_prep(r, p[:, QA[0]:QA[1]] * (DK_A ** -0.5), r["ka"][...], p[:, IA[0]:IA[1]], g_a, pos, masks,
                    ("qa", "kta", "da"), H_A, DK_A, DV_A, OUT_A, c)
    _gla_block_prep(r, p[:, QB[0]:QB[1]] * (DK_B ** -0.5), p[:, KB[0]:KB[1]], p[:, VB[0]:VB[1]], g_b, pos, masks,
                    ("qb", "ktb", "db"), H_B, DK_B, DV_B, OUT_B, c)

    def body(n, carry):
        for q in range(nseq):
            ch = q * steps + n
            row0 = q * tb + n * c
            rs = slice(row0, row0 + c) if isinstance(n, int) else pl.ds(pl.multiple_of(row0, c), c)
            for (qn, kn, dn, st, v, base, heads, dk, dv) in (
                    ("qa", "kta", "da", r["sa"], p[rs, IA[0]:IA[1]], OUT_A, H_A, DK_A, DV_A),
                    ("qb", "ktb", "db", r["sb"], p[rs, VB[0]:VB[1]], OUT_B, H_B, DK_B, DV_B)):
                qx, kt, d = r[qn][rs, :], r[kn][ch], r[dn][ch]
                for h in range(heads):
                    ks = slice(h * dk, (h + 1) * dk)
                    vs = slice(h * dv, (h + 1) * dv)
                    s = st[q, h]
                    r["oi"][rs, base + h * dv:base + (h + 1) * dv] = _dot(qx[:, ks], s)
                    st[q, h] = s * d[ks, 0:dv] + jnp.dot(kt[ks, :], v[:, vs].astype(BF16),
                                                         preferred_element_type=F32)
            khr, bkt, d = r["khr"][ch], r["bkt"][ch], r["dc"][ch]
            u0, v = r["u0"][rs, :], r["vc"][rs, :]
            for h in range(H_C):
                cs = slice(h * N_C, (h + 1) * N_C)
                s = r["sct"][q, h]
                x = _dot(khr[:, cs], s)
                u = -(x[0:c] + u0[:, cs])
                r["u"][rs, cs] = u
                r["oi"][rs, OUT_C + h * N_C:OUT_C + (h + 1) * N_C] = x[c:2 * c]
                r["sct"][q, h] = s * d[cs, 0:N_C] + jnp.dot(
                    bkt[cs, :], jnp.concatenate([u, v[:, cs]], axis=0).astype(BF16),
                    preferred_element_type=F32)
        return carry

    if steps == 1:
        body(0, 0)
    else:
        lax.fori_loop(0, steps, body, 0)

    r["o"][...] += r["oi"][...]
    for h in range(H_C):
        cs = slice(h * N_C, (h + 1) * N_C)
        r["o"][:, OUT_C + h * N_C:OUT_C + (h + 1) * N_C] += jnp.dot(
            r["rb"][h], r["u"][:, cs].astype(BF16), preferred_element_type=F32)

    @pl.when(t == pl.num_programs(1) - 1)
    def _emit():
        for q in range(nseq):
            for h in range(H_C):
                r["sc"][q, h] = r["sct"][q, h].T


def _layer_kernel(names, cfg, *refs):
    r = dict(zip(names, refs))
    rows = cfg.rows
    p = r["p"]

    x = r["x"][...].reshape(rows, D_MODEL)
    hn = x * lax.rsqrt(jnp.mean(x * x, axis=-1, keepdims=True) + 1e-6) * r["norm_w"][...]
    p[...] = jnp.dot(hn.astype(BF16), r["w_in"][...], preferred_element_type=F32)

    _recurrences(r, p, cfg)

    o = r["o"]
    oa = o[:, OUT_A:OUT_A + W_A]
    ya = (oa * lax.rsqrt(_segsum(oa * oa, r["m64a"][...]) * (1.0 / DV_A) + 1e-5)
          * r["hgrn_norm_w"][...] * _silu(p[:, GATE_A[0]:GATE_A[1]]))
    ob = o[:, OUT_B:OUT_B + W_B]
    yb = (ob * lax.rsqrt(_segsum(ob * ob, r["m96"][...]) * (1.0 / DV_B) + 1e-5)
          * r["gla_norm_w"][...] * _silu(p[:, GATE_B[0]:GATE_B[1]]))
    oc = o[:, OUT_C:D_MODEL]
    m64c = r["m64c"][...]
    dc = oc - _segsum(oc, m64c) * (1.0 / N_C)
    ocn = dc * lax.rsqrt(_segsum(dc * dc, m64c) * (1.0 / N_C) + GN_EPS) * r["ln_w"][...] + r["ln_b"][...]
    bonus = _segsum(r["rc"][...] * r["kc"][...] * r["r_k"][...], m64c) * r["vc"][...]
    yc = (ocn + bonus) * _silu(p[:, GATE_C[0]:GATE_C[1]])
    y = jnp.concatenate([ya, yb, yc], axis=-1)
    out = r["x"][...].reshape(rows, D_MODEL) + jnp.dot(y.astype(BF16), r["w_out"][...], preferred_element_type=F32)
    if cfg.final:
        out = out * lax.rsqrt(jnp.mean(out * out, axis=-1, keepdims=True) + 1e-6) * r["final_norm_w"][...]
    r["xo"][...] = out.reshape(r["xo"].shape)


def _gla_gates(r, p, cfg):
    hl = r["hgrn_lb"][...]
    e = jnp.exp(hl - jnp.max(hl, axis=0, keepdims=True))
    sm = e / jnp.sum(e, axis=0, keepdims=True)
    cum = sm[0:1]
    for j in range(1, cfg.layer + 1):
        cum = cum + sm[j:j + 1]
    lb = cum - sm[0:1]
    za = p[:, ZA[0]:ZA[1]]
    ls = _logsig(za)
    la = jnp.log(jnp.maximum(lb, TINY))
    bb = jnp.log1p(-lb) + ls
    lae = jnp.maximum(la, bb) + jnp.log1p(jnp.exp(-jnp.abs(la - bb)))
    g_a = jnp.where(lb > 0.0, lae, ls)
    r["ka"][...] = (1.0 - lb) * jax.nn.sigmoid(-za)
    g_b = _logsig(_dot_hi(p[:, GKL[0]:GKL[1]], r["gk_w2"][...]) + r["gk_b"][...]) * (1.0 / GLA_NORM)
    return g_a, g_b


def _rwkv_gates(r, p, cfg):
    rows, tseq = cfg.rows, cfg.tseq
    t = pl.program_id(1)
    pc = p[:, PC[0]:PC[1]]
    xs = r["xs"]
    if cfg.has_state:
        row = lax.broadcasted_iota(jnp.int32, (rows, 1), 0)
        prev = jnp.where((row & (tseq - 1)) == 0, r["shift_rows"][...], pltpu.roll(pc, 1, axis=0))
        r["pct"][...] = pc
    else:
        @pl.when(t == 0)
        def _init():
            r["pct"][...] = jnp.zeros(r["pct"].shape, F32)

        xs[...] = pltpu.roll(pc, 1, axis=0)
        for q in range(rows // tseq):
            xs[q * tseq:q * tseq + 1, :] = r["pct"][(q + 1) * SUBLANES - 1:(q + 1) * SUBLANES, :]
            r["pct"][q * SUBLANES:(q + 1) * SUBLANES, :] = p[(q + 1) * tseq - SUBLANES:(q + 1) * tseq, PC[0]:PC[1]]
        prev = xs[...]
    xs[...] = pc + (prev - pc) * r["mu"][...]
    kc = xs[:, SH_K[0]:SH_K[1]]
    wlog = -_softplus(-(r["w0"][...] + _dot_hi(jnp.tanh(xs[:, SH_WL[0]:SH_WL[1]]), r["w2"][...]))) - 0.5
    lw = -jnp.exp(wlog)
    av = jax.nn.sigmoid(r["a0"][...] + _dot_hi(xs[:, SH_AL[0]:SH_AL[1]], r["a2"][...]))
    kk = kc * r["k_k"][...]
    kk = kk * lax.rsqrt(jnp.maximum(_segsum(kk * kk, r["m64c"][...]), 1e-24))
    r["rc"][...] = xs[:, SH_R[0]:SH_R[1]]
    r["vc"][...] = xs[:, SH_V[0]:SH_V[1]]
    r["kc"][...] = kc * (1.0 + (av - 1.0) * r["k_a"][...])
    return lw, kk, kk * av


def _block_diag_ones(width, seg):
    i = jnp.arange(width) // seg
    return (i[:, None] == i[None, :]).astype(BF16)


def _run_layer(cfg, nb, nt, x3, state, new_states, params):
    l, rows = cfg.layer, cfg.rows
    nseq = rows // cfg.tseq
    nseq_total = nb * nseq
    assert not cfg.has_state or nt == 1
    x_block = (1, rows, D_MODEL) if cfg.has_state else (nseq, cfg.tseq, D_MODEL)
    ins, specs = [], []

    def add(name, arr, spec):
        ins.append((name, arr))
        specs.append(spec)

    def layer_row(name, arr):
        w = arr.shape[-1]
        add(name, arr.reshape(DEPTH, 1, w), pl.BlockSpec((None, 1, w), lambda b, t: (l, 0, 0)))

    once = pl.Buffered(1)

    def layer_mat(name, arr):
        add(name, arr, pl.BlockSpec((None,) + arr.shape[1:], lambda b, t: (l, 0, 0), pipeline_mode=once))

    def whole(name, arr):
        add(name, arr, pl.BlockSpec(arr.shape, lambda b, t: (0,) * arr.ndim, pipeline_mode=once))

    add("x", x3, pl.BlockSpec(x_block, lambda b, t: (b, t, 0)))
    if cfg.has_state:
        st_a, st_b, st_c, shift_rows = state
        add("shift_rows", shift_rows, pl.BlockSpec((None, rows, D_SHIFT), lambda b, t: (l, b, 0)))
        add("sa_in", st_a, pl.BlockSpec((None, nseq, H_A, DK_A, DV_A), lambda b, t: (l, b, 0, 0, 0)))
        add("sb_in", st_b, pl.BlockSpec((None, nseq, H_B, DK_B, DV_B), lambda b, t: (l, b, 0, 0, 0)))
        add("sc_in", st_c, pl.BlockSpec((None, nseq, H_C, N_C, N_C), lambda b, t: (l, b, 0, 0, 0)))
    layer_row("norm_w", params["norm_w"])
    layer_mat("w_in", params["w_in"])
    whole("hgrn_lb", params["hgrn_lb"])
    layer_row("hgrn_norm_w", params["hgrn_norm_w"])
    layer_mat("gk_w2", params["gla_gk_w2"])
    layer_row("gk_b", params["gla_gk_b"])
    layer_row("gla_norm_w", params["gla_norm_w"])
    layer_row("mu", params["rwkv_mu"])
    layer_row("w0", params["rwkv_w0"])
    layer_mat("w2", params["rwkv_w2"])
    layer_row("a0", params["rwkv_a0"])
    layer_mat("a2", params["rwkv_a2"])
    layer_row("k_k", params["rwkv_k_k"])
    layer_row("k_a", params["rwkv_k_a"])
    layer_row("r_k", params["rwkv_r_k"])
    layer_row("ln_w", params["rwkv_ln_w"])
    layer_row("ln_b", params["rwkv_ln_b"])
    layer_mat("w_out", params["w_out"])
    if cfg.final:
        whole("final_norm_w", params["final_norm_w"].reshape(1, D_MODEL))
    whole("m64a", params["m64a"])
    whole("m96", params["m96"])
    whole("m64c", params["m64c"])

    pct_rows = rows if cfg.has_state else nseq * SUBLANES
    state_dims = ((H_A, DK_A, DV_A), (H_B, DK_B, DV_B), (H_C, N_C, N_C))
    aliases = {}
    for k, (name, arr, dims) in enumerate(zip(("sa_all", "sb_all", "sc_all"), new_states, state_dims)):
        assert arr.shape == (DEPTH, nseq_total) + dims
        aliases[len(ins)] = 1 + k
        add(name, arr, pl.BlockSpec(memory_space=pl.ANY))
    outs = [("xo", jax.ShapeDtypeStruct(x3.shape, F32), pl.BlockSpec(x_block, lambda b, t: (b, t, 0)))]
    outs += [(name, jax.ShapeDtypeStruct((DEPTH, nseq_total) + dims, F32),
              pl.BlockSpec((None, nseq) + dims, lambda b, t: (l, b, 0, 0, 0)))
             for name, dims in zip(("sa", "sb", "sc"), state_dims)]
    outs += [("pct", jax.ShapeDtypeStruct((nb * pct_rows, D_SHIFT), F32),
              pl.BlockSpec((pct_rows, D_SHIFT), lambda b, t: (b, 0)))]
    a_w, b_w, c = H_A * DK_A, H_B * DK_B, cfg.chunk
    nch = rows // c
    scratch = [("p", (rows, D_IN), F32), ("xs", (rows, D_SHIFT), F32), ("o", (rows, D_MODEL), F32),
               ("oi", (rows, D_MODEL), F32),
               ("qa", (rows, a_w), F32), ("ka", (rows, a_w), F32), ("qb", (rows, b_w), F32),
               ("rc", (rows, W_C), F32), ("vc", (rows, W_C), F32), ("kc", (rows, W_C), F32),
               ("u0", (rows, W_C), F32), ("u", (rows, W_C), F32),
               ("kta", (nch, a_w, c), BF16), ("da", (nch, a_w, LANES), F32),
               ("ktb", (nch, b_w, c), BF16), ("db", (nch, b_w, LANES), F32),
               ("bkt", (nch, W_C, 2 * c), BF16), ("dc", (nch, W_C, LANES), F32),
               ("khr", (nch, 2 * c, W_C), F32), ("rb", (H_C, rows, rows), BF16),
               ("sct", (nseq, H_C, N_C, N_C), F32)]
    names = tuple(n for n, _ in ins) + tuple(n for n, _, _ in outs) + tuple(n for n, _, _ in scratch)
    return pl.pallas_call(
        functools.partial(_layer_kernel, names, cfg),
        grid=(nb, nt),
        in_specs=specs,
        out_specs=[s for _, _, s in outs],
        out_shape=[s for _, s, _ in outs],
        scratch_shapes=[pltpu.VMEM(shape, dt) for _, shape, dt in scratch],
        input_output_aliases=aliases,
        compiler_params=pltpu.CompilerParams(
            dimension_semantics=("arbitrary", "arbitrary"),
            vmem_limit_bytes=VMEM_LIMIT_BYTES),
        name=f"layer{l}_{'sample' if cfg.has_state else 'prompt'}",
    )(*[a for _, a in ins])


def _trunk(x, state, params, rows, tseq, chunk):
    bsz, tlen, _ = x.shape
    has_state = state is not None
    nseq = rows // tseq
    assert rows % tseq == 0 and tseq % chunk == 0 and bsz % nseq == 0 and tlen % tseq == 0
    nb, nt = bsz // nseq, tlen // tseq
    if has_state:
        assert tlen == tseq
        st_a, st_b, st_c, st_s = state
        shift_rows = jnp.pad(st_s[:, :, None, :], ((0, 0), (0, 0), (0, tlen - 1), (0, 0)))
        state = (st_a, st_b, st_c, shift_rows.reshape(DEPTH, bsz * tlen, D_SHIFT))
        x3 = x.reshape(nb, rows, D_MODEL)
    else:
        x3 = x
    new_states = tuple(jnp.zeros((DEPTH, bsz) + dims, F32)
                       for dims in ((H_A, DK_A, DV_A), (H_B, DK_B, DV_B), (H_C, N_C, N_C)))
    new_s = []
    for l in range(DEPTH):
        cfg = Cfg(layer=l, rows=rows, tseq=tseq, chunk=chunk, has_state=has_state, final=(l == DEPTH - 1))
        x3, *new_states, pct = _run_layer(cfg, nb, nt, x3, state, tuple(new_states), params)
        if has_state:
            new_s.append(pct.reshape(bsz, tlen, D_SHIFT)[:, tlen - 1])
        else:
            new_s.append(pct.reshape(bsz, SUBLANES, D_SHIFT)[:, SUBLANES - 1])
    return (x3.reshape(bsz, tlen, D_MODEL), *new_states, jnp.stack(new_s))


PROMPT_ROWS = 256
PROMPT_TSEQ = 32
PROMPT_CHUNK = 32
SAMPLE_ROWS = 32


def kernel(x_prompt, x_sample, state_hgrn, state_gla, state_rwkv, state_shift, norm_w, w_in, hgrn_lb, hgrn_norm_w, gla_gk_w2, gla_gk_b, gla_norm_w, rwkv_mu, rwkv_w0, rwkv_w2, rwkv_a0, rwkv_a2, rwkv_k_k, rwkv_k_a, rwkv_r_k, rwkv_ln_w, rwkv_ln_b, w_out, final_norm_w):
    params = dict(
        norm_w=norm_w, w_in=w_in.astype(BF16), hgrn_lb=hgrn_lb, hgrn_norm_w=hgrn_norm_w,
        gla_gk_w2=gla_gk_w2, gla_gk_b=gla_gk_b, gla_norm_w=gla_norm_w, rwkv_mu=rwkv_mu,
        rwkv_w0=rwkv_w0, rwkv_w2=rwkv_w2, rwkv_a0=rwkv_a0, rwkv_a2=rwkv_a2, rwkv_k_k=rwkv_k_k,
        rwkv_k_a=rwkv_k_a, rwkv_r_k=rwkv_r_k, rwkv_ln_w=rwkv_ln_w, rwkv_ln_b=rwkv_ln_b,
        w_out=w_out.astype(BF16), final_norm_w=final_norm_w,
        m64a=_block_diag_ones(W_A, DV_A), m96=_block_diag_ones(W_B, DV_B),
        m64c=_block_diag_ones(W_C, N_C),
    )
    y_p, hgrn_p, gla_p, rwkv_p, shift_p = _trunk(x_prompt, None, params, PROMPT_ROWS, PROMPT_TSEQ, PROMPT_CHUNK)
    tok = x_sample.shape[1]
    y_s, hgrn_s, gla_s, rwkv_s, shift_s = _trunk(
        x_sample, (state_hgrn, state_gla, state_rwkv, state_shift), params, SAMPLE_ROWS, tok, tok)
    return (y_p, y_s, hgrn_p, gla_p, rwkv_p, shift_p, hgrn_s, gla_s, rwkv_s, shift_s)
```

```python
import collections
import functools

import jax
import jax.numpy as jnp
from jax import lax
from jax.experimental import pallas as pl
from jax.experimental.pallas import tpu as pltpu

F32 = jnp.float32
BF16 = jnp.bfloat16

D_MODEL = 1024
DEPTH = 4
HEAD = 64
W_A = D_MODEL // 4
H_A = W_A // HEAD
DK_A = HEAD
DV_A = W_A // H_A
W_B = (D_MODEL - W_A) // 2
H_B = 4
DV_B = W_B // H_B
DK_B = DV_B // 2
GLA_LR = 16
GLA_NORM = 16.0
W_C = D_MODEL - W_A - W_B
H_C = W_C // HEAD
N_C = HEAD
DECAY_LR = 32
AAA_LR = 32
GN_EPS = 64e-5
TINY = 1e-30
D_A_IN = 2 * H_A * DK_A + 2 * W_A
D_B_IN = 2 * H_B * DK_B + 2 * W_B + GLA_LR
D_SHIFT = 3 * W_C + DECAY_LR + AAA_LR
D_IN = D_A_IN + D_B_IN + D_SHIFT + W_C

QA = (0, H_A * DK_A)
ZA = (QA[1], QA[1] + H_A * DK_A)
IA = (ZA[1], ZA[1] + W_A)
GATE_A = (IA[1], IA[1] + W_A)
QB = (D_A_IN, D_A_IN + H_B * DK_B)
KB = (QB[1], QB[1] + H_B * DK_B)
VB = (KB[1], KB[1] + W_B)
GKL = (VB[1], VB[1] + GLA_LR)
GATE_B = (GKL[1], GKL[1] + W_B)
PC = (D_A_IN + D_B_IN, D_A_IN + D_B_IN + D_SHIFT)
GATE_C = (PC[1], PC[1] + W_C)
SH_R = (0, W_C)
SH_WL = (SH_R[1], SH_R[1] + DECAY_LR)
SH_K = (SH_WL[1], SH_WL[1] + W_C)
SH_V = (SH_K[1], SH_K[1] + W_C)
SH_AL = (SH_V[1], SH_V[1] + AAA_LR)
OUT_A = 0
OUT_B = W_A
OUT_C = W_A + W_B

SUBLANES = 8
LANES = 128
VMEM_LIMIT_BYTES = 56 * 1024 * 1024

Cfg = collections.namedtuple("Cfg", "layer rows tseq chunk has_state final")

NN = (((1,), (0,)), ((), ()))
NT = (((1,), (1,)), ((), ()))


def _dot(a, b):
    return jnp.dot(a.astype(BF16), b.astype(BF16), preferred_element_type=F32)


def _split(a):
    hi = a.astype(BF16)
    return hi, (a - hi.astype(F32)).astype(BF16)


def _dot3s(a_split, b_split, dims):
    (ah, al), (bh, bl) = a_split, b_split
    return (lax.dot_general(ah, bh, dims, preferred_element_type=F32)
            + lax.dot_general(ah, bl, dims, preferred_element_type=F32)
            + lax.dot_general(al, bh, dims, preferred_element_type=F32))


def _dot_nt3(a, b):
    return _dot3s(_split(a), _split(b), NT)


def _dot_hi(a, b):
    return jnp.dot(a, b, precision=lax.Precision.HIGHEST, preferred_element_type=F32)


def _segsum(x, m):
    hi, lo = _split(x)
    return (jnp.dot(hi, m, preferred_element_type=F32)
            + jnp.dot(lo, m, preferred_element_type=F32))


def _softplus(x):
    return jnp.maximum(x, 0.0) + jnp.log1p(jnp.exp(-jnp.abs(x)))


def _logsig(x):
    return jnp.minimum(x, 0.0) - jnp.log1p(jnp.exp(-jnp.abs(x)))


def _silu(x):
    return x * jax.nn.sigmoid(x)


def _chunk_scan(x, chunk, pos, reverse=False):
    n = x.shape[0]
    s = 1
    while s < chunk:
        if reverse:
            x = x + jnp.where(pos < chunk - s, pltpu.roll(x, n - s, axis=0), 0.0)
        else:
            x = x + jnp.where(pos >= s, pltpu.roll(x, s, axis=0), 0.0)
        s *= 2
    return x


def _chunk_row(x, m, chunk, pos):
    rows, w = x.shape
    if chunk % SUBLANES == 0:
        x3 = x.reshape(rows // chunk, chunk, w)
        return jnp.broadcast_to(x3[:, m:m + 1, :], x3.shape).reshape(rows, w)
    z = jnp.where(pos == m, x, 0.0)
    return jnp.where(pos >= m, _chunk_scan(z, chunk, pos), _chunk_scan(z, chunk, pos, reverse=True))


def _store_chunk_columns(kt_ref, lane0, x, c):
    tile = max(c, SUBLANES)
    for t0 in range(0, x.shape[0], tile):
        xt = x[t0:t0 + tile, :].T.astype(BF16)
        for j in range(tile // c):
            kt_ref[t0 // c + j, :, lane0:lane0 + c] = xt[:, j * c:(j + 1) * c]


def _store_chunk_decay(d_ref, decay, c):
    for ch in range(decay.shape[0] // c):
        last = (ch + 1) * c - 1
        t0 = last // SUBLANES * SUBLANES
        col = decay[t0:t0 + SUBLANES, :].T[:, last - t0:last - t0 + 1]
        d_ref[ch] = jnp.broadcast_to(col, d_ref.shape[1:])


def _gla_block_prep(r, q, k, v, g, pos, masks, names, heads, dk, dv, out_base, c):
    qx_name, kt_name, d_name = names
    causal = masks[0]
    cum = _chunk_scan(g, c, pos)
    rel = cum - _chunk_row(cum, c // 2 - 1, c, pos)
    q_rel = q * jnp.exp(rel)
    k_rel = k * jnp.exp(-rel)
    eg = jnp.exp(cum)
    r[qx_name][...] = q * eg
    k_end = k * jnp.exp(_chunk_row(cum, c - 1, c, pos) - cum)
    _store_chunk_columns(r[kt_name], 0, k_end, c)
    _store_chunk_decay(r[d_name], eg, c)
    for h in range(heads):
        ks = slice(h * dk, (h + 1) * dk)
        sc = jnp.where(causal, _dot_nt3(q_rel[:, ks], k_rel[:, ks]), 0.0)
        r["o"][:, out_base + h * dv:out_base + (h + 1) * dv] = _dot(sc, v[:, h * dv:(h + 1) * dv])


def _unit_lower_inverses(a_list, eye, same, c):
    if c % SUBLANES:
        w, p = [eye - a for a in a_list], a_list
        n = 2
        while n < c:
            p = [_dot3s(_split(x), _split(x), NN) for x in p]
            w = [x + _dot3s(_split(x), _split(y), NN) for x, y in zip(w, p)]
            n *= 2
        return w
    nblk = eye.shape[0] // c

    def side_by_side(x):
        out = x[0:c]
        for k in range(1, nblk):
            out = out + x[k * c:(k + 1) * c]
        return out

    def block_diag(x):
        return jnp.where(same, jnp.tile(x, (nblk, 1)), 0.0)

    eye_w = side_by_side(eye)
    pw = [side_by_side(a) for a in a_list]
    w = [eye_w - x for x in pw]
    pw = [_dot3s(_split(x), _split(a), NN) for x, a in zip(pw, a_list)]
    n = 2
    while n < c:
        p_bd = [_split(block_diag(x)) for x in pw]
        n *= 2
        if n < c:
            prod = [_dot3s(_split(jnp.concatenate([x, y], axis=0)), s, NN) for x, y, s in zip(pw, w, p_bd)]
            pw = [z[0:c] for z in prod]
            w = [y + z[c:2 * c] for y, z in zip(w, prod)]
        else:
            w = [y + _dot3s(_split(y), s, NN) for y, s in zip(w, p_bd)]
    return [block_diag(y) for y in w]


def _rwkv_block_prep(r, lw, kk, b, pos, masks, c):
    causal, strict, eye, same = masks
    nch = lw.shape[0] // c
    rr, kmod, v = r["rc"][...], r["kc"][...], r["vc"][...]
    lc = _chunk_scan(lw, c, pos)
    el = jnp.exp(lc)
    inv_cum = jnp.exp(-lc)
    kk_in = kk * jnp.exp(lc - lw)
    b_out = b * inv_cum
    k_out = kmod * inv_cum
    r_in = rr * el
    to_end = jnp.exp(_chunk_row(lc, c - 1, c, pos) - lc)
    _store_chunk_columns(r["bkt"], 0, b * to_end, c)
    _store_chunk_columns(r["bkt"], c, kmod * to_end, c)
    _store_chunk_decay(r["dc"], el, c)
    for ch in range(nch):
        r["khr"][ch, c:2 * c, :] = r_in[ch * c:(ch + 1) * c, :]
    cols = [slice(h * N_C, (h + 1) * N_C) for h in range(H_C)]
    a = [jnp.where(strict, _dot_nt3(kk_in[:, cs], b_out[:, cs]), 0.0) for cs in cols]
    w_s = [_split(x) for x in _unit_lower_inverses(a, eye, same, c)]
    bv = [_dot(jnp.where(strict, _dot_nt3(kk_in[:, cs], k_out[:, cs]), 0.0), v[:, cs]) for cs in cols]
    for h, cs in enumerate(cols):
        z = _dot3s(w_s[h], _split(jnp.concatenate([kk_in[:, cs], bv[h]], axis=1)), NN)
        r["u0"][:, cs] = z[:, N_C:2 * N_C]
        for ch in range(nch):
            r["khr"][ch, 0:c, cs] = z[ch * c:(ch + 1) * c, 0:N_C]
    for h, cs in enumerate(cols):
        r["rb"][h] = jnp.where(causal, _dot_nt3(r_in[:, cs], b_out[:, cs]), 0.0).astype(BF16)
        rk = jnp.where(causal, _dot_nt3(r_in[:, cs], k_out[:, cs]), 0.0)
        r["o"][:, OUT_C + h * N_C:OUT_C + (h + 1) * N_C] = _dot(rk, v[:, cs])


def _recurrences(r, p, cfg):
    rows, c, tb = cfg.rows, cfg.chunk, cfg.tseq
    nseq = rows // tb
    steps = tb // c
    t = pl.program_id(1)
    pos = lax.broadcasted_iota(jnp.int32, (rows, 1), 0) & (c - 1)
    shift = c.bit_length() - 1
    i = lax.broadcasted_iota(jnp.int32, (rows, rows), 0)
    j = lax.broadcasted_iota(jnp.int32, (rows, rows), 1)
    same = (i >> shift) == (j >> shift)
    masks = (same & (j <= i), same & (j < i), jnp.where(i == j, 1.0, 0.0), same)

    @pl.when(t == 0)
    def _init():
        if cfg.has_state:
            r["sa"][...] = r["sa_in"][...]
            r["sb"][...] = r["sb_in"][...]
            for q in range(nseq):
                for h in range(H_C):
                    r["sct"][q, h] = r["sc_in"][q, h].T
        else:
            r["sa"][...] = jnp.zeros(r["sa"].shape, F32)
            r["sb"][...] = jnp.zeros(r["sb"].shape, F32)
            r["sct"][...] = jnp.zeros(r["sct"].shape, F32)

    lw, kk, b = _rwkv_gates(r, p, cfg)
    _rwkv_block_prep(r, lw, kk, b, pos, masks, c)
    g_a, g_b = _gla_gates(r, p, cfg)
    _gla_block_prep(r, p[:, QA[0]:QA[1]] * (DK_A ** -0.5), r["ka"][...], p[:, IA[0]:IA[1]], g_a, pos, masks,
                    ("qa", "kta", "da"), H_A, DK_A, DV_A, OUT_A, c)
    _gla_block_prep(r, p[:, QB[0]:QB[1]] * (DK_B ** -0.5), p[:, KB[0]:KB[1]], p[:, VB[0]:VB[1]], g_b, pos, masks,
                    ("qb", "ktb", "db"), H_B, DK_B, DV_B, OUT_B, c)

    def body(n, carry):
        for q in range(nseq):
            ch = q * steps + n
            row0 = q * tb + n * c
            rs = slice(row0, row0 + c) if isinstance(n, int) else pl.ds(pl.multiple_of(row0, c), c)
            for (qn, kn, dn, st, v, base, heads, dk, dv) in (
                    ("qa", "kta", "da", r["sa"], p[rs, IA[0]:IA[1]], OUT_A, H_A, DK_A, DV_A),
                    ("qb", "ktb", "db", r["sb"], p[rs, VB[0]:VB[1]], OUT_B, H_B, DK_B, DV_B)):
                qx, kt, d = r[qn][rs, :], r[kn][ch], r[dn][ch]
                for h in range(heads):
                    ks = slice(h * dk, (h + 1) * dk)
                    vs = slice(h * dv, (h + 1) * dv)
                    s = st[q, h]
                    r["oi"][rs, base + h * dv:base + (h + 1) * dv] = _dot(qx[:, ks], s)
                    st[q, h] = s * d[ks, 0:dv] + jnp.dot(kt[ks, :], v[:, vs].astype(BF16),
                                                         preferred_element_type=F32)
            khr, bkt, d = r["khr"][ch], r["bkt"][ch], r["dc"][ch]
            u0, v = r["u0"][rs, :], r["vc"][rs, :]
            for h in range(H_C):
                cs = slice(h * N_C, (h + 1) * N_C)
                s = r["sct"][q, h]
                x = _dot(khr[:, cs], s)
                u = -(x[0:c] + u0[:, cs])
                r["u"][rs, cs] = u
                r["oi"][rs, OUT_C + h * N_C:OUT_C + (h + 1) * N_C] = x[c:2 * c]
                r["sct"][q, h] = s * d[cs, 0:N_C] + jnp.dot(
                    bkt[cs, :], jnp.concatenate([u, v[:, cs]], axis=0).astype(BF16),
                    preferred_element_type=F32)
        return carry

    if steps == 1:
        body(0, 0)
    else:
        lax.fori_loop(0, steps, body, 0)

    r["o"][...] += r["oi"][...]
    for h in range(H_C):
        cs = slice(h * N_C, (h + 1) * N_C)
        r["o"][:, OUT_C + h * N_C:OUT_C + (h + 1) * N_C] += jnp.dot(
            r["rb"][h], r["u"][:, cs].astype(BF16), preferred_element_type=F32)

    @pl.when(t == pl.num_programs(1) - 1)
    def _emit():
        for q in range(nseq):
            for h in range(H_C):
                r["sc"][q, h] = r["sct"][q, h].T


def _layer_kernel(names, cfg, *refs):
    r = dict(zip(names, refs))
    rows = cfg.rows
    p = r["p"]

    x = r["x"][...].reshape(rows, D_MODEL)
    hn = x * lax.rsqrt(jnp.mean(x * x, axis=-1, keepdims=True) + 1e-6) * r["norm_w"][...]
    p[...] = jnp.dot(hn.astype(BF16), r["w_in"][...], preferred_element_type=F32)

    _recurrences(r, p, cfg)

    o = r["o"]
    oa = o[:, OUT_A:OUT_A + W_A]
    ya = (oa * lax.rsqrt(_segsum(oa * oa, r["m64a"][...]) * (1.0 / DV_A) + 1e-5)
          * r["hgrn_norm_w"][...] * _silu(p[:, GATE_A[0]:GATE_A[1]]))
    ob = o[:, OUT_B:OUT_B + W_B]
    yb = (ob * lax.rsqrt(_segsum(ob * ob, r["m96"][...]) * (1.0 / DV_B) + 1e-5)
          * r["gla_norm_w"][...] * _silu(p[:, GATE_B[0]:GATE_B[1]]))
    oc = o[:, OUT_C:D_MODEL]
    m64c = r["m64c"][...]
    dc = oc - _segsum(oc, m64c) * (1.0 / N_C)
    ocn = dc * lax.rsqrt(_segsum(dc * dc, m64c) * (1.0 / N_C) + GN_EPS) * r["ln_w"][...] + r["ln_b"][...]
    bonus = _segsum(r["rc"][...] * r["kc"][...] * r["r_k"][...], m64c) * r["vc"][...]
    yc = (ocn + bonus) * _silu(p[:, GATE_C[0]:GATE_C[1]])
    y = jnp.concatenate([ya, yb, yc], axis=-1)
    out = r["x"][...].reshape(rows, D_MODEL) + jnp.dot(y.astype(BF16), r["w_out"][...], preferred_element_type=F32)
    if cfg.final:
        out = out * lax.rsqrt(jnp.mean(out * out, axis=-1, keepdims=True) + 1e-6) * r["final_norm_w"][...]
    r["xo"][...] = out.reshape(r["xo"].shape)


def _gla_gates(r, p, cfg):
    hl = r["hgrn_lb"][...]
    e = jnp.exp(hl - jnp.max(hl, axis=0, keepdims=True))
    sm = e / jnp.sum(e, axis=0, keepdims=True)
    cum = sm[0:1]
    for j in range(1, cfg.layer + 1):
        cum = cum + sm[j:j + 1]
    lb = cum - sm[0:1]
    za = p[:, ZA[0]:ZA[1]]
    ls = _logsig(za)
    la = jnp.log(jnp.maximum(lb, TINY))
    bb = jnp.log1p(-lb) + ls
    lae = jnp.maximum(la, bb) + jnp.log1p(jnp.exp(-jnp.abs(la - bb)))
    g_a = jnp.where(lb > 0.0, lae, ls)
    r["ka"][...] = (1.0 - lb) * jax.nn.sigmoid(-za)
    g_b = _logsig(_dot_hi(p[:, GKL[0]:GKL[1]], r["gk_w2"][...]) + r["gk_b"][...]) * (1.0 / GLA_NORM)
    return g_a, g_b


def _rwkv_gates(r, p, cfg):
    rows, tseq = cfg.rows, cfg.tseq
    t = pl.program_id(1)
    pc = p[:, PC[0]:PC[1]]
    xs = r["xs"]
    if cfg.has_state:
        row = lax.broadcasted_iota(jnp.int32, (rows, 1), 0)
        prev = jnp.where((row & (tseq - 1)) == 0, r["shift_rows"][...], pltpu.roll(pc, 1, axis=0))
        r["pct"][...] = pc
    else:
        @pl.when(t == 0)
        def _init():
            r["pct"][...] = jnp.zeros(r["pct"].shape, F32)

        xs[...] = pltpu.roll(pc, 1, axis=0)
        for q in range(rows // tseq):
            xs[q * tseq:q * tseq + 1, :] = r["pct"][(q + 1) * SUBLANES - 1:(q + 1) * SUBLANES, :]
            r["pct"][q * SUBLANES:(q + 1) * SUBLANES, :] = p[(q + 1) * tseq - SUBLANES:(q + 1) * tseq, PC[0]:PC[1]]
        prev = xs[...]
    xs[...] = pc + (prev - pc) * r["mu"][...]
    kc = xs[:, SH_K[0]:SH_K[1]]
    wlog = -_softplus(-(r["w0"][...] + _dot_hi(jnp.tanh(xs[:, SH_WL[0]:SH_WL[1]]), r["w2"][...]))) - 0.5
    lw = -jnp.exp(wlog)
    av = jax.nn.sigmoid(r["a0"][...] + _dot_hi(xs[:, SH_AL[0]:SH_AL[1]], r["a2"][...]))
    kk = kc * r["k_k"][...]
    kk = kk * lax.rsqrt(jnp.maximum(_segsum(kk * kk, r["m64c"][...]), 1e-24))
    r["rc"][...] = xs[:, SH_R[0]:SH_R[1]]
    r["vc"][...] = xs[:, SH_V[0]:SH_V[1]]
    r["kc"][...] = kc * (1.0 + (av - 1.0) * r["k_a"][...])
    return lw, kk, kk * av


def _block_diag_ones(width, seg):
    i = jnp.arange(width) // seg
    return (i[:, None] == i[None, :]).astype(BF16)


def _run_layer(cfg, nb, nt, x3, state, new_states, params):
    l, rows = cfg.layer, cfg.rows
    nseq = rows // cfg.tseq
    nseq_total = nb * nseq
    assert not cfg.has_state or nt == 1
    x_block = (1, rows, D_MODEL) if cfg.has_state else (nseq, cfg.tseq, D_MODEL)
    ins, specs = [], []

    def add(name, arr, spec):
        ins.append((name, arr))
        specs.append(spec)

    def layer_row(name, arr):
        w = arr.shape[-1]
        add(name, arr.reshape(DEPTH, 1, w), pl.BlockSpec((None, 1, w), lambda b, t: (l, 0, 0)))

    once = pl.Buffered(1)

    def layer_mat(name, arr):
        add(name, arr, pl.BlockSpec((None,) + arr.shape[1:], lambda b, t: (l, 0, 0), pipeline_mode=once))

    def whole(name, arr):
        add(name, arr, pl.BlockSpec(arr.shape, lambda b, t: (0,) * arr.ndim, pipeline_mode=once))

    add("x", x3, pl.BlockSpec(x_block, lambda b, t: (b, t, 0)))
    if cfg.has_state:
        st_a, st_b, st_c, shift_rows = state
        add("shift_rows", shift_rows, pl.BlockSpec((None, rows, D_SHIFT), lambda b, t: (l, b, 0)))
        add("sa_in", st_a, pl.BlockSpec((None, nseq, H_A, DK_A, DV_A), lambda b, t: (l, b, 0, 0, 0)))
        add("sb_in", st_b, pl.BlockSpec((None, nseq, H_B, DK_B, DV_B), lambda b, t: (l, b, 0, 0, 0)))
        add("sc_in", st_c, pl.BlockSpec((None, nseq, H_C, N_C, N_C), lambda b, t: (l, b, 0, 0, 0)))
    layer_row("norm_w", params["norm_w"])
    layer_mat("w_in", params["w_in"])
    whole("hgrn_lb", params["hgrn_lb"])
    layer_row("hgrn_norm_w", params["hgrn_norm_w"])
    layer_mat("gk_w2", params["gla_gk_w2"])
    layer_row("gk_b", params["gla_gk_b"])
    layer_row("gla_norm_w", params["gla_norm_w"])
    layer_row("mu", params["rwkv_mu"])
    layer_row("w0", params["rwkv_w0"])
    layer_mat("w2", params["rwkv_w2"])
    layer_row("a0", params["rwkv_a0"])
    layer_mat("a2", params["rwkv_a2"])
    layer_row("k_k", params["rwkv_k_k"])
    layer_row("k_a", params["rwkv_k_a"])
    layer_row("r_k", params["rwkv_r_k"])
    layer_row("ln_w", params["rwkv_ln_w"])
    layer_row("ln_b", params["rwkv_ln_b"])
    layer_mat("w_out", params["w_out"])
    if cfg.final:
        whole("final_norm_w", params["final_norm_w"].reshape(1, D_MODEL))
    whole("m64a", params["m64a"])
    whole("m96", params["m96"])
    whole("m64c", params["m64c"])

    pct_rows = rows if cfg.has_state else nseq * SUBLANES
    state_dims = ((H_A, DK_A, DV_A), (H_B, DK_B, DV_B), (H_C, N_C, N_C))
    aliases = {}
    for k, (name, arr, dims) in enumerate(zip(("sa_all", "sb_all", "sc_all"), new_states, state_dims)):
        assert arr.shape == (DEPTH, nseq_total) + dims
        aliases[len(ins)] = 1 + k
        add(name, arr, pl.BlockSpec(memory_space=pl.ANY))
    outs = [("xo", jax.ShapeDtypeStruct(x3.shape, F32), pl.BlockSpec(x_block, lambda b, t: (b, t, 0)))]
    outs += [(name, jax.ShapeDtypeStruct((DEPTH, nseq_total) + dims, F32),
              pl.BlockSpec((None, nseq) + dims, lambda b, t: (l, b, 0, 0, 0)))
             for name, dims in zip(("sa", "sb", "sc"), state_dims)]
    outs += [("pct", jax.ShapeDtypeStruct((nb * pct_rows, D_SHIFT), F32),
              pl.BlockSpec((pct_rows, D_SHIFT), lambda b, t: (b, 0)))]
    a_w, b_w, c = H_A * DK_A, H_B * DK_B, cfg.chunk
    nch = rows // c
    scratch = [("p", (rows, D_IN), F32), ("xs", (rows, D_SHIFT), F32), ("o", (rows, D_MODEL), F32),
               ("oi", (rows, D_MODEL), F32),
               ("qa", (rows, a_w), F32), ("ka", (rows, a_w), F32), ("qb", (rows, b_w), F32),
               ("rc", (rows, W_C), F32), ("vc", (rows, W_C), F32), ("kc", (rows, W_C), F32),
               ("u0", (rows, W_C), F32), ("u", (rows, W_C), F32),
               ("kta", (nch, a_w, c), BF16), ("da", (nch, a_w, LANES), F32),
               ("ktb", (nch, b_w, c), BF16), ("db", (nch, b_w, LANES), F32),
               ("bkt", (nch, W_C, 2 * c), BF16), ("dc", (nch, W_C, LANES), F32),
               ("khr", (nch, 2 * c, W_C), F32), ("rb", (H_C, rows, rows), BF16),
               ("sct", (nseq, H_C, N_C, N_C), F32)]
    names = tuple(n for n, _ in ins) + tuple(n for n, _, _ in outs) + tuple(n for n, _, _ in scratch)
    return pl.pallas_call(
        functools.partial(_layer_kernel, names, cfg),
        grid=(nb, nt),
        in_specs=specs,
        out_specs=[s for _, _, s in outs],
        out_shape=[s for _, s, _ in outs],
        scratch_shapes=[pltpu.VMEM(shape, dt) for _, shape, dt in scratch],
        input_output_aliases=aliases,
        compiler_params=pltpu.CompilerParams(
            dimension_semantics=("arbitrary", "arbitrary"),
            vmem_limit_bytes=VMEM_LIMIT_BYTES),
        name=f"layer{l}_{'sample' if cfg.has_state else 'prompt'}",
    )(*[a for _, a in ins])


def _trunk(x, state, params, rows, tseq, chunk):
    bsz, tlen, _ = x.shape
    has_state = state is not None
    nseq = rows // tseq
    assert rows % tseq == 0 and tseq % chunk == 0 and bsz % nseq == 0 and tlen % tseq == 0
    nb, nt = bsz // nseq, tlen // tseq
    if has_state:
        assert tlen == tseq
        st_a, st_b, st_c, st_s = state
        shift_rows = jnp.pad(st_s[:, :, None, :], ((0, 0), (0, 0), (0, tlen - 1), (0, 0)))
        state = (st_a, st_b, st_c, shift_rows.reshape(DEPTH, bsz * tlen, D_SHIFT))
        x3 = x.reshape(nb, rows, D_MODEL)
    else:
        x3 = x
    new_states = tuple(jnp.zeros((DEPTH, bsz) + dims, F32)
                       for dims in ((H_A, DK_A, DV_A), (H_B, DK_B, DV_B), (H_C, N_C, N_C)))
    new_s = []
    for l in range(DEPTH):
        cfg = Cfg(layer=l, rows=rows, tseq=tseq, chunk=chunk, has_state=has_state, final=(l == DEPTH - 1))
        x3, *new_states, pct = _run_layer(cfg, nb, nt, x3, state, tuple(new_states), params)
        if has_state:
            new_s.append(pct.reshape(bsz, tlen, D_SHIFT)[:, tlen - 1])
        else:
            new_s.append(pct.reshape(bsz, SUBLANES, D_SHIFT)[:, SUBLANES - 1])
    return (x3.reshape(bsz, tlen, D_MODEL), *new_states, jnp.stack(new_s))


PROMPT_ROWS = 256
PROMPT_TSEQ = 32
PROMPT_CHUNK = 32
SAMPLE_ROWS = 32


def kernel(x_prompt, x_sample, state_hgrn, state_gla, state_rwkv, state_shift, norm_w, w_in, hgrn_lb, hgrn_norm_w, gla_gk_w2, gla_gk_b, gla_norm_w, rwkv_mu, rwkv_w0, rwkv_w2, rwkv_a0, rwkv_a2, rwkv_k_k, rwkv_k_a, rwkv_r_k, rwkv_ln_w, rwkv_ln_b, w_out, final_norm_w):
    params = dict(
        norm_w=norm_w, w_in=w_in.astype(BF16), hgrn_lb=hgrn_lb, hgrn_norm_w=hgrn_norm_w,
        gla_gk_w2=gla_gk_w2, gla_gk_b=gla_gk_b, gla_norm_w=gla_norm_w, rwkv_mu=rwkv_mu,
        rwkv_w0=rwkv_w0, rwkv_w2=rwkv_w2, rwkv_a0=rwkv_a0, rwkv_a2=rwkv_a2, rwkv_k_k=rwkv_k_k,
        rwkv_k_a=rwkv_k_a, rwkv_r_k=rwkv_r_k, rwkv_ln_w=rwkv_ln_w, rwkv_ln_b=rwkv_ln_b,
        w_out=w_out.astype(BF16), final_norm_w=final_norm_w,
        m64a=_block_diag_ones(W_A, DV_A), m96=_block_diag_ones(W_B, DV_B),
        m64c=_block_diag_ones(W_C, N_C),
    )
    y_p, hgrn_p, gla_p, rwkv_p, shift_p = _trunk(x_prompt, None, params, PROMPT_ROWS, PROMPT_TSEQ, PROMPT_CHUNK)
    tok = x_sample.shape[1]
    y_s, hgrn_s, gla_s, rwkv_s, shift_s = _trunk(
        x_sample, (state_hgrn, state_gla, state_rwkv, state_shift), params, SAMPLE_ROWS, tok, tok)
    return (y_p, y_s, hgrn_p, gla_p, rwkv_p, shift_p, hgrn_s, gla_s, rwkv_s, shift_s)
```

```python
import collections
import functools

import jax
import jax.numpy as jnp
from jax import lax
from jax.experimental import pallas as pl
from jax.experimental.pallas import tpu as pltpu

F32 = jnp.float32
BF16 = jnp.bfloat16

D_MODEL = 1024
DEPTH = 4
HEAD = 64
W_A = D_MODEL // 4
H_A = W_A // HEAD
DK_A = HEAD
DV_A = W_A // H_A
W_B = (D_MODEL - W_A) // 2
H_B = 4
DV_B = W_B // H_B
DK_B = DV_B // 2
GLA_LR = 16
GLA_NORM = 16.0
W_C = D_MODEL - W_A - W_B
H_C = W_C // HEAD
N_C = HEAD
DECAY_LR = 32
AAA_LR = 32
GN_EPS = 64e-5
TINY = 1e-30
D_A_IN = 2 * H_A * DK_A + 2 * W_A
D_B_IN = 2 * H_B * DK_B + 2 * W_B + GLA_LR
D_SHIFT = 3 * W_C + DECAY_LR + AAA_LR
D_IN = D_A_IN + D_B_IN + D_SHIFT + W_C


def _ranges(pieces):
    out, pos = {}, 0
    for name, width in pieces:
        out[name] = (pos, pos + width)
        pos += width
    return out


_MODEL_COLS = _ranges((
    ("qa", H_A * DK_A), ("za", H_A * DK_A), ("ia", W_A), ("gate_a", W_A),
    ("qb", H_B * DK_B), ("kb", H_B * DK_B), ("vb", W_B), ("gkl", GLA_LR), ("gate_b", W_B),
    ("r", W_C), ("wl", DECAY_LR), ("kc", W_C), ("vc", W_C), ("al", AAA_LR), ("gate_c", W_C)))
_COL_ORDER = ("qa", "za", "ia", "gate_a", "qb", "kb", "vb", "gate_b", "r", "kc", "vc", "gate_c",
              "gkl", "wl", "al")
_COLS = _ranges(tuple((n, _MODEL_COLS[n][1] - _MODEL_COLS[n][0]) for n in _COL_ORDER))
QA, ZA, IA, GATE_A = _COLS["qa"], _COLS["za"], _COLS["ia"], _COLS["gate_a"]
QB, KB, VB, GATE_B, GKL = _COLS["qb"], _COLS["kb"], _COLS["vb"], _COLS["gate_b"], _COLS["gkl"]
GATE_C = _COLS["gate_c"]
PC_MAIN = (_COLS["r"][0], _COLS["vc"][1])
PC_TAIL = (_COLS["wl"][0], _COLS["al"][1])
_SHIFT_ORDER = ("r", "kc", "vc", "wl", "al")
_SHIFT = _ranges(tuple((n, _MODEL_COLS[n][1] - _MODEL_COLS[n][0]) for n in _SHIFT_ORDER))
SH_R, SH_K, SH_V, SH_WL, SH_AL = (_SHIFT[n] for n in _SHIFT_ORDER)
N_MAIN = PC_MAIN[1] - PC_MAIN[0]
OUT_A = 0
OUT_B = W_A
OUT_C = W_A + W_B

SUBLANES = 8
LANES = 128
VMEM_LIMIT_BYTES = 56 * 1024 * 1024

Cfg = collections.namedtuple("Cfg", "layer rows tseq chunk has_state final")

NN = (((1,), (0,)), ((), ()))
NT = (((1,), (1,)), ((), ()))


def _dot(a, b):
    return jnp.dot(a.astype(BF16), b.astype(BF16), preferred_element_type=F32)


def _split(a):
    hi = a.astype(BF16)
    return hi, (a - hi.astype(F32)).astype(BF16)


def _dot3s(a_split, b_split, dims):
    (ah, al), (bh, bl) = a_split, b_split
    return (lax.dot_general(ah, bh, dims, preferred_element_type=F32)
            + lax.dot_general(ah, bl, dims, preferred_element_type=F32)
            + lax.dot_general(al, bh, dims, preferred_element_type=F32))


def _dot_nt(a, b):
    return lax.dot_general(a.astype(BF16), b.astype(BF16), NT, preferred_element_type=F32)


def _dot_nt3(a, b):
    return _dot3s(_split(a), _split(b), NT)


def _dot_hi(a, b):
    return jnp.dot(a, b, precision=lax.Precision.HIGHEST, preferred_element_type=F32)


def _segsum(x, m):
    hi, lo = _split(x)
    return (jnp.dot(hi, m, preferred_element_type=F32)
            + jnp.dot(lo, m, preferred_element_type=F32))


def _softplus(x):
    return jnp.maximum(x, 0.0) + jnp.log1p(jnp.exp(-jnp.abs(x)))


def _logsig(x):
    return jnp.minimum(x, 0.0) - jnp.log1p(jnp.exp(-jnp.abs(x)))


def _silu(x):
    return x * jax.nn.sigmoid(x)


def _chunk_scan(x, chunk, pos, reverse=False):
    n = x.shape[0]
    s = 1
    while s < chunk:
        if reverse:
            x = x + jnp.where(pos < chunk - s, pltpu.roll(x, n - s, axis=0), 0.0)
        else:
            x = x + jnp.where(pos >= s, pltpu.roll(x, s, axis=0), 0.0)
        s *= 2
    return x


def _chunk_row(x, m, chunk, pos):
    rows, w = x.shape
    if chunk % SUBLANES == 0:
        x3 = x.reshape(rows // chunk, chunk, w)
        return jnp.broadcast_to(x3[:, m:m + 1, :], x3.shape).reshape(rows, w)
    z = jnp.where(pos == m, x, 0.0)
    return jnp.where(pos >= m, _chunk_scan(z, chunk, pos), _chunk_scan(z, chunk, pos, reverse=True))


def _store_chunk_columns(kt_ref, lane0, x, c):
    tile = max(c, SUBLANES)
    for t0 in range(0, x.shape[0], tile):
        xt = x[t0:t0 + tile, :].T.astype(BF16)
        for j in range(tile // c):
            kt_ref[t0 // c + j, :, lane0:lane0 + c] = xt[:, j * c:(j + 1) * c]


def _store_chunk_decay(d_ref, decay, c):
    for ch in range(decay.shape[0] // c):
        last = (ch + 1) * c - 1
        t0 = last // SUBLANES * SUBLANES
        col = decay[t0:t0 + SUBLANES, :].T[:, last - t0:last - t0 + 1]
        d_ref[ch] = jnp.broadcast_to(col, d_ref.shape[1:])


def _gla_block_prep(r, q, k, v, g, pos, masks, names, heads, dk, dv, out_base, c):
    qx_name, kt_name, d_name = names
    causal = masks[0]
    cum = _chunk_scan(g, c, pos)
    rel = cum - _chunk_row(cum, c // 2 - 1, c, pos)
    q_rel = q * jnp.exp(rel)
    k_rel = k * jnp.exp(-rel)
    eg = jnp.exp(cum)
    r[qx_name][...] = q * eg
    k_end = k * jnp.exp(_chunk_row(cum, c - 1, c, pos) - cum)
    _store_chunk_columns(r[kt_name], 0, k_end, c)
    _store_chunk_decay(r[d_name], eg, c)
    for h in range(heads):
        ks = slice(h * dk, (h + 1) * dk)
        sc = jnp.where(causal, _dot_nt(q_rel[:, ks], k_rel[:, ks]), 0.0)
        r["o"][:, out_base + h * dv:out_base + (h + 1) * dv] = _dot(sc, v[:, h * dv:(h + 1) * dv])


def _unit_lower_inverses(a_list, eye, same, c):
    if c % SUBLANES:
        w, p = [eye - a for a in a_list], a_list
        n = 2
        while n < c:
            p = [_dot3s(_split(x), _split(x), NN) for x in p]
            w = [x + _dot3s(_split(x), _split(y), NN) for x, y in zip(w, p)]
            n *= 2
        return w
    nblk = eye.shape[0] // c

    def side_by_side(x):
        out = x[0:c]
        for k in range(1, nblk):
            out = out + x[k * c:(k + 1) * c]
        return out

    def block_diag(x):
        return jnp.where(same, jnp.tile(x, (nblk, 1)), 0.0)

    def block_diag_split(x):
        hi = x.astype(BF16).astype(F32)
        return block_diag(hi).astype(BF16), block_diag(x - hi).astype(BF16)

    eye_w = side_by_side(eye)
    pw = [side_by_side(a) for a in a_list]
    w = [eye_w - x for x in pw]
    pw = [_dot3s(_split(x), _split(a), NN) for x, a in zip(pw, a_list)]
    n = 2
    while n < c:
        p_bd = [block_diag_split(x) for x in pw]
        n *= 2
        if n < c:
            prod = [_dot3s(_split(jnp.concatenate([x, y], axis=0)), s, NN) for x, y, s in zip(pw, w, p_bd)]
            pw = [z[0:c] for z in prod]
            w = [y + z[c:2 * c] for y, z in zip(w, prod)]
        else:
            w = [y + _dot3s(_split(y), s, NN) for y, s in zip(w, p_bd)]
    return [block_diag(y) for y in w]


def _rwkv_block_prep(r, lw, kk, b, pos, masks, c):
    causal, strict, eye, same = masks
    nch = lw.shape[0] // c
    rr, kmod, v = r["rc"][...], r["kc"][...], r["vc"][...]
    lc = _chunk_scan(lw, c, pos)
    el = jnp.exp(lc)
    inv_cum = jnp.exp(-lc)
    kk_in = kk * jnp.exp(lc - lw)
    b_out = b * inv_cum
    k_out = kmod * inv_cum
    r_in = rr * el
    to_end = jnp.exp(_chunk_row(lc, c - 1, c, pos) - lc)
    _store_chunk_columns(r["bkt"], 0, b * to_end, c)
    _store_chunk_columns(r["bkt"], c, kmod * to_end, c)
    _store_chunk_decay(r["dc"], el, c)
    for ch in range(nch):
        r["khr"][ch, c:2 * c, :] = r_in[ch * c:(ch + 1) * c, :]
    cols = [slice(h * N_C, (h + 1) * N_C) for h in range(H_C)]
    a = [jnp.where(strict, _dot_nt3(kk_in[:, cs], b_out[:, cs]), 0.0) for cs in cols]
    w_s = [_split(x) for x in _unit_lower_inverses(a, eye, same, c)]
    bv = [_dot(jnp.where(strict, _dot_nt3(kk_in[:, cs], k_out[:, cs]), 0.0), v[:, cs]) for cs in cols]
    for h, cs in enumerate(cols):
        z = _dot3s(w_s[h], _split(jnp.concatenate([kk_in[:, cs], bv[h]], axis=1)), NN)
        r["u0"][:, cs] = z[:, N_C:2 * N_C]
        for ch in range(nch):
            r["khr"][ch, 0:c, cs] = z[ch * c:(ch + 1) * c, 0:N_C]
    for h, cs in enumerate(cols):
        r["rb"][h] = jnp.where(causal, _dot_nt(r_in[:, cs], b_out[:, cs]), 0.0).astype(BF16)
        rk = jnp.where(causal, _dot_nt(r_in[:, cs], k_out[:, cs]), 0.0)
        r["o"][:, OUT_C + h * N_C:OUT_C + (h + 1) * N_C] = _dot(rk, v[:, cs])


def _recurrences(r, p, cfg):
    rows, c, tb = cfg.rows, cfg.chunk, cfg.tseq
    nseq = rows // tb
    steps = tb // c
    t = pl.program_id(1)
    pos = lax.broadcasted_iota(jnp.int32, (rows, 1), 0) & (c - 1)
    shift = c.bit_length() - 1
    i = lax.broadcasted_iota(jnp.int32, (rows, rows), 0)
    j = lax.broadcasted_iota(jnp.int32, (rows, rows), 1)
    same = (i >> shift) == (j >> shift)
    masks = (same & (j <= i), same & (j < i), jnp.where(i == j, 1.0, 0.0), same)

    @pl.when(t == 0)
    def _init():
        if cfg.has_state:
            r["sa"][...] = r["sa_in"][...]
            r["sb"][...] = r["sb_in"][...]
            for q in range(nseq):
                for h in range(H_C):
                    r["sct"][q, h] = r["sc_in"][q, h].T
        else:
            r["sa"][...] = jnp.zeros(r["sa"].shape, F32)
            r["sb"][...] = jnp.zeros(r["sb"].shape, F32)
            r["sct"][...] = jnp.zeros(r["sct"].shape, F32)

    lw, kk, b = _rwkv_gates(r, p, cfg)
    _rwkv_block_prep(r, lw, kk, b, pos, masks, c)
    g_a, g_b = _gla_gates(r, p, cfg)
    _gla_block_prep(r, p[:, QA[0]:QA[1]] * (DK_A ** -0.5), r["ka"][...], p[:, IA[0]:IA[1]], g_a, pos, masks,
                    ("qa", "kta", "da"), H_A, DK_A, DV_A, OUT_A, c)
    _gla_block_prep(r, p[:, QB[0]:QB[1]] * (DK_B ** -0.5), p[:, KB[0]:KB[1]], p[:, VB[0]:VB[1]], g_b, pos, masks,
                    ("qb", "ktb", "db"), H_B, DK_B, DV_B, OUT_B, c)

    def body(n, carry):
        for q in range(nseq):
            ch = q * steps + n
            row0 = q * tb + n * c
            rs = slice(row0, row0 + c) if isinstance(n, int) else pl.ds(pl.multiple_of(row0, c), c)
            for (qn, kn, dn, st, v, base, heads, dk, dv) in (
                    ("qa", "kta", "da", r["sa"], p[rs, IA[0]:IA[1]], OUT_A, H_A, DK_A, DV_A),
                    ("qb", "ktb", "db", r["sb"], p[rs, VB[0]:VB[1]], OUT_B, H_B, DK_B, DV_B)):
                qx, kt, d = r[qn][rs, :], r[kn][ch], r[dn][ch]
                for h in range(heads):
                    ks = slice(h * dk, (h + 1) * dk)
                    vs = slice(h * dv, (h + 1) * dv)
                    s = st[q, h]
                    r["oi"][rs, base + h * dv:base + (h + 1) * dv] = _dot(qx[:, ks], s)
                    st[q, h] = s * d[ks, 0:dv] + jnp.dot(kt[ks, :], v[:, vs].astype(BF16),
                                                         preferred_element_type=F32)
            khr, bkt, d = r["khr"][ch], r["bkt"][ch], r["dc"][ch]
            u0, v = r["u0"][rs, :], r["vc"][rs, :]
            for h in range(H_C):
                cs = slice(h * N_C, (h + 1) * N_C)
                s = r["sct"][q, h]
                x = _dot(khr[:, cs], s)
                u = -(x[0:c] + u0[:, cs])
                r["u"][rs, cs] = u
                r["oi"][rs, OUT_C + h * N_C:OUT_C + (h + 1) * N_C] = x[c:2 * c]
                r["sct"][q, h] = s * d[cs, 0:N_C] + jnp.dot(
                    bkt[cs, :], jnp.concatenate([u, v[:, cs]], axis=0).astype(BF16),
                    preferred_element_type=F32)
        return carry

    if steps == 1:
        body(0, 0)
    else:
        lax.fori_loop(0, steps, body, 0)

    r["o"][...] += r["oi"][...]
    for h in range(H_C):
        cs = slice(h * N_C, (h + 1) * N_C)
        r["o"][:, OUT_C + h * N_C:OUT_C + (h + 1) * N_C] += jnp.dot(
            r["rb"][h], r["u"][:, cs].astype(BF16), preferred_element_type=F32)

    @pl.when(t == pl.num_programs(1) - 1)
    def _emit():
        for q in range(nseq):
            for h in range(H_C):
                r["sc"][q, h] = r["sct"][q, h].T


def _layer_kernel(names, cfg, *refs):
    r = dict(zip(names, refs))
    rows = cfg.rows
    p = r["p"]

    x = r["x"][...].reshape(rows, D_MODEL)
    hn = x * lax.rsqrt(jnp.mean(x * x, axis=-1, keepdims=True) + 1e-6) * r["norm_w"][...]
    p[...] = jnp.dot(hn.astype(BF16), r["w_in"][...], preferred_element_type=F32)

    _recurrences(r, p, cfg)

    o = r["o"]
    oa = o[:, OUT_A:OUT_A + W_A]
    ya = (oa * lax.rsqrt(_segsum(oa * oa, r["m64a"][...]) * (1.0 / DV_A) + 1e-5)
          * r["hgrn_norm_w"][...] * _silu(p[:, GATE_A[0]:GATE_A[1]]))
    ob = o[:, OUT_B:OUT_B + W_B]
    yb = (ob * lax.rsqrt(_segsum(ob * ob, r["m96"][...]) * (1.0 / DV_B) + 1e-5)
          * r["gla_norm_w"][...] * _silu(p[:, GATE_B[0]:GATE_B[1]]))
    oc = o[:, OUT_C:D_MODEL]
    m64c = r["m64c"][...]
    dc = oc - _segsum(oc, m64c) * (1.0 / N_C)
    ocn = dc * lax.rsqrt(_segsum(dc * dc, m64c) * (1.0 / N_C) + GN_EPS) * r["ln_w"][...] + r["ln_b"][...]
    bonus = _segsum(r["rc"][...] * r["kc"][...] * r["r_k"][...], m64c) * r["vc"][...]
    yc = (ocn + bonus) * _silu(p[:, GATE_C[0]:GATE_C[1]])
    y = jnp.concatenate([ya, yb, yc], axis=-1)
    out = r["x"][...].reshape(rows, D_MODEL) + jnp.dot(y.astype(BF16), r["w_out"][...], preferred_element_type=F32)
    if cfg.final:
        out = out * lax.rsqrt(jnp.mean(out * out, axis=-1, keepdims=True) + 1e-6) * r["final_norm_w"][...]
    r["xo"][...] = out.reshape(r["xo"].shape)


def _gla_gates(r, p, cfg):
    hl = r["hgrn_lb"][...]
    e = jnp.exp(hl - jnp.max(hl, axis=0, keepdims=True))
    sm = e / jnp.sum(e, axis=0, keepdims=True)
    cum = sm[0:1]
    for j in range(1, cfg.layer + 1):
        cum = cum + sm[j:j + 1]
    lb = cum - sm[0:1]
    za = p[:, ZA[0]:ZA[1]]
    ls = _logsig(za)
    la = jnp.log(jnp.maximum(lb, TINY))
    bb = jnp.log1p(-lb) + ls
    lae = jnp.maximum(la, bb) + jnp.log1p(jnp.exp(-jnp.abs(la - bb)))
    g_a = jnp.where(lb > 0.0, lae, ls)
    r["ka"][...] = (1.0 - lb) * jax.nn.sigmoid(-za)
    g_b = _logsig(_dot_hi(p[:, GKL[0]:GKL[1]], r["gk_w2"][...]) + r["gk_b"][...]) * (1.0 / GLA_NORM)
    return g_a, g_b


def _rwkv_gates(r, p, cfg):
    rows, tseq = cfg.rows, cfg.tseq
    t = pl.program_id(1)
    pc = jnp.concatenate([p[:, PC_MAIN[0]:PC_MAIN[1]], p[:, PC_TAIL[0]:PC_TAIL[1]]], axis=1)
    xs = r["xs"]
    if cfg.has_state:
        row = lax.broadcasted_iota(jnp.int32, (rows, 1), 0)
        prev = jnp.where((row & (tseq - 1)) == 0, r["shift_rows"][...], pltpu.roll(pc, 1, axis=0))
        r["pct"][...] = pc
    else:
        @pl.when(t == 0)
        def _init():
            r["pct"][...] = jnp.zeros(r["pct"].shape, F32)

        xs[...] = pltpu.roll(pc, 1, axis=0)
        for q in range(rows // tseq):
            xs[q * tseq:q * tseq + 1, :] = r["pct"][(q + 1) * SUBLANES - 1:(q + 1) * SUBLANES, :]
            last = slice((q + 1) * tseq - SUBLANES, (q + 1) * tseq)
            r["pct"][q * SUBLANES:(q + 1) * SUBLANES, 0:N_MAIN] = p[last, PC_MAIN[0]:PC_MAIN[1]]
            r["pct"][q * SUBLANES:(q + 1) * SUBLANES, N_MAIN:D_SHIFT] = p[last, PC_TAIL[0]:PC_TAIL[1]]
        prev = xs[...]
    xs[...] = pc + (prev - pc) * r["mu"][...]
    kc = xs[:, SH_K[0]:SH_K[1]]
    wlog = -_softplus(-(r["w0"][...] + _dot_hi(jnp.tanh(xs[:, SH_WL[0]:SH_WL[1]]), r["w2"][...]))) - 0.5
    lw = -jnp.exp(wlog)
    av = jax.nn.sigmoid(r["a0"][...] + _dot_hi(xs[:, SH_AL[0]:SH_AL[1]], r["a2"][...]))
    kk = kc * r["k_k"][...]
    kk = kk * lax.rsqrt(jnp.maximum(_segsum(kk * kk, r["m64c"][...]), 1e-24))
    r["rc"][...] = xs[:, SH_R[0]:SH_R[1]]
    r["vc"][...] = xs[:, SH_V[0]:SH_V[1]]
    r["kc"][...] = kc * (1.0 + (av - 1.0) * r["k_a"][...])
    return lw, kk, kk * av


def _block_diag_ones(width, seg):
    i = jnp.arange(width) // seg
    return (i[:, None] == i[None, :]).astype(BF16)


def _run_layer(cfg, nb, nt, x3, state, new_states, params):
    l, rows = cfg.layer, cfg.rows
    nseq = rows // cfg.tseq
    nseq_total = nb * nseq
    assert not cfg.has_state or nt == 1
    x_block = (1, rows, D_MODEL) if cfg.has_state else (nseq, cfg.tseq, D_MODEL)
    ins, specs = [], []

    def add(name, arr, spec):
        ins.append((name, arr))
        specs.append(spec)

    def layer_row(name, arr):
        w = arr.shape[-1]
        add(name, arr.reshape(DEPTH, 1, w), pl.BlockSpec((None, 1, w), lambda b, t: (l, 0, 0)))

    once = pl.Buffered(1)

    def layer_mat(name, arr):
        add(name, arr, pl.BlockSpec((None,) + arr.shape[1:], lambda b, t: (l, 0, 0), pipeline_mode=once))

    def whole(name, arr):
        add(name, arr, pl.BlockSpec(arr.shape, lambda b, t: (0,) * arr.ndim, pipeline_mode=once))

    add("x", x3, pl.BlockSpec(x_block, lambda b, t: (b, t, 0)))
    if cfg.has_state:
        st_a, st_b, st_c, shift_rows = state
        add("shift_rows", shift_rows, pl.BlockSpec((None, rows, D_SHIFT), lambda b, t: (l, b, 0)))
        add("sa_in", st_a, pl.BlockSpec((None, nseq, H_A, DK_A, DV_A), lambda b, t: (l, b, 0, 0, 0)))
        add("sb_in", st_b, pl.BlockSpec((None, nseq, H_B, DK_B, DV_B), lambda b, t: (l, b, 0, 0, 0)))
        add("sc_in", st_c, pl.BlockSpec((None, nseq, H_C, N_C, N_C), lambda b, t: (l, b, 0, 0, 0)))
    layer_row("norm_w", params["norm_w"])
    layer_mat("w_in", params["w_in"])
    whole("hgrn_lb", params["hgrn_lb"])
    layer_row("hgrn_norm_w", params["hgrn_norm_w"])
    layer_mat("gk_w2", params["gla_gk_w2"])
    layer_row("gk_b", params["gla_gk_b"])
    layer_row("gla_norm_w", params["gla_norm_w"])
    layer_row("mu", params["rwkv_mu"])
    layer_row("w0", params["rwkv_w0"])
    layer_mat("w2", params["rwkv_w2"])
    layer_row("a0", params["rwkv_a0"])
    layer_mat("a2", params["rwkv_a2"])
    layer_row("k_k", params["rwkv_k_k"])
    layer_row("k_a", params["rwkv_k_a"])
    layer_row("r_k", params["rwkv_r_k"])
    layer_row("ln_w", params["rwkv_ln_w"])
    layer_row("ln_b", params["rwkv_ln_b"])
    layer_mat("w_out", params["w_out"])
    if cfg.final:
        whole("final_norm_w", params["final_norm_w"].reshape(1, D_MODEL))
    whole("m64a", params["m64a"])
    whole("m96", params["m96"])
    whole("m64c", params["m64c"])

    pct_rows = rows if cfg.has_state else nseq * SUBLANES
    state_dims = ((H_A, DK_A, DV_A), (H_B, DK_B, DV_B), (H_C, N_C, N_C))
    aliases = {}
    for k, (name, arr, dims) in enumerate(zip(("sa_all", "sb_all", "sc_all"), new_states, state_dims)):
        assert arr.shape == (DEPTH, nseq_total) + dims
        aliases[len(ins)] = 1 + k
        add(name, arr, pl.BlockSpec(memory_space=pl.ANY))
    outs = [("xo", jax.ShapeDtypeStruct(x3.shape, F32), pl.BlockSpec(x_block, lambda b, t: (b, t, 0)))]
    outs += [(name, jax.ShapeDtypeStruct((DEPTH, nseq_total) + dims, F32),
              pl.BlockSpec((None, nseq) + dims, lambda b, t: (l, b, 0, 0, 0)))
             for name, dims in zip(("sa", "sb", "sc"), state_dims)]
    outs += [("pct", jax.ShapeDtypeStruct((nb * pct_rows, D_SHIFT), F32),
              pl.BlockSpec((pct_rows, D_SHIFT), lambda b, t: (b, 0)))]
    a_w, b_w, c = H_A * DK_A, H_B * DK_B, cfg.chunk
    nch = rows // c
    scratch = [("p", (rows, D_IN), F32), ("xs", (rows, D_SHIFT), F32), ("o", (rows, D_MODEL), F32),
               ("oi", (rows, D_MODEL), F32),
               ("qa", (rows, a_w), F32), ("ka", (rows, a_w), F32), ("qb", (rows, b_w), F32),
               ("rc", (rows, W_C), F32), ("vc", (rows, W_C), F32), ("kc", (rows, W_C), F32),
               ("u0", (rows, W_C), F32), ("u", (rows, W_C), F32),
               ("kta", (nch, a_w, c), BF16), ("da", (nch, a_w, LANES), F32),
               ("ktb", (nch, b_w, c), BF16), ("db", (nch, b_w, LANES), F32),
               ("bkt", (nch, W_C, 2 * c), BF16), ("dc", (nch, W_C, LANES), F32),
               ("khr", (nch, 2 * c, W_C), F32), ("rb", (H_C, rows, rows), BF16),
               ("sct", (nseq, H_C, N_C, N_C), F32)]
    names = tuple(n for n, _ in ins) + tuple(n for n, _, _ in outs) + tuple(n for n, _, _ in scratch)
    return pl.pallas_call(
        functools.partial(_layer_kernel, names, cfg),
        grid=(nb, nt),
        in_specs=specs,
        out_specs=[s for _, _, s in outs],
        out_shape=[s for _, s, _ in outs],
        scratch_shapes=[pltpu.VMEM(shape, dt) for _, shape, dt in scratch],
        input_output_aliases=aliases,
        compiler_params=pltpu.CompilerParams(
            dimension_semantics=("arbitrary", "arbitrary"),
            vmem_limit_bytes=VMEM_LIMIT_BYTES),
        name=f"layer{l}_{'sample' if cfg.has_state else 'prompt'}",
    )(*[a for _, a in ins])


_MODEL_SHIFT_ORDER = tuple(n for n in _MODEL_COLS if n in _SHIFT_ORDER)


def _to_kernel_cols(w):
    return jnp.concatenate([w[..., _MODEL_COLS[n][0]:_MODEL_COLS[n][1]] for n in _COL_ORDER], axis=-1)


def _shift_to_kernel(s):
    base = _MODEL_COLS[_MODEL_SHIFT_ORDER[0]][0]
    return jnp.concatenate([s[..., _MODEL_COLS[n][0] - base:_MODEL_COLS[n][1] - base] for n in _SHIFT_ORDER], axis=-1)


def _shift_to_model(s):
    return jnp.concatenate([s[..., _SHIFT[n][0]:_SHIFT[n][1]] for n in _MODEL_SHIFT_ORDER], axis=-1)


def _make_params(norm_w, w_in, hgrn_lb, hgrn_norm_w, gla_gk_w2, gla_gk_b, gla_norm_w, rwkv_mu, rwkv_w0, rwkv_w2,
                 rwkv_a0, rwkv_a2, rwkv_k_k, rwkv_k_a, rwkv_r_k, rwkv_ln_w, rwkv_ln_b, w_out, final_norm_w):
    return dict(
        norm_w=norm_w, w_in=_to_kernel_cols(w_in.astype(BF16)), hgrn_lb=hgrn_lb, hgrn_norm_w=hgrn_norm_w,
        gla_gk_w2=gla_gk_w2, gla_gk_b=gla_gk_b, gla_norm_w=gla_norm_w, rwkv_mu=_shift_to_kernel(rwkv_mu),
        rwkv_w0=rwkv_w0, rwkv_w2=rwkv_w2, rwkv_a0=rwkv_a0, rwkv_a2=rwkv_a2, rwkv_k_k=rwkv_k_k,
        rwkv_k_a=rwkv_k_a, rwkv_r_k=rwkv_r_k, rwkv_ln_w=rwkv_ln_w, rwkv_ln_b=rwkv_ln_b,
        w_out=w_out.astype(BF16), final_norm_w=final_norm_w,
        m64a=_block_diag_ones(W_A, DV_A), m96=_block_diag_ones(W_B, DV_B),
        m64c=_block_diag_ones(W_C, N_C),
    )


def _trunk(x, state, params, rows, tseq, chunk):
    bsz, tlen, _ = x.shape
    has_state = state is not None
    nseq = rows // tseq
    assert rows % tseq == 0 and tseq % chunk == 0 and bsz % nseq == 0 and tlen % tseq == 0
    nb, nt = bsz // nseq, tlen // tseq
    if has_state:
        assert tlen == tseq
        st_a, st_b, st_c, st_s = state
        shift_rows = jnp.pad(_shift_to_kernel(st_s)[:, :, None, :], ((0, 0), (0, 0), (0, tlen - 1), (0, 0)))
        state = (st_a, st_b, st_c, shift_rows.reshape(DEPTH, bsz * tlen, D_SHIFT))
        x3 = x.reshape(nb, rows, D_MODEL)
    else:
        x3 = x
    new_states = tuple(jnp.zeros((DEPTH, bsz) + dims, F32)
                       for dims in ((H_A, DK_A, DV_A), (H_B, DK_B, DV_B), (H_C, N_C, N_C)))
    new_s = []
    for l in range(DEPTH):
        cfg = Cfg(layer=l, rows=rows, tseq=tseq, chunk=chunk, has_state=has_state, final=(l == DEPTH - 1))
        x3, *new_states, pct = _run_layer(cfg, nb, nt, x3, state, tuple(new_states), params)
        if has_state:
            new_s.append(pct.reshape(bsz, tlen, D_SHIFT)[:, tlen - 1])
        else:
            new_s.append(pct.reshape(bsz, SUBLANES, D_SHIFT)[:, SUBLANES - 1])
    return (x3.reshape(bsz, tlen, D_MODEL), *new_states, _shift_to_model(jnp.stack(new_s)))


PROMPT_ROWS = 256
PROMPT_TSEQ = 32
PROMPT_CHUNK = 32
SAMPLE_ROWS = 64


def kernel(x_prompt, x_sample, state_hgrn, state_gla, state_rwkv, state_shift, norm_w, w_in, hgrn_lb, hgrn_norm_w, gla_gk_w2, gla_gk_b, gla_norm_w, rwkv_mu, rwkv_w0, rwkv_w2, rwkv_a0, rwkv_a2, rwkv_k_k, rwkv_k_a, rwkv_r_k, rwkv_ln_w, rwkv_ln_b, w_out, final_norm_w):
    params = _make_params(norm_w, w_in, hgrn_lb, hgrn_norm_w, gla_gk_w2, gla_gk_b, gla_norm_w, rwkv_mu, rwkv_w0,
                          rwkv_w2, rwkv_a0, rwkv_a2, rwkv_k_k, rwkv_k_a, rwkv_r_k, rwkv_ln_w, rwkv_ln_b, w_out,
                          final_norm_w)
    y_p, hgrn_p, gla_p, rwkv_p, shift_p = _trunk(x_prompt, None, params, PROMPT_ROWS, PROMPT_TSEQ, PROMPT_CHUNK)
    tok = x_sample.shape[1]
    y_s, hgrn_s, gla_s, rwkv_s, shift_s = _trunk(
        x_sample, (state_hgrn, state_gla, state_rwkv, state_shift), params, SAMPLE_ROWS, tok, tok)
    return (y_p, y_s, hgrn_p, gla_p, rwkv_p, shift_p, hgrn_s, gla_s, rwkv_s, shift_s)
```

```python
import collections
import functools

import jax
import jax.numpy as jnp
from jax import lax
from jax.experimental import pallas as pl
from jax.experimental.pallas import tpu as pltpu

F32 = jnp.float32
BF16 = jnp.bfloat16

D_MODEL = 1024
DEPTH = 4
HEAD = 64
W_A = D_MODEL // 4
H_A = W_A // HEAD
DK_A = HEAD
DV_A = W_A // H_A
W_B = (D_MODEL - W_A) // 2
H_B = 4
DV_B = W_B // H_B
DK_B = DV_B // 2
GLA_LR = 16
GLA_NORM = 16.0
W_C = D_MODEL - W_A - W_B
H_C = W_C // HEAD
N_C = HEAD
DECAY_LR = 32
AAA_LR = 32
GN_EPS = 64e-5
TINY = 1e-30
D_A_IN = 2 * H_A * DK_A + 2 * W_A
D_B_IN = 2 * H_B * DK_B + 2 * W_B + GLA_LR
D_SHIFT = 3 * W_C + DECAY_LR + AAA_LR
D_IN = D_A_IN + D_B_IN + D_SHIFT + W_C


def _ranges(pieces):
    out, pos = {}, 0
    for name, width in pieces:
        out[name] = (pos, pos + width)
        pos += width
    return out


_MODEL_COLS = _ranges((
    ("qa", H_A * DK_A), ("za", H_A * DK_A), ("ia", W_A), ("gate_a", W_A),
    ("qb", H_B * DK_B), ("kb", H_B * DK_B), ("vb", W_B), ("gkl", GLA_LR), ("gate_b", W_B),
    ("r", W_C), ("wl", DECAY_LR), ("kc", W_C), ("vc", W_C), ("al", AAA_LR), ("gate_c", W_C)))
_COL_ORDER = ("qa", "za", "ia", "gate_a", "qb", "kb", "vb", "gate_b", "r", "kc", "vc", "gate_c",
              "gkl", "wl", "al")
_COLS = _ranges(tuple((n, _MODEL_COLS[n][1] - _MODEL_COLS[n][0]) for n in _COL_ORDER))
QA, ZA, IA, GATE_A = _COLS["qa"], _COLS["za"], _COLS["ia"], _COLS["gate_a"]
QB, KB, VB, GATE_B, GKL = _COLS["qb"], _COLS["kb"], _COLS["vb"], _COLS["gate_b"], _COLS["gkl"]
GATE_C = _COLS["gate_c"]
PC_MAIN = (_COLS["r"][0], _COLS["vc"][1])
PC_TAIL = (_COLS["wl"][0], _COLS["al"][1])
_SHIFT_ORDER = ("r", "kc", "vc", "wl", "al")
_SHIFT = _ranges(tuple((n, _MODEL_COLS[n][1] - _MODEL_COLS[n][0]) for n in _SHIFT_ORDER))
SH_R, SH_K, SH_V, SH_WL, SH_AL = (_SHIFT[n] for n in _SHIFT_ORDER)
N_MAIN = PC_MAIN[1] - PC_MAIN[0]
OUT_A = 0
OUT_B = W_A
OUT_C = W_A + W_B

SUBLANES = 8
LANES = 128
VMEM_LIMIT_BYTES = 56 * 1024 * 1024

Cfg = collections.namedtuple("Cfg", "layer rows tseq chunk has_state final")

NN = (((1,), (0,)), ((), ()))
NT = (((1,), (1,)), ((), ()))


def _dot(a, b):
    return jnp.dot(a.astype(BF16), b.astype(BF16), preferred_element_type=F32)


def _split(a):
    hi = a.astype(BF16)
    return hi, (a - hi.astype(F32)).astype(BF16)


def _dot3s(a_split, b_split, dims):
    (ah, al), (bh, bl) = a_split, b_split
    return (lax.dot_general(ah, bh, dims, preferred_element_type=F32)
            + lax.dot_general(ah, bl, dims, preferred_element_type=F32)
            + lax.dot_general(al, bh, dims, preferred_element_type=F32))


def _dot_nt(a, b):
    return lax.dot_general(a.astype(BF16), b.astype(BF16), NT, preferred_element_type=F32)


def _dot_nt3(a, b):
    return _dot3s(_split(a), _split(b), NT)


def _dot_hi(a, b):
    return _dot3s(_split(a), _split(b), NN)


def _segsum(x, m):
    return jnp.dot(x.astype(BF16), m, preferred_element_type=F32)


def _log1pexp(x):
    return jnp.log(1.0 + jnp.exp(-jnp.abs(x)))


def _softplus(x):
    return jnp.maximum(x, 0.0) + _log1pexp(x)


def _logsig(x):
    return jnp.minimum(x, 0.0) - _log1pexp(x)


def _silu(x):
    return x * jax.nn.sigmoid(x)


def _chunk_scan(x, chunk, pos, reverse=False):
    n = x.shape[0]
    s = 1
    while s < chunk:
        if reverse:
            x = x + jnp.where(pos < chunk - s, pltpu.roll(x, n - s, axis=0), 0.0)
        else:
            x = x + jnp.where(pos >= s, pltpu.roll(x, s, axis=0), 0.0)
        s *= 2
    return x


def _chunk_row(x, m, chunk, pos):
    rows, w = x.shape
    if chunk % SUBLANES == 0:
        x3 = x.reshape(rows // chunk, chunk, w)
        return jnp.broadcast_to(x3[:, m:m + 1, :], x3.shape).reshape(rows, w)
    z = jnp.where(pos == m, x, 0.0)
    return jnp.where(pos >= m, _chunk_scan(z, chunk, pos), _chunk_scan(z, chunk, pos, reverse=True))


def _store_chunk_columns(kt_ref, lane0, x, c):
    tile = max(c, SUBLANES)
    for t0 in range(0, x.shape[0], tile):
        xt = x[t0:t0 + tile, :].T.astype(BF16)
        for j in range(tile // c):
            kt_ref[t0 // c + j, :, lane0:lane0 + c] = xt[:, j * c:(j + 1) * c]


def _store_chunk_decay(d_ref, decay, c):
    for ch in range(decay.shape[0] // c):
        last = (ch + 1) * c - 1
        t0 = last // SUBLANES * SUBLANES
        col = decay[t0:t0 + SUBLANES, :].T[:, last - t0:last - t0 + 1]
        d_ref[ch] = jnp.broadcast_to(col, d_ref.shape[1:])


def _gla_block_prep(r, q, k, v, g, pos, masks, names, heads, dk, dv, out_base, c):
    qx_name, kt_name, d_name = names
    causal = masks[0]
    cum = _chunk_scan(g, c, pos)
    rel = cum - _chunk_row(cum, c // 2 - 1, c, pos)
    q_rel = q * jnp.exp(rel)
    k_rel = k * jnp.exp(-rel)
    eg = jnp.exp(cum)
    r[qx_name][...] = q * eg
    k_end = k * jnp.exp(_chunk_row(cum, c - 1, c, pos) - cum)
    _store_chunk_columns(r[kt_name], 0, k_end, c)
    _store_chunk_decay(r[d_name], eg, c)
    sub = causal.shape[0]
    for h in range(heads):
        ks = slice(h * dk, (h + 1) * dk)
        for r0 in range(0, q.shape[0], sub):
            rs = slice(r0, r0 + sub)
            sc = jnp.where(causal, _dot_nt(q_rel[rs, ks], k_rel[rs, ks]), 0.0)
            r["o"][rs, out_base + h * dv:out_base + (h + 1) * dv] = _dot(sc, v[rs, h * dv:(h + 1) * dv])


def _unit_lower_inverses(a_list, eye, same, c):
    if c % SUBLANES:
        w, p = [eye - a for a in a_list], a_list
        n = 2
        while n < c:
            p = [_dot3s(_split(x), _split(x), NN) for x in p]
            w = [x + _dot3s(_split(x), _split(y), NN) for x, y in zip(w, p)]
            n *= 2
        return w
    nblk = eye.shape[0] // c

    def side_by_side(x):
        out = x[0:c]
        for k in range(1, nblk):
            out = out + x[k * c:(k + 1) * c]
        return out

    def block_diag(x):
        return jnp.where(same, jnp.tile(x, (nblk, 1)), 0.0)

    def block_diag_split(x):
        hi = x.astype(BF16).astype(F32)
        return block_diag(hi).astype(BF16), block_diag(x - hi).astype(BF16)

    eye_w = side_by_side(eye)
    pw = [side_by_side(a) for a in a_list]
    w = [eye_w - x for x in pw]
    pw = [_dot3s(_split(x), _split(a), NN) for x, a in zip(pw, a_list)]
    n = 2
    while n < c:
        p_bd = [block_diag_split(x) for x in pw]
        n *= 2
        if n < c:
            prod = [_dot3s(_split(jnp.concatenate([x, y], axis=0)), s, NN) for x, y, s in zip(pw, w, p_bd)]
            pw = [z[0:c] for z in prod]
            w = [y + z[c:2 * c] for y, z in zip(w, prod)]
        else:
            w = [y + _dot3s(_split(y), s, NN) for y, s in zip(w, p_bd)]
    return [block_diag(y) for y in w]


def _rwkv_block_prep(r, lw, kk, b, pos, masks, c):
    causal, strict, eye, same = masks
    nch = lw.shape[0] // c
    rr, kmod, v = r["rc"][...], r["kc"][...], r["vc"][...]
    lc = _chunk_scan(lw, c, pos)
    el = jnp.exp(lc)
    inv_cum = jnp.exp(-lc)
    kk_in = kk * jnp.exp(lc - lw)
    b_out = b * inv_cum
    k_out = kmod * inv_cum
    r_in = rr * el
    to_end = jnp.exp(_chunk_row(lc, c - 1, c, pos) - lc)
    _store_chunk_columns(r["bkt"], 0, b * to_end, c)
    _store_chunk_columns(r["bkt"], c, kmod * to_end, c)
    _store_chunk_decay(r["dc"], el, c)
    for ch in range(nch):
        r["khr"][ch, c:2 * c, :] = r_in[ch * c:(ch + 1) * c, :]
    sub = eye.shape[0]
    parts = [(h, slice(r0, r0 + sub), slice(h * N_C, (h + 1) * N_C))
             for h in range(H_C) for r0 in range(0, lw.shape[0], sub)]
    a = [jnp.where(strict, _dot_nt3(kk_in[rs, cs], b_out[rs, cs]), 0.0) for _, rs, cs in parts]
    w_s = [_split(x) for x in _unit_lower_inverses(a, eye, same, c)]
    bv = [_dot(jnp.where(strict, _dot_nt3(kk_in[rs, cs], k_out[rs, cs]), 0.0), v[rs, cs]) for _, rs, cs in parts]
    for k, (h, rs, cs) in enumerate(parts):
        z = _dot3s(w_s[k], _split(jnp.concatenate([kk_in[rs, cs], bv[k]], axis=1)), NN)
        r["u0"][rs, cs] = z[:, N_C:2 * N_C]
        for j in range(sub // c):
            r["khr"][rs.start // c + j, 0:c, cs] = z[j * c:(j + 1) * c, 0:N_C]
    for h, rs, cs in parts:
        r["rb"][h, rs.start // sub] = jnp.where(causal, _dot_nt(r_in[rs, cs], b_out[rs, cs]), 0.0).astype(BF16)
        rk = jnp.where(causal, _dot_nt(r_in[rs, cs], k_out[rs, cs]), 0.0)
        r["o"][rs, OUT_C + h * N_C:OUT_C + (h + 1) * N_C] = _dot(rk, v[rs, cs])


def _recurrences(r, p, cfg):
    rows, c, tb = cfg.rows, cfg.chunk, cfg.tseq
    nseq = rows // tb
    steps = tb // c
    t = pl.program_id(1)
    pos = lax.broadcasted_iota(jnp.int32, (rows, 1), 0) & (c - 1)
    shift = c.bit_length() - 1
    sub = rows
    i = lax.broadcasted_iota(jnp.int32, (sub, sub), 0)
    j = lax.broadcasted_iota(jnp.int32, (sub, sub), 1)
    same = (i >> shift) == (j >> shift)
    masks = (same & (j <= i), same & (j < i), jnp.where(i == j, 1.0, 0.0), same)

    @pl.when(t == 0)
    def _init():
        if cfg.has_state:
            r["sa"][...] = r["sa_in"][...]
            r["sb"][...] = r["sb_in"][...]
            for q in range(nseq):
                for h in range(H_C):
                    r["sct"][q, h] = r["sc_in"][q, h].T
        else:
            r["sa"][...] = jnp.zeros(r["sa"].shape, F32)
            r["sb"][...] = jnp.zeros(r["sb"].shape, F32)
            r["sct"][...] = jnp.zeros(r["sct"].shape, F32)

    lw, kk, b = _rwkv_gates(r, p, cfg)
    _rwkv_block_prep(r, lw, kk, b, pos, masks, c)
    g_a, g_b = _gla_gates(r, p, cfg)
    _gla_block_prep(r, p[:, QA[0]:QA[1]] * (DK_A ** -0.5), r["ka"][...], p[:, IA[0]:IA[1]], g_a, pos, masks,
                    ("qa", "kta", "da"), H_A, DK_A, DV_A, OUT_A, c)
    _gla_block_prep(r, p[:, QB[0]:QB[1]] * (DK_B ** -0.5), p[:, KB[0]:KB[1]], p[:, VB[0]:VB[1]], g_b, pos, masks,
                    ("qb", "ktb", "db"), H_B, DK_B, DV_B, OUT_B, c)

    def body(n, carry):
        for q in range(nseq):
            ch = q * steps + n
            row0 = q * tb + n * c
            rs = slice(row0, row0 + c) if isinstance(n, int) else pl.ds(pl.multiple_of(row0, c), c)
            for (qn, kn, dn, st, v, base, heads, dk, dv) in (
                    ("qa", "kta", "da", r["sa"], p[rs, IA[0]:IA[1]], OUT_A, H_A, DK_A, DV_A),
                    ("qb", "ktb", "db", r["sb"], p[rs, VB[0]:VB[1]], OUT_B, H_B, DK_B, DV_B)):
                qx, kt, d = r[qn][rs, :], r[kn][ch], r[dn][ch]
                for h in range(heads):
                    ks = slice(h * dk, (h + 1) * dk)
                    vs = slice(h * dv, (h + 1) * dv)
                    s = st[q, h]
                    r["oi"][rs, base + h * dv:base + (h + 1) * dv] = _dot(qx[:, ks], s)
                    st[q, h] = s * d[ks, 0:dv] + jnp.dot(kt[ks, :], v[:, vs].astype(BF16),
                                                         preferred_element_type=F32)
            khr, bkt, d = r["khr"][ch], r["bkt"][ch], r["dc"][ch]
            u0, v = r["u0"][rs, :], r["vc"][rs, :]
            for h in range(H_C):
                cs = slice(h * N_C, (h + 1) * N_C)
                s = r["sct"][q, h]
                x = _dot(khr[:, cs], s)
                u = -(x[0:c] + u0[:, cs])
                r["u"][rs, cs] = u
                r["oi"][rs, OUT_C + h * N_C:OUT_C + (h + 1) * N_C] = x[c:2 * c]
                r["sct"][q, h] = s * d[cs, 0:N_C] + jnp.dot(
                    bkt[cs, :], jnp.concatenate([u, v[:, cs]], axis=0).astype(BF16),
                    preferred_element_type=F32)
        return carry

    if steps == 1:
        body(0, 0)
    else:
        lax.fori_loop(0, steps, body, 0)

    r["o"][...] += r["oi"][...]
    for h in range(H_C):
        cs = slice(h * N_C, (h + 1) * N_C)
        for k in range(rows // sub):
            rs = slice(k * sub, (k + 1) * sub)
            r["o"][rs, OUT_C + h * N_C:OUT_C + (h + 1) * N_C] += jnp.dot(
                r["rb"][h, k], r["u"][rs, cs].astype(BF16), preferred_element_type=F32)

    @pl.when(t == pl.num_programs(1) - 1)
    def _emit():
        for q in range(nseq):
            for h in range(H_C):
                r["sc"][q, h] = r["sct"][q, h].T


def _layer_kernel(names, cfg, *refs):
    r = dict(zip(names, refs))
    rows = cfg.rows
    p = r["p"]

    x = r["x"][...].reshape(rows, D_MODEL)
    hn = x * lax.rsqrt(jnp.mean(x * x, axis=-1, keepdims=True) + 1e-6) * r["norm_w"][...]
    p[...] = jnp.dot(hn.astype(BF16), r["w_in"][...], preferred_element_type=F32)

    _recurrences(r, p, cfg)

    o = r["o"]
    oa = o[:, OUT_A:OUT_A + W_A]
    ya = (oa * lax.rsqrt(_segsum(oa * oa, r["m64a"][...]) * (1.0 / DV_A) + 1e-5)
          * r["hgrn_norm_w"][...] * _silu(p[:, GATE_A[0]:GATE_A[1]]))
    ob = o[:, OUT_B:OUT_B + W_B]
    yb = (ob * lax.rsqrt(_segsum(ob * ob, r["m96"][...]) * (1.0 / DV_B) + 1e-5)
          * r["gla_norm_w"][...] * _silu(p[:, GATE_B[0]:GATE_B[1]]))
    oc = o[:, OUT_C:D_MODEL]
    m64c = r["m64c"][...]
    dc = oc - _segsum(oc, m64c) * (1.0 / N_C)
    ocn = dc * lax.rsqrt(_segsum(dc * dc, m64c) * (1.0 / N_C) + GN_EPS) * r["ln_w"][...] + r["ln_b"][...]
    bonus = _segsum(r["rc"][...] * r["kc"][...] * r["r_k"][...], m64c) * r["vc"][...]
    yc = (ocn + bonus) * _silu(p[:, GATE_C[0]:GATE_C[1]])
    y = jnp.concatenate([ya, yb, yc], axis=-1)
    out = r["x"][...].reshape(rows, D_MODEL) + jnp.dot(y.astype(BF16), r["w_out"][...], preferred_element_type=F32)
    if cfg.final:
        out = out * lax.rsqrt(jnp.mean(out * out, axis=-1, keepdims=True) + 1e-6) * r["final_norm_w"][...]
    r["xo"][...] = out.reshape(r["xo"].shape)


def _gla_gates(r, p, cfg):
    hl = r["hgrn_lb"][...]
    e = jnp.exp(hl - jnp.max(hl, axis=0, keepdims=True))
    sm = e / jnp.sum(e, axis=0, keepdims=True)
    cum = sm[0:1]
    for j in range(1, cfg.layer + 1):
        cum = cum + sm[j:j + 1]
    lb = cum - sm[0:1]
    za = p[:, ZA[0]:ZA[1]]
    ls = _logsig(za)
    la = jnp.log(jnp.maximum(lb, TINY))
    bb = jnp.log1p(-lb) + ls
    lae = jnp.maximum(la, bb) + _log1pexp(la - bb)
    g_a = jnp.where(lb > 0.0, lae, ls)
    r["ka"][...] = (1.0 - lb) * jax.nn.sigmoid(-za)
    g_b = _logsig(_dot_hi(p[:, GKL[0]:GKL[1]], r["gk_w2"][...]) + r["gk_b"][...]) * (1.0 / GLA_NORM)
    return g_a, g_b


def _rwkv_gates(r, p, cfg):
    rows, tseq = cfg.rows, cfg.tseq
    t = pl.program_id(1)
    pc = jnp.concatenate([p[:, PC_MAIN[0]:PC_MAIN[1]], p[:, PC_TAIL[0]:PC_TAIL[1]]], axis=1)
    xs = r["xs"]
    if cfg.has_state:
        row = lax.broadcasted_iota(jnp.int32, (rows, 1), 0)
        prev = jnp.where((row & (tseq - 1)) == 0, r["shift_rows"][...], pltpu.roll(pc, 1, axis=0))
        r["pct"][...] = pc
    else:
        @pl.when(t == 0)
        def _init():
            r["pct"][...] = jnp.zeros(r["pct"].shape, F32)

        xs[...] = pltpu.roll(pc, 1, axis=0)
        for q in range(rows // tseq):
            xs[q * tseq:q * tseq + 1, :] = r["pct"][(q + 1) * SUBLANES - 1:(q + 1) * SUBLANES, :]
            last = slice((q + 1) * tseq - SUBLANES, (q + 1) * tseq)
            r["pct"][q * SUBLANES:(q + 1) * SUBLANES, 0:N_MAIN] = p[last, PC_MAIN[0]:PC_MAIN[1]]
            r["pct"][q * SUBLANES:(q + 1) * SUBLANES, N_MAIN:D_SHIFT] = p[last, PC_TAIL[0]:PC_TAIL[1]]
        prev = xs[...]
    xs[...] = pc + (prev - pc) * r["mu"][...]
    kc = xs[:, SH_K[0]:SH_K[1]]
    wlog = -_softplus(-(r["w0"][...] + _dot_hi(jnp.tanh(xs[:, SH_WL[0]:SH_WL[1]]), r["w2"][...]))) - 0.5
    lw = -jnp.exp(wlog)
    av = jax.nn.sigmoid(r["a0"][...] + _dot_hi(xs[:, SH_AL[0]:SH_AL[1]], r["a2"][...]))
    kk = kc * r["k_k"][...]
    kk = kk * lax.rsqrt(jnp.maximum(_segsum(kk * kk, r["m64c"][...]), 1e-24))
    r["rc"][...] = xs[:, SH_R[0]:SH_R[1]]
    r["vc"][...] = xs[:, SH_V[0]:SH_V[1]]
    r["kc"][...] = kc * (1.0 + (av - 1.0) * r["k_a"][...])
    return lw, kk, kk * av


def _block_diag_ones(width, seg):
    i = jnp.arange(width) // seg
    return (i[:, None] == i[None, :]).astype(BF16)


def _run_layer(cfg, nb, nt, x3, state, new_states, params):
    l, rows = cfg.layer, cfg.rows
    nseq = rows // cfg.tseq
    nseq_total = nb * nseq
    assert not cfg.has_state or nt == 1
    x_block = (1, rows, D_MODEL) if cfg.has_state else (nseq, cfg.tseq, D_MODEL)
    ins, specs = [], []

    def add(name, arr, spec):
        ins.append((name, arr))
        specs.append(spec)

    def layer_row(name, arr):
        w = arr.shape[-1]
        add(name, arr.reshape(DEPTH, 1, w), pl.BlockSpec((None, 1, w), lambda b, t: (l, 0, 0)))

    once = pl.Buffered(1)

    def layer_mat(name, arr):
        add(name, arr, pl.BlockSpec((None,) + arr.shape[1:], lambda b, t: (l, 0, 0), pipeline_mode=once))

    def whole(name, arr):
        add(name, arr, pl.BlockSpec(arr.shape, lambda b, t: (0,) * arr.ndim, pipeline_mode=once))

    add("x", x3, pl.BlockSpec(x_block, lambda b, t: (b, t, 0)))
    if cfg.has_state:
        st_a, st_b, st_c, shift_rows = state
        add("shift_rows", shift_rows, pl.BlockSpec((None, rows, D_SHIFT), lambda b, t: (l, b, 0)))
        add("sa_in", st_a, pl.BlockSpec((None, nseq, H_A, DK_A, DV_A), lambda b, t: (l, b, 0, 0, 0)))
        add("sb_in", st_b, pl.BlockSpec((None, nseq, H_B, DK_B, DV_B), lambda b, t: (l, b, 0, 0, 0)))
        add("sc_in", st_c, pl.BlockSpec((None, nseq, H_C, N_C, N_C), lambda b, t: (l, b, 0, 0, 0)))
    layer_row("norm_w", params["norm_w"])
    layer_mat("w_in", params["w_in"])
    whole("hgrn_lb", params["hgrn_lb"])
    layer_row("hgrn_norm_w", params["hgrn_norm_w"])
    layer_mat("gk_w2", params["gla_gk_w2"])
    layer_row("gk_b", params["gla_gk_b"])
    layer_row("gla_norm_w", params["gla_norm_w"])
    layer_row("mu", params["rwkv_mu"])
    layer_row("w0", params["rwkv_w0"])
    layer_mat("w2", params["rwkv_w2"])
    layer_row("a0", params["rwkv_a0"])
    layer_mat("a2", params["rwkv_a2"])
    layer_row("k_k", params["rwkv_k_k"])
    layer_row("k_a", params["rwkv_k_a"])
    layer_row("r_k", params["rwkv_r_k"])
    layer_row("ln_w", params["rwkv_ln_w"])
    layer_row("ln_b", params["rwkv_ln_b"])
    layer_mat("w_out", params["w_out"])
    if cfg.final:
        whole("final_norm_w", params["final_norm_w"].reshape(1, D_MODEL))
    whole("m64a", params["m64a"])
    whole("m96", params["m96"])
    whole("m64c", params["m64c"])

    pct_rows = rows if cfg.has_state else nseq * SUBLANES
    state_dims = ((H_A, DK_A, DV_A), (H_B, DK_B, DV_B), (H_C, N_C, N_C))
    aliases = {}
    for k, (name, arr, dims) in enumerate(zip(("sa_all", "sb_all", "sc_all"), new_states, state_dims)):
        assert arr.shape == (DEPTH, nseq_total) + dims
        aliases[len(ins)] = 1 + k
        add(name, arr, pl.BlockSpec(memory_space=pl.ANY))
    outs = [("xo", jax.ShapeDtypeStruct(x3.shape, F32), pl.BlockSpec(x_block, lambda b, t: (b, t, 0)))]
    outs += [(name, jax.ShapeDtypeStruct((DEPTH, nseq_total) + dims, F32),
              pl.BlockSpec((None, nseq) + dims, lambda b, t: (l, b, 0, 0, 0)))
             for name, dims in zip(("sa", "sb", "sc"), state_dims)]
    outs += [("pct", jax.ShapeDtypeStruct((nb * pct_rows, D_SHIFT), F32),
              pl.BlockSpec((pct_rows, D_SHIFT), lambda b, t: (b, 0)))]
    a_w, b_w, c = H_A * DK_A, H_B * DK_B, cfg.chunk
    nch = rows // c
    scratch = [("p", (rows, D_IN), F32), ("xs", (rows, D_SHIFT), F32), ("o", (rows, D_MODEL), F32),
               ("oi", (rows, D_MODEL), F32),
               ("qa", (rows, a_w), F32), ("ka", (rows, a_w), F32), ("qb", (rows, b_w), F32),
               ("rc", (rows, W_C), F32), ("vc", (rows, W_C), F32), ("kc", (rows, W_C), F32),
               ("u0", (rows, W_C), F32), ("u", (rows, W_C), F32),
               ("kta", (nch, a_w, c), BF16), ("da", (nch, a_w, LANES), F32),
               ("ktb", (nch, b_w, c), BF16), ("db", (nch, b_w, LANES), F32),
               ("bkt", (nch, W_C, 2 * c), BF16), ("dc", (nch, W_C, LANES), F32),
               ("khr", (nch, 2 * c, W_C), F32), ("rb", (H_C, 1, rows, rows), BF16),
               ("sct", (nseq, H_C, N_C, N_C), F32)]
    names = tuple(n for n, _ in ins) + tuple(n for n, _, _ in outs) + tuple(n for n, _, _ in scratch)
    return pl.pallas_call(
        functools.partial(_layer_kernel, names, cfg),
        grid=(nb, nt),
        in_specs=specs,
        out_specs=[s for _, _, s in outs],
        out_shape=[s for _, s, _ in outs],
        scratch_shapes=[pltpu.VMEM(shape, dt) for _, shape, dt in scratch],
        input_output_aliases=aliases,
        compiler_params=pltpu.CompilerParams(
            dimension_semantics=("arbitrary", "arbitrary"),
            vmem_limit_bytes=VMEM_LIMIT_BYTES),
        name=f"layer{l}_{'sample' if cfg.has_state else 'prompt'}",
    )(*[a for _, a in ins])


_MODEL_SHIFT_ORDER = tuple(n for n in _MODEL_COLS if n in _SHIFT_ORDER)


def _to_kernel_cols(w):
    return jnp.concatenate([w[..., _MODEL_COLS[n][0]:_MODEL_COLS[n][1]] for n in _COL_ORDER], axis=-1)


def _shift_to_kernel(s):
    base = _MODEL_COLS[_MODEL_SHIFT_ORDER[0]][0]
    return jnp.concatenate([s[..., _MODEL_COLS[n][0] - base:_MODEL_COLS[n][1] - base] for n in _SHIFT_ORDER], axis=-1)


def _shift_to_model(s):
    return jnp.concatenate([s[..., _SHIFT[n][0]:_SHIFT[n][1]] for n in _MODEL_SHIFT_ORDER], axis=-1)


def _make_params(norm_w, w_in, hgrn_lb, hgrn_norm_w, gla_gk_w2, gla_gk_b, gla_norm_w, rwkv_mu, rwkv_w0, rwkv_w2,
                 rwkv_a0, rwkv_a2, rwkv_k_k, rwkv_k_a, rwkv_r_k, rwkv_ln_w, rwkv_ln_b, w_out, final_norm_w):
    return dict(
        norm_w=norm_w, w_in=_to_kernel_cols(w_in).astype(BF16), hgrn_lb=hgrn_lb, hgrn_norm_w=hgrn_norm_w,
        gla_gk_w2=gla_gk_w2, gla_gk_b=gla_gk_b, gla_norm_w=gla_norm_w, rwkv_mu=_shift_to_kernel(rwkv_mu),
        rwkv_w0=rwkv_w0, rwkv_w2=rwkv_w2, rwkv_a0=rwkv_a0, rwkv_a2=rwkv_a2, rwkv_k_k=rwkv_k_k,
        rwkv_k_a=rwkv_k_a, rwkv_r_k=rwkv_r_k, rwkv_ln_w=rwkv_ln_w, rwkv_ln_b=rwkv_ln_b,
        w_out=w_out.astype(BF16), final_norm_w=final_norm_w,
        m64a=_block_diag_ones(W_A, DV_A), m96=_block_diag_ones(W_B, DV_B),
        m64c=_block_diag_ones(W_C, N_C),
    )


def _trunk(x, state, params, rows, tseq, chunk):
    bsz, tlen, _ = x.shape
    has_state = state is not None
    nseq = rows // tseq
    assert rows % tseq == 0 and tseq % chunk == 0 and bsz % nseq == 0 and tlen % tseq == 0
    nb, nt = bsz // nseq, tlen // tseq
    if has_state:
        assert tlen == tseq
        st_a, st_b, st_c, st_s = state
        shift_rows = jnp.pad(_shift_to_kernel(st_s)[:, :, None, :], ((0, 0), (0, 0), (0, tlen - 1), (0, 0)))
        state = (st_a, st_b, st_c, shift_rows.reshape(DEPTH, bsz * tlen, D_SHIFT))
        x3 = x.reshape(nb, rows, D_MODEL)
    else:
        x3 = x
    new_states = tuple(jnp.zeros((DEPTH, bsz) + dims, F32)
                       for dims in ((H_A, DK_A, DV_A), (H_B, DK_B, DV_B), (H_C, N_C, N_C)))
    new_s = []
    for l in range(DEPTH):
        cfg = Cfg(layer=l, rows=rows, tseq=tseq, chunk=chunk, has_state=has_state, final=(l == DEPTH - 1))
        x3, *new_states, pct = _run_layer(cfg, nb, nt, x3, state, tuple(new_states), params)
        if has_state:
            new_s.append(pct.reshape(bsz, tlen, D_SHIFT)[:, tlen - 1])
        else:
            new_s.append(pct.reshape(bsz, SUBLANES, D_SHIFT)[:, SUBLANES - 1])
    return (x3.reshape(bsz, tlen, D_MODEL), *new_states, _shift_to_model(jnp.stack(new_s)))


PROMPT_ROWS = 256
PROMPT_TSEQ = 32
PROMPT_CHUNK = 32
SAMPLE_ROWS = 64


def kernel(x_prompt, x_sample, state_hgrn, state_gla, state_rwkv, state_shift, norm_w, w_in, hgrn_lb, hgrn_norm_w, gla_gk_w2, gla_gk_b, gla_norm_w, rwkv_mu, rwkv_w0, rwkv_w2, rwkv_a0, rwkv_a2, rwkv_k_k, rwkv_k_a, rwkv_r_k, rwkv_ln_w, rwkv_ln_b, w_out, final_norm_w):
    params = _make_params(norm_w, w_in, hgrn_lb, hgrn_norm_w, gla_gk_w2, gla_gk_b, gla_norm_w, rwkv_mu, rwkv_w0,
                          rwkv_w2, rwkv_a0, rwkv_a2, rwkv_k_k, rwkv_k_a, rwkv_r_k, rwkv_ln_w, rwkv_ln_b, w_out,
                          final_norm_w)
    y_p, hgrn_p, gla_p, rwkv_p, shift_p = _trunk(x_prompt, None, params, PROMPT_ROWS, PROMPT_TSEQ, PROMPT_CHUNK)
    tok = x_sample.shape[1]
    y_s, hgrn_s, gla_s, rwkv_s, shift_s = _trunk(
        x_sample, (state_hgrn, state_gla, state_rwkv, state_shift), params, SAMPLE_ROWS, tok, tok)
    return (y_p, y_s, hgrn_p, gla_p, rwkv_p, shift_p, hgrn_s, gla_s, rwkv_s, shift_s)
```

```python
import collections
import functools

import jax
import jax.numpy as jnp
from jax import lax
from jax.experimental import pallas as pl
from jax.experimental.pallas import tpu as pltpu

F32 = jnp.float32
BF16 = jnp.bfloat16

D_MODEL = 1024
DEPTH = 4
HEAD = 64
W_A = D_MODEL // 4
H_A = W_A // HEAD
DK_A = HEAD
DV_A = W_A // H_A
W_B = (D_MODEL - W_A) // 2
H_B = 4
DV_B = W_B // H_B
DK_B = DV_B // 2
GLA_LR = 16
GLA_NORM = 16.0
W_C = D_MODEL - W_A - W_B
H_C = W_C // HEAD
N_C = HEAD
DECAY_LR = 32
AAA_LR = 32
GN_EPS = 64e-5
TINY = 1e-30
D_A_IN = 2 * H_A * DK_A + 2 * W_A
D_B_IN = 2 * H_B * DK_B + 2 * W_B + GLA_LR
D_SHIFT = 3 * W_C + DECAY_LR + AAA_LR
D_IN = D_A_IN + D_B_IN + D_SHIFT + W_C


def _ranges(pieces):
    out, pos = {}, 0
    for name, width in pieces:
        out[name] = (pos, pos + width)
        pos += width
    return out


_MODEL_COLS = _ranges((
    ("qa", H_A * DK_A), ("za", H_A * DK_A), ("ia", W_A), ("gate_a", W_A),
    ("qb", H_B * DK_B), ("kb", H_B * DK_B), ("vb", W_B), ("gkl", GLA_LR), ("gate_b", W_B),
    ("r", W_C), ("wl", DECAY_LR), ("kc", W_C), ("vc", W_C), ("al", AAA_LR), ("gate_c", W_C)))
_COL_ORDER = ("qa", "za", "ia", "gate_a", "qb", "kb", "vb", "gate_b", "r", "kc", "vc", "gate_c",
              "gkl", "wl", "al")
_COLS = _ranges(tuple((n, _MODEL_COLS[n][1] - _MODEL_COLS[n][0]) for n in _COL_ORDER))
QA, ZA, IA, GATE_A = _COLS["qa"], _COLS["za"], _COLS["ia"], _COLS["gate_a"]
QB, KB, VB, GATE_B, GKL = _COLS["qb"], _COLS["kb"], _COLS["vb"], _COLS["gate_b"], _COLS["gkl"]
GATE_C = _COLS["gate_c"]
PC_MAIN = (_COLS["r"][0], _COLS["vc"][1])
PC_TAIL = (_COLS["wl"][0], _COLS["al"][1])
_SHIFT_ORDER = ("r", "kc", "vc", "wl", "al")
_SHIFT = _ranges(tuple((n, _MODEL_COLS[n][1] - _MODEL_COLS[n][0]) for n in _SHIFT_ORDER))
SH_R, SH_K, SH_V, SH_WL, SH_AL = (_SHIFT[n] for n in _SHIFT_ORDER)
N_MAIN = PC_MAIN[1] - PC_MAIN[0]
OUT_A = 0
OUT_B = W_A
OUT_C = W_A + W_B

SUBLANES = 8
LANES = 128
VMEM_LIMIT_BYTES = 58 * 1024 * 1024

Cfg = collections.namedtuple("Cfg", "layer rows tseq chunk has_state final")

NN = (((1,), (0,)), ((), ()))
NT = (((1,), (1,)), ((), ()))


def _dot(a, b):
    return jnp.dot(a.astype(BF16), b.astype(BF16), preferred_element_type=F32)


def _split(a):
    hi = a.astype(BF16)
    return hi, (a - hi.astype(F32)).astype(BF16)


def _dot3s(a_split, b_split, dims):
    (ah, al), (bh, bl) = a_split, b_split
    return (lax.dot_general(ah, bh, dims, preferred_element_type=F32)
            + lax.dot_general(ah, bl, dims, preferred_element_type=F32)
            + lax.dot_general(al, bh, dims, preferred_element_type=F32))


def _dot_nt(a, b):
    return lax.dot_general(a.astype(BF16), b.astype(BF16), NT, preferred_element_type=F32)


def _dot_nt3(a, b):
    return _dot3s(_split(a), _split(b), NT)


def _dot_hi(a, b):
    return _dot3s(_split(a), _split(b), NN)


def _segsum(x, m):
    return jnp.dot(x.astype(BF16), m, preferred_element_type=F32)


def _log1pexp(x):
    return jnp.log(1.0 + jnp.exp(-jnp.abs(x)))


def _softplus(x):
    return jnp.maximum(x, 0.0) + _log1pexp(x)


def _logsig(x):
    return jnp.minimum(x, 0.0) - _log1pexp(x)


def _silu(x):
    return x * jax.nn.sigmoid(x)


def _chunk_scan(x, chunk, pos, reverse=False):
    n = x.shape[0]
    s = 1
    while s < chunk:
        if reverse:
            x = x + jnp.where(pos < chunk - s, pltpu.roll(x, n - s, axis=0), 0.0)
        else:
            x = x + jnp.where(pos >= s, pltpu.roll(x, s, axis=0), 0.0)
        s *= 2
    return x


def _chunk_row(x, m, chunk, pos):
    rows, w = x.shape
    if chunk % SUBLANES == 0:
        x3 = x.reshape(rows // chunk, chunk, w)
        return jnp.broadcast_to(x3[:, m:m + 1, :], x3.shape).reshape(rows, w)
    z = jnp.where(pos == m, x, 0.0)
    return jnp.where(pos >= m, _chunk_scan(z, chunk, pos), _chunk_scan(z, chunk, pos, reverse=True))


def _store_chunk_columns(kt_ref, lane0, x, c):
    tile = max(c, SUBLANES)
    for t0 in range(0, x.shape[0], tile):
        xt = x[t0:t0 + tile, :].T.astype(BF16)
        for j in range(tile // c):
            kt_ref[t0 // c + j, :, lane0:lane0 + c] = xt[:, j * c:(j + 1) * c]


def _store_chunk_decay(d_ref, decay, c):
    for ch in range(decay.shape[0] // c):
        last = (ch + 1) * c - 1
        t0 = last // SUBLANES * SUBLANES
        d_ref[:, ch:ch + 1] = decay[t0:t0 + SUBLANES, :].T[:, last - t0:last - t0 + 1]


def _gla_block_prep(r, q, k, v, g, pos, masks, names, heads, dk, dv, out_base, c):
    qx_name, kt_name, d_name = names
    causal = masks[0]
    cum = _chunk_scan(g, c, pos)
    rel = cum - _chunk_row(cum, c // 2 - 1, c, pos)
    q_rel = q * jnp.exp(rel)
    k_rel = k * jnp.exp(-rel)
    eg = jnp.exp(cum)
    r[qx_name][...] = q * eg
    k_end = k * jnp.exp(_chunk_row(cum, c - 1, c, pos) - cum)
    _store_chunk_columns(r[kt_name], 0, k_end, c)
    _store_chunk_decay(r[d_name], eg, c)
    sub = causal.shape[0]
    for h in range(heads):
        ks = slice(h * dk, (h + 1) * dk)
        for r0 in range(0, q.shape[0], sub):
            rs = slice(r0, r0 + sub)
            sc = jnp.where(causal, _dot_nt(q_rel[rs, ks], k_rel[rs, ks]), 0.0)
            r["o"][rs, out_base + h * dv:out_base + (h + 1) * dv] = _dot(sc, v[rs, h * dv:(h + 1) * dv])


def _unit_lower_inverses(a_list, eye, same, c):
    if c % SUBLANES:
        w, p = [eye - a for a in a_list], a_list
        n = 2
        while n < c:
            p = [_dot3s(_split(x), _split(x), NN) for x in p]
            w = [x + _dot3s(_split(x), _split(y), NN) for x, y in zip(w, p)]
            n *= 2
            yield
        return w
    nblk = eye.shape[0] // c

    def side_by_side(x):
        out = x[0:c]
        for k in range(1, nblk):
            out = out + x[k * c:(k + 1) * c]
        return out

    def block_diag(x):
        return jnp.where(same, jnp.tile(x, (nblk, 1)), 0.0)

    def block_diag_split(x):
        hi = x.astype(BF16).astype(F32)
        return block_diag(hi).astype(BF16), block_diag(x - hi).astype(BF16)

    eye_w = side_by_side(eye)
    pw = [side_by_side(a) for a in a_list]
    w = [eye_w - x for x in pw]
    pw = [_dot3s(_split(x), _split(a), NN) for x, a in zip(pw, a_list)]
    n = 2
    while n < c:
        yield
        p_bd = [block_diag_split(x) for x in pw]
        n *= 2
        if n < c:
            prod = [_dot3s(_split(jnp.concatenate([x, y], axis=0)), s, NN) for x, y, s in zip(pw, w, p_bd)]
            pw = [z[0:c] for z in prod]
            w = [y + z[c:2 * c] for y, z in zip(w, prod)]
        else:
            w = [y + _dot3s(_split(y), s, NN) for y, s in zip(w, p_bd)]
    return [block_diag(y) for y in w]


def _rwkv_block_prep(r, lw, kk, b, pos, masks, c):
    causal, strict, eye, same = masks
    nch = lw.shape[0] // c
    rr, kmod, v = r["rc"][...], r["kc"][...], r["vc"][...]
    lc = _chunk_scan(lw, c, pos)
    el = jnp.exp(lc)
    inv_cum = jnp.exp(-lc)
    kk_in = kk * jnp.exp(lc - lw)
    b_out = b * inv_cum
    k_out = kmod * inv_cum
    r_in = rr * el
    to_end = jnp.exp(_chunk_row(lc, c - 1, c, pos) - lc)
    _store_chunk_columns(r["bkt"], 0, b * to_end, c)
    _store_chunk_columns(r["bkt"], c, kmod * to_end, c)
    _store_chunk_decay(r["dc"], el, c)
    for ch in range(nch):
        r["khr"][ch, c:2 * c, :] = r_in[ch * c:(ch + 1) * c, :]
    sub = eye.shape[0]
    parts = [(h, slice(r0, r0 + sub), slice(h * N_C, (h + 1) * N_C))
             for h in range(H_C) for r0 in range(0, lw.shape[0], sub)]
    a = [jnp.where(strict, _dot_nt3(kk_in[rs, cs], b_out[rs, cs]), 0.0) for _, rs, cs in parts]
    yield
    w_s = [_split(x) for x in (yield from _unit_lower_inverses(a, eye, same, c))]
    yield
    bv = [_dot(jnp.where(strict, _dot_nt3(kk_in[rs, cs], k_out[rs, cs]), 0.0), v[rs, cs]) for _, rs, cs in parts]
    yield
    for k, (h, rs, cs) in enumerate(parts):
        z = _dot3s(w_s[k], _split(jnp.concatenate([kk_in[rs, cs], bv[k]], axis=1)), NN)
        r["u0"][rs, cs] = z[:, N_C:2 * N_C]
        for j in range(sub // c):
            r["khr"][rs.start // c + j, 0:c, cs] = z[j * c:(j + 1) * c, 0:N_C]
    yield
    for h, rs, cs in parts:
        r["rb"][h] = jnp.where(causal, _dot_nt(r_in[rs, cs], b_out[rs, cs]), 0.0).astype(BF16)
        rk = jnp.where(causal, _dot_nt(r_in[rs, cs], k_out[rs, cs]), 0.0)
        r["o"][rs, OUT_C + h * N_C:OUT_C + (h + 1) * N_C] = _dot(rk, v[rs, cs])


def _stage_a(r, cfg):
    rows, c = cfg.rows, cfg.chunk
    p = r["p"]
    x = r["x"][...].reshape(rows, D_MODEL)
    hn = x * lax.rsqrt(jnp.mean(x * x, axis=-1, keepdims=True) + 1e-6) * r["norm_w"][...]
    p[...] = jnp.dot(hn.astype(BF16), r["w_in"][...], preferred_element_type=F32)
    yield

    pos = lax.broadcasted_iota(jnp.int32, (rows, 1), 0) & (c - 1)
    shift = c.bit_length() - 1
    i = lax.broadcasted_iota(jnp.int32, (rows, rows), 0)
    j = lax.broadcasted_iota(jnp.int32, (rows, rows), 1)
    same = (i >> shift) == (j >> shift)
    masks = (same & (j <= i), same & (j < i), jnp.where(i == j, 1.0, 0.0), same)

    lw, kk, b = _rwkv_gates(r, p, cfg)
    yield
    yield from _rwkv_block_prep(r, lw, kk, b, pos, masks, c)
    yield
    g_a, g_b = _gla_gates(r, p, cfg)
    yield
    _gla_block_prep(r, p[:, QA[0]:QA[1]] * (DK_A ** -0.5), r["ka"][...], p[:, IA[0]:IA[1]], g_a, pos, masks,
                    ("qa", "kta", "da"), H_A, DK_A, DV_A, OUT_A, c)
    yield
    _gla_block_prep(r, p[:, QB[0]:QB[1]] * (DK_B ** -0.5), p[:, KB[0]:KB[1]], p[:, VB[0]:VB[1]], g_b, pos, masks,
                    ("qb", "ktb", "db"), H_B, DK_B, DV_B, OUT_B, c)


def _stage_b(r, cfg):
    rows, c = cfg.rows, cfg.chunk
    assert cfg.tseq == c
    p = r["p"]

    da, db, dcol = r["da"][...], r["db"][...], r["dc"][...]
    for q in range(rows // c):
        rs = slice(q * c, (q + 1) * c)
        for (qn, kn, d, st, v, base, heads, dk, dv) in (
                ("qa", "kta", da, r["sa"], p[rs, IA[0]:IA[1]], OUT_A, H_A, DK_A, DV_A),
                ("qb", "ktb", db, r["sb"], p[rs, VB[0]:VB[1]], OUT_B, H_B, DK_B, DV_B)):
            qx, kt = r[qn][rs, :], r[kn][q]
            for h in range(heads):
                ks = slice(h * dk, (h + 1) * dk)
                vs = slice(h * dv, (h + 1) * dv)
                s = st[q, h]
                r["oi"][rs, base + h * dv:base + (h + 1) * dv] = _dot(qx[:, ks], s)
                st[q, h] = s * d[ks, q:q + 1] + jnp.dot(kt[ks, :], v[:, vs].astype(BF16),
                                                        preferred_element_type=F32)
        khr, bkt = r["khr"][q], r["bkt"][q]
        u0, v = r["u0"][rs, :], r["vc"][rs, :]
        for h in range(H_C):
            cs = slice(h * N_C, (h + 1) * N_C)
            s = r["sct"][q, h]
            x = _dot(khr[:, cs], s)
            u = -(x[0:c] + u0[:, cs])
            r["u"][rs, cs] = u
            r["oi"][rs, OUT_C + h * N_C:OUT_C + (h + 1) * N_C] = x[c:2 * c]
            r["sct"][q, h] = s * dcol[cs, q:q + 1] + jnp.dot(
                bkt[cs, :], jnp.concatenate([u, v[:, cs]], axis=0).astype(BF16),
                preferred_element_type=F32)
        yield

    r["o"][...] += r["oi"][...]
    for h in range(H_C):
        cs = slice(h * N_C, (h + 1) * N_C)
        r["o"][:, OUT_C + h * N_C:OUT_C + (h + 1) * N_C] += jnp.dot(
            r["rb"][h], r["u"][:, cs].astype(BF16), preferred_element_type=F32)

    yield

    o = r["o"]
    oa = o[:, OUT_A:OUT_A + W_A]
    ya = (oa * lax.rsqrt(_segsum(oa * oa, r["m64a"][...]) * (1.0 / DV_A) + 1e-5)
          * r["hgrn_norm_w"][...] * _silu(p[:, GATE_A[0]:GATE_A[1]]))
    ob = o[:, OUT_B:OUT_B + W_B]
    yb = (ob * lax.rsqrt(_segsum(ob * ob, r["m96"][...]) * (1.0 / DV_B) + 1e-5)
          * r["gla_norm_w"][...] * _silu(p[:, GATE_B[0]:GATE_B[1]]))
    oc = o[:, OUT_C:D_MODEL]
    m64c = r["m64c"][...]
    dc = oc - _segsum(oc, m64c) * (1.0 / N_C)
    ocn = dc * lax.rsqrt(_segsum(dc * dc, m64c) * (1.0 / N_C) + GN_EPS) * r["ln_w"][...] + r["ln_b"][...]
    bonus = _segsum(r["rc"][...] * r["kc"][...] * r["r_k"][...], m64c) * r["vc"][...]
    yc = (ocn + bonus) * _silu(p[:, GATE_C[0]:GATE_C[1]])
    y = jnp.concatenate([ya, yb, yc], axis=-1)
    out = r["xres"][...].reshape(rows, D_MODEL) + jnp.dot(y.astype(BF16), r["w_out"][...],
                                                          preferred_element_type=F32)
    if cfg.final:
        out = out * lax.rsqrt(jnp.mean(out * out, axis=-1, keepdims=True) + 1e-6) * r["final_norm_w"][...]
    r["xo"][...] = out.reshape(r["xo"].shape)


HANDOFF = ("p", "rc", "vc", "kc", "o", "qa", "qb", "kta", "da", "ktb", "db", "bkt", "dc", "khr", "u0", "rb")


def _interleave(*stages):
    stages = list(stages)
    while stages:
        for stage in list(stages):
            try:
                next(stage)
            except StopIteration:
                stages.remove(stage)


def _layer_kernel(names, cfg, nb, nt, *refs):
    r = dict(zip(names, refs))
    nseq = cfg.rows // cfg.tseq
    g = pl.program_id(0)
    last = nb * nt - 1
    t_a = jnp.minimum(g, last) % nt
    t_b = jnp.maximum(g - 1, 0) % nt

    @pl.when(g == 0)
    def _first():
        for n in HANDOFF:
            r[n + "1"][...] = jnp.zeros(r[n + "1"].shape, r[n + "1"].dtype)

    if not cfg.has_state:
        @pl.when(t_a == 0)
        def _init_carry():
            r["pct"][...] = jnp.zeros(r["pct"].shape, F32)

    @pl.when(t_b == 0)
    def _init_states():
        if cfg.has_state:
            r["sa"][...] = r["sa_in"][...]
            r["sb"][...] = r["sb_in"][...]
            for q in range(nseq):
                for h in range(H_C):
                    r["sct"][q, h] = r["sc_in"][q, h].T
        else:
            r["sa"][...] = jnp.zeros(r["sa"].shape, F32)
            r["sb"][...] = jnp.zeros(r["sb"].shape, F32)
            r["sct"][...] = jnp.zeros(r["sct"].shape, F32)

    def view(slot):
        return {**r, **{n: r[n + str(slot)] for n in HANDOFF}}

    for parity in (0, 1):
        @pl.when(g % 2 == parity)
        def _step(parity=parity):
            _interleave(_stage_a(view(parity), cfg), _stage_b(view(1 - parity), cfg))

    @pl.when(t_b == nt - 1)
    def _emit():
        for q in range(nseq):
            for h in range(H_C):
                r["sc"][q, h] = r["sct"][q, h].T


def _gla_gates(r, p, cfg):
    hl = r["hgrn_lb"][...]
    e = jnp.exp(hl - jnp.max(hl, axis=0, keepdims=True))
    sm = e / jnp.sum(e, axis=0, keepdims=True)
    cum = sm[0:1]
    for j in range(1, cfg.layer + 1):
        cum = cum + sm[j:j + 1]
    lb = cum - sm[0:1]
    za = p[:, ZA[0]:ZA[1]]
    ls = _logsig(za)
    la = jnp.log(jnp.maximum(lb, TINY))
    bb = jnp.log1p(-lb) + ls
    lae = jnp.maximum(la, bb) + _log1pexp(la - bb)
    g_a = jnp.where(lb > 0.0, lae, ls)
    r["ka"][...] = (1.0 - lb) * jax.nn.sigmoid(-za)
    g_b = _logsig(_dot_hi(p[:, GKL[0]:GKL[1]], r["gk_w2"][...]) + r["gk_b"][...]) * (1.0 / GLA_NORM)
    return g_a, g_b


def _rwkv_gates(r, p, cfg):
    rows, tseq = cfg.rows, cfg.tseq
    pc = jnp.concatenate([p[:, PC_MAIN[0]:PC_MAIN[1]], p[:, PC_TAIL[0]:PC_TAIL[1]]], axis=1)
    xs = r["xs"]
    if cfg.has_state:
        row = lax.broadcasted_iota(jnp.int32, (rows, 1), 0)
        prev = jnp.where((row & (tseq - 1)) == 0, r["shift_rows"][...], pltpu.roll(pc, 1, axis=0))
        r["pct"][...] = pc
    else:
        xs[...] = pltpu.roll(pc, 1, axis=0)
        for q in range(rows // tseq):
            xs[q * tseq:q * tseq + 1, :] = r["pct"][(q + 1) * SUBLANES - 1:(q + 1) * SUBLANES, :]
            last = slice((q + 1) * tseq - SUBLANES, (q + 1) * tseq)
            r["pct"][q * SUBLANES:(q + 1) * SUBLANES, 0:N_MAIN] = p[last, PC_MAIN[0]:PC_MAIN[1]]
            r["pct"][q * SUBLANES:(q + 1) * SUBLANES, N_MAIN:D_SHIFT] = p[last, PC_TAIL[0]:PC_TAIL[1]]
        prev = xs[...]
    xs[...] = pc + (prev - pc) * r["mu"][...]
    kc = xs[:, SH_K[0]:SH_K[1]]
    wlog = -_softplus(-(r["w0"][...] + _dot_hi(jnp.tanh(xs[:, SH_WL[0]:SH_WL[1]]), r["w2"][...]))) - 0.5
    lw = -jnp.exp(wlog)
    av = jax.nn.sigmoid(r["a0"][...] + _dot_hi(xs[:, SH_AL[0]:SH_AL[1]], r["a2"][...]))
    kk = kc * r["k_k"][...]
    kk = kk * lax.rsqrt(jnp.maximum(_segsum(kk * kk, r["m64c"][...]), 1e-24))
    r["rc"][...] = xs[:, SH_R[0]:SH_R[1]]
    r["vc"][...] = xs[:, SH_V[0]:SH_V[1]]
    r["kc"][...] = kc * (1.0 + (av - 1.0) * r["k_a"][...])
    return lw, kk, kk * av


def _block_diag_ones(width, seg):
    i = jnp.arange(width) // seg
    return (i[:, None] == i[None, :]).astype(BF16)


def _run_layer(cfg, nb, nt, x3, state, new_states, params):
    l, rows = cfg.layer, cfg.rows
    nseq = rows // cfg.tseq
    nseq_total = nb * nseq
    assert not cfg.has_state or nt == 1
    x_block = (1, rows, D_MODEL) if cfg.has_state else (nseq, cfg.tseq, D_MODEL)
    ins, specs = [], []

    def add(name, arr, spec):
        ins.append((name, arr))
        specs.append(spec)

    def layer_row(name, arr):
        w = arr.shape[-1]
        add(name, arr.reshape(DEPTH, 1, w), pl.BlockSpec((None, 1, w), lambda g: (l, 0, 0)))

    once = pl.Buffered(1)

    def layer_mat(name, arr):
        add(name, arr, pl.BlockSpec((None,) + arr.shape[1:], lambda g: (l, 0, 0), pipeline_mode=once))

    def whole(name, arr):
        add(name, arr, pl.BlockSpec(arr.shape, lambda g: (0,) * arr.ndim, pipeline_mode=once))

    last = nb * nt - 1

    def blk_a(g):
        ga = jnp.minimum(g, last)
        return ga // nt, ga % nt

    def blk_b(g):
        gb = jnp.maximum(g - 1, 0)
        return gb // nt, gb % nt

    add("x", x3, pl.BlockSpec(x_block, lambda g: (*blk_a(g), 0)))
    add("xres", x3, pl.BlockSpec(x_block, lambda g: (*blk_b(g), 0)))
    if cfg.has_state:
        st_a, st_b, st_c, shift_rows = state
        add("shift_rows", shift_rows, pl.BlockSpec((None, rows, D_SHIFT), lambda g: (l, blk_a(g)[0], 0)))
        add("sa_in", st_a, pl.BlockSpec((None, nseq, H_A, DK_A, DV_A), lambda g: (l, blk_b(g)[0], 0, 0, 0)))
        add("sb_in", st_b, pl.BlockSpec((None, nseq, H_B, DK_B, DV_B), lambda g: (l, blk_b(g)[0], 0, 0, 0)))
        add("sc_in", st_c, pl.BlockSpec((None, nseq, H_C, N_C, N_C), lambda g: (l, blk_b(g)[0], 0, 0, 0)))
    layer_row("norm_w", params["norm_w"])
    layer_mat("w_in", params["w_in"])
    whole("hgrn_lb", params["hgrn_lb"])
    layer_row("hgrn_norm_w", params["hgrn_norm_w"])
    layer_mat("gk_w2", params["gla_gk_w2"])
    layer_row("gk_b", params["gla_gk_b"])
    layer_row("gla_norm_w", params["gla_norm_w"])
    layer_row("mu", params["rwkv_mu"])
    layer_row("w0", params["rwkv_w0"])
    layer_mat("w2", params["rwkv_w2"])
    layer_row("a0", params["rwkv_a0"])
    layer_mat("a2", params["rwkv_a2"])
    layer_row("k_k", params["rwkv_k_k"])
    layer_row("k_a", params["rwkv_k_a"])
    layer_row("r_k", params["rwkv_r_k"])
    layer_row("ln_w", params["rwkv_ln_w"])
    layer_row("ln_b", params["rwkv_ln_b"])
    layer_mat("w_out", params["w_out"])
    if cfg.final:
        whole("final_norm_w", params["final_norm_w"].reshape(1, D_MODEL))
    whole("m64a", params["m64a"])
    whole("m96", params["m96"])
    whole("m64c", params["m64c"])

    pct_rows = rows if cfg.has_state else nseq * SUBLANES
    state_dims = ((H_A, DK_A, DV_A), (H_B, DK_B, DV_B), (H_C, N_C, N_C))
    aliases = {}
    for k, (name, arr, dims) in enumerate(zip(("sa_all", "sb_all", "sc_all"), new_states, state_dims)):
        assert arr.shape == (DEPTH, nseq_total) + dims
        aliases[len(ins)] = 1 + k
        add(name, arr, pl.BlockSpec(memory_space=pl.ANY))
    outs = [("xo", jax.ShapeDtypeStruct(x3.shape, F32), pl.BlockSpec(x_block, lambda g: (*blk_b(g), 0)))]
    outs += [(name, jax.ShapeDtypeStruct((DEPTH, nseq_total) + dims, F32),
              pl.BlockSpec((None, nseq) + dims, lambda g: (l, blk_b(g)[0], 0, 0, 0)))
             for name, dims in zip(("sa", "sb", "sc"), state_dims)]
    outs += [("pct", jax.ShapeDtypeStruct((nb * pct_rows, D_SHIFT), F32),
              pl.BlockSpec((pct_rows, D_SHIFT), lambda g: (blk_a(g)[0], 0)))]
    a_w, b_w, c = H_A * DK_A, H_B * DK_B, cfg.chunk
    nch = rows // c
    scratch = [("p", (rows, D_IN), F32), ("xs", (rows, D_SHIFT), F32), ("o", (rows, D_MODEL), F32),
               ("oi", (rows, D_MODEL), F32),
               ("qa", (rows, a_w), F32), ("ka", (rows, a_w), F32), ("qb", (rows, b_w), F32),
               ("rc", (rows, W_C), F32), ("vc", (rows, W_C), F32), ("kc", (rows, W_C), F32),
               ("u0", (rows, W_C), F32), ("u", (rows, W_C), F32),
               ("kta", (nch, a_w, c), BF16), ("da", (a_w, nch), F32),
               ("ktb", (nch, b_w, c), BF16), ("db", (b_w, nch), F32),
               ("bkt", (nch, W_C, 2 * c), BF16), ("dc", (W_C, nch), F32),
               ("khr", (nch, 2 * c, W_C), F32), ("rb", (H_C, rows, rows), BF16),
               ("sct", (nseq, H_C, N_C, N_C), F32)]
    scratch = ([s for s in scratch if s[0] not in HANDOFF]
               + [(n + slot, shape, dt) for n, shape, dt in scratch if n in HANDOFF for slot in "01"])
    names = tuple(n for n, _ in ins) + tuple(n for n, _, _ in outs) + tuple(n for n, _, _ in scratch)
    return pl.pallas_call(
        functools.partial(_layer_kernel, names, cfg, nb, nt),
        grid=(nb * nt + 1,),
        in_specs=specs,
        out_specs=[s for _, _, s in outs],
        out_shape=[s for _, s, _ in outs],
        scratch_shapes=[pltpu.VMEM(shape, dt) for _, shape, dt in scratch],
        input_output_aliases=aliases,
        compiler_params=pltpu.CompilerParams(
            dimension_semantics=("arbitrary",),
            vmem_limit_bytes=VMEM_LIMIT_BYTES),
        name=f"layer{l}_{'sample' if cfg.has_state else 'prompt'}",
    )(*[a for _, a in ins])


_MODEL_SHIFT_ORDER = tuple(n for n in _MODEL_COLS if n in _SHIFT_ORDER)


def _to_kernel_cols(w):
    return jnp.concatenate([w[..., _MODEL_COLS[n][0]:_MODEL_COLS[n][1]] for n in _COL_ORDER], axis=-1)


def _shift_to_kernel(s):
    base = _MODEL_COLS[_MODEL_SHIFT_ORDER[0]][0]
    return jnp.concatenate([s[..., _MODEL_COLS[n][0] - base:_MODEL_COLS[n][1] - base] for n in _SHIFT_ORDER], axis=-1)


def _shift_to_model(s):
    return jnp.concatenate([s[..., _SHIFT[n][0]:_SHIFT[n][1]] for n in _MODEL_SHIFT_ORDER], axis=-1)


def _make_params(norm_w, w_in, hgrn_lb, hgrn_norm_w, gla_gk_w2, gla_gk_b, gla_norm_w, rwkv_mu, rwkv_w0, rwkv_w2,
                 rwkv_a0, rwkv_a2, rwkv_k_k, rwkv_k_a, rwkv_r_k, rwkv_ln_w, rwkv_ln_b, w_out, final_norm_w):
    return dict(
        norm_w=norm_w, w_in=_to_kernel_cols(w_in).astype(BF16), hgrn_lb=hgrn_lb, hgrn_norm_w=hgrn_norm_w,
        gla_gk_w2=gla_gk_w2, gla_gk_b=gla_gk_b, gla_norm_w=gla_norm_w, rwkv_mu=_shift_to_kernel(rwkv_mu),
        rwkv_w0=rwkv_w0, rwkv_w2=rwkv_w2, rwkv_a0=rwkv_a0, rwkv_a2=rwkv_a2, rwkv_k_k=rwkv_k_k,
        rwkv_k_a=rwkv_k_a, rwkv_r_k=rwkv_r_k, rwkv_ln_w=rwkv_ln_w, rwkv_ln_b=rwkv_ln_b,
        w_out=w_out.astype(BF16), final_norm_w=final_norm_w,
        m64a=_block_diag_ones(W_A, DV_A), m96=_block_diag_ones(W_B, DV_B),
        m64c=_block_diag_ones(W_C, N_C),
    )


def _trunk(x, state, params, rows, tseq, chunk):
    bsz, tlen, _ = x.shape
    has_state = state is not None
    nseq = rows // tseq
    assert rows % tseq == 0 and tseq % chunk == 0 and bsz % nseq == 0 and tlen % tseq == 0
    nb, nt = bsz // nseq, tlen // tseq
    if has_state:
        assert tlen == tseq
        st_a, st_b, st_c, st_s = state
        shift_rows = jnp.pad(_shift_to_kernel(st_s)[:, :, None, :], ((0, 0), (0, 0), (0, tlen - 1), (0, 0)))
        state = (st_a, st_b, st_c, shift_rows.reshape(DEPTH, bsz * tlen, D_SHIFT))
        x3 = x.reshape(nb, rows, D_MODEL)
    else:
        x3 = x
    new_states = tuple(jnp.zeros((DEPTH, bsz) + dims, F32)
                       for dims in ((H_A, DK_A, DV_A), (H_B, DK_B, DV_B), (H_C, N_C, N_C)))
    new_s = []
    for l in range(DEPTH):
        cfg = Cfg(layer=l, rows=rows, tseq=tseq, chunk=chunk, has_state=has_state, final=(l == DEPTH - 1))
        x3, *new_states, pct = _run_layer(cfg, nb, nt, x3, state, tuple(new_states), params)
        if has_state:
            new_s.append(pct.reshape(bsz, tlen, D_SHIFT)[:, tlen - 1])
        else:
            new_s.append(pct.reshape(bsz, SUBLANES, D_SHIFT)[:, SUBLANES - 1])
    return (x3.reshape(bsz, tlen, D_MODEL), *new_states, _shift_to_model(jnp.stack(new_s)))


PROMPT_ROWS = 256
PROMPT_TSEQ = 32
PROMPT_CHUNK = 32
SAMPLE_ROWS = 32


def kernel(x_prompt, x_sample, state_hgrn, state_gla, state_rwkv, state_shift, norm_w, w_in, hgrn_lb, hgrn_norm_w, gla_gk_w2, gla_gk_b, gla_norm_w, rwkv_mu, rwkv_w0, rwkv_w2, rwkv_a0, rwkv_a2, rwkv_k_k, rwkv_k_a, rwkv_r_k, rwkv_ln_w, rwkv_ln_b, w_out, final_norm_w):
    params = _make_params(norm_w, w_in, hgrn_lb, hgrn_norm_w, gla_gk_w2, gla_gk_b, gla_norm_w, rwkv_mu, rwkv_w0,
                          rwkv_w2, rwkv_a0, rwkv_a2, rwkv_k_k, rwkv_k_a, rwkv_r_k, rwkv_ln_w, rwkv_ln_b, w_out,
                          final_norm_w)
    y_p, hgrn_p, gla_p, rwkv_p, shift_p = _trunk(x_prompt, None, params, PROMPT_ROWS, PROMPT_TSEQ, PROMPT_CHUNK)
    tok = x_sample.shape[1]
    y_s, hgrn_s, gla_s, rwkv_s, shift_s = _trunk(
        x_sample, (state_hgrn, state_gla, state_rwkv, state_shift), params, SAMPLE_ROWS, tok, tok)
    return (y_p, y_s, hgrn_p, gla_p, rwkv_p, shift_p, hgrn_s, gla_s, rwkv_s, shift_s)
```

```python
import collections
import functools

import jax
import jax.numpy as jnp
from jax import lax
from jax.experimental import pallas as pl
from jax.experimental.pallas import tpu as pltpu

F32 = jnp.float32
BF16 = jnp.bfloat16

D_MODEL = 1024
DEPTH = 4
HEAD = 64
W_A = D_MODEL // 4
H_A = W_A // HEAD
DK_A = HEAD
DV_A = W_A // H_A
W_B = (D_MODEL - W_A) // 2
H_B = 4
DV_B = W_B // H_B
DK_B = DV_B // 2
GLA_LR = 16
GLA_NORM = 16.0
W_C = D_MODEL - W_A - W_B
H_C = W_C // HEAD
N_C = HEAD
DECAY_LR = 32
AAA_LR = 32
GN_EPS = 64e-5
TINY = 1e-30
D_A_IN = 2 * H_A * DK_A + 2 * W_A
D_B_IN = 2 * H_B * DK_B + 2 * W_B + GLA_LR
D_SHIFT = 3 * W_C + DECAY_LR + AAA_LR
D_IN = D_A_IN + D_B_IN + D_SHIFT + W_C


def _ranges(pieces):
    out, pos = {}, 0
    for name, width in pieces:
        out[name] = (pos, pos + width)
        pos += width
    return out


_MODEL_COLS = _ranges((
    ("qa", H_A * DK_A), ("za", H_A * DK_A), ("ia", W_A), ("gate_a", W_A),
    ("qb", H_B * DK_B), ("kb", H_B * DK_B), ("vb", W_B), ("gkl", GLA_LR), ("gate_b", W_B),
    ("r", W_C), ("wl", DECAY_LR), ("kc", W_C), ("vc", W_C), ("al", AAA_LR), ("gate_c", W_C)))
_COL_ORDER = ("qa", "za", "ia", "gate_a", "qb", "kb", "vb", "gate_b", "r", "kc", "vc", "gate_c",
              "gkl", "wl", "al")
_COLS = _ranges(tuple((n, _MODEL_COLS[n][1] - _MODEL_COLS[n][0]) for n in _COL_ORDER))
QA, ZA, IA, GATE_A = _COLS["qa"], _COLS["za"], _COLS["ia"], _COLS["gate_a"]
QB, KB, VB, GATE_B, GKL = _COLS["qb"], _COLS["kb"], _COLS["vb"], _COLS["gate_b"], _COLS["gkl"]
GATE_C = _COLS["gate_c"]
PC_MAIN = (_COLS["r"][0], _COLS["vc"][1])
PC_TAIL = (_COLS["wl"][0], _COLS["al"][1])
_SHIFT_ORDER = ("r", "kc", "vc", "wl", "al")
_SHIFT = _ranges(tuple((n, _MODEL_COLS[n][1] - _MODEL_COLS[n][0]) for n in _SHIFT_ORDER))
SH_R, SH_K, SH_V, SH_WL, SH_AL = (_SHIFT[n] for n in _SHIFT_ORDER)
N_MAIN = PC_MAIN[1] - PC_MAIN[0]
OUT_A = 0
OUT_B = W_A
OUT_C = W_A + W_B

SUBLANES = 8
LANES = 128
VMEM_LIMIT_BYTES = 58 * 1024 * 1024

Cfg = collections.namedtuple("Cfg", "layer rows tseq chunk has_state final")

NN = (((1,), (0,)), ((), ()))
NT = (((1,), (1,)), ((), ()))


def _dot(a, b):
    return jnp.dot(a.astype(BF16), b.astype(BF16), preferred_element_type=F32)


def _split(a):
    hi = a.astype(BF16)
    return hi, (a - hi.astype(F32)).astype(BF16)


def _dot3s(a_split, b_split, dims):
    (ah, al), (bh, bl) = a_split, b_split
    return (lax.dot_general(ah, bh, dims, preferred_element_type=F32)
            + lax.dot_general(ah, bl, dims, preferred_element_type=F32)
            + lax.dot_general(al, bh, dims, preferred_element_type=F32))


def _dot_nt(a, b):
    return lax.dot_general(a.astype(BF16), b.astype(BF16), NT, preferred_element_type=F32)


def _dot_nt3(a, b):
    return _dot3s(_split(a), _split(b), NT)


def _dot_hi(a, b):
    return _dot3s(_split(a), _split(b), NN)


def _segsum(x, m):
    return jnp.dot(x.astype(BF16), m, preferred_element_type=F32)


def _log1pexp(x):
    return jnp.log(1.0 + jnp.exp(-jnp.abs(x)))


def _softplus(x):
    return jnp.maximum(x, 0.0) + _log1pexp(x)


def _logsig(x):
    return jnp.minimum(x, 0.0) - _log1pexp(x)


def _silu(x):
    return x * jax.nn.sigmoid(x)


def _chunk_scan(x, chunk, pos, reverse=False):
    n = x.shape[0]
    s = 1
    while s < chunk:
        if reverse:
            x = x + jnp.where(pos < chunk - s, pltpu.roll(x, n - s, axis=0), 0.0)
        else:
            x = x + jnp.where(pos >= s, pltpu.roll(x, s, axis=0), 0.0)
        s *= 2
    return x


def _chunk_row(x, m, chunk, pos):
    rows, w = x.shape
    if chunk % SUBLANES == 0:
        x3 = x.reshape(rows // chunk, chunk, w)
        return jnp.broadcast_to(x3[:, m:m + 1, :], x3.shape).reshape(rows, w)
    z = jnp.where(pos == m, x, 0.0)
    return jnp.where(pos >= m, _chunk_scan(z, chunk, pos), _chunk_scan(z, chunk, pos, reverse=True))


def _store_chunk_columns(kt_ref, lane0, x, c):
    tile = max(c, SUBLANES)
    packed = 2 * SUBLANES
    for t0 in range(0, x.shape[0], tile):
        if tile % packed == 0:
            xt = x[t0:t0 + tile, :].astype(BF16).T
        else:
            xt = x[t0:t0 + tile, :].T.astype(BF16)
        for j in range(tile // c):
            kt_ref[t0 // c + j, :, lane0:lane0 + c] = xt[:, j * c:(j + 1) * c]


def _store_chunk_decay(d_ref, decay, c):
    for ch in range(decay.shape[0] // c):
        last = (ch + 1) * c - 1
        t0 = last // SUBLANES * SUBLANES
        d_ref[:, ch:ch + 1] = decay[t0:t0 + SUBLANES, :].T[:, last - t0:last - t0 + 1]


def _gla_block_prep(r, q, k, v, g, pos, masks, names, heads, dk, dv, out_base, c):
    qx_name, kt_name, d_name = names
    causal = masks[0]
    cum = _chunk_scan(g, c, pos)
    rel = cum - _chunk_row(cum, c // 2 - 1, c, pos)
    q_rel = q * jnp.exp(rel)
    k_rel = k * jnp.exp(-rel)
    eg = jnp.exp(cum)
    r[qx_name][...] = q * eg
    k_end = k * jnp.exp(_chunk_row(cum, c - 1, c, pos) - cum)
    _store_chunk_columns(r[kt_name], 0, k_end, c)
    _store_chunk_decay(r[d_name], eg, c)
    sub = causal.shape[0]
    for h in range(heads):
        ks = slice(h * dk, (h + 1) * dk)
        for r0 in range(0, q.shape[0], sub):
            rs = slice(r0, r0 + sub)
            sc = jnp.where(causal, _dot_nt(q_rel[rs, ks], k_rel[rs, ks]), 0.0)
            r["o"][rs, out_base + h * dv:out_base + (h + 1) * dv] = _dot(sc, v[rs, h * dv:(h + 1) * dv])


def _unit_lower_inverses(a_list, eye, same, c):
    if c % SUBLANES:
        w, p = [eye - a for a in a_list], a_list
        n = 2
        while n < c:
            p = [_dot3s(_split(x), _split(x), NN) for x in p]
            w = [x + _dot3s(_split(x), _split(y), NN) for x, y in zip(w, p)]
            n *= 2
            yield
        return w
    nblk = eye.shape[0] // c

    def side_by_side(x):
        out = x[0:c]
        for k in range(1, nblk):
            out = out + x[k * c:(k + 1) * c]
        return out

    def block_diag(x):
        return jnp.where(same, jnp.tile(x, (nblk, 1)), 0.0)

    def block_diag_split(x):
        hi = x.astype(BF16).astype(F32)
        return block_diag(hi).astype(BF16), block_diag(x - hi).astype(BF16)

    eye_w = side_by_side(eye)
    pw = [side_by_side(a) for a in a_list]
    w = [eye_w - x for x in pw]
    pw = [_dot3s(_split(x), _split(a), NN) for x, a in zip(pw, a_list)]
    n = 2
    while n < c:
        yield
        p_bd = [block_diag_split(x) for x in pw]
        n *= 2
        if n < c:
            prod = [_dot3s(_split(jnp.concatenate([x, y], axis=0)), s, NN) for x, y, s in zip(pw, w, p_bd)]
            pw = [z[0:c] for z in prod]
            w = [y + z[c:2 * c] for y, z in zip(w, prod)]
        else:
            w = [y + _dot3s(_split(y), s, NN) for y, s in zip(w, p_bd)]
    return [block_diag(y) for y in w]


def _rwkv_block_prep(r, lw, kk, b, pos, masks, c):
    causal, strict, eye, same = masks
    nch = lw.shape[0] // c
    rr, kmod, v = r["rc"][...], r["kc"][...], r["vc"][...]
    lc = _chunk_scan(lw, c, pos)
    el = jnp.exp(lc)
    inv_cum = jnp.exp(-lc)
    kk_in = kk * jnp.exp(lc - lw)
    b_out = b * inv_cum
    k_out = kmod * inv_cum
    r_in = rr * el
    to_end = jnp.exp(_chunk_row(lc, c - 1, c, pos) - lc)
    _store_chunk_columns(r["bkt"], 0, b * to_end, c)
    _store_chunk_columns(r["bkt"], c, kmod * to_end, c)
    _store_chunk_decay(r["dc"], el, c)
    for ch in range(nch):
        r["khr"][ch, c:2 * c, :] = r_in[ch * c:(ch + 1) * c, :]
    sub = eye.shape[0]
    parts = [(h, slice(r0, r0 + sub), slice(h * N_C, (h + 1) * N_C))
             for h in range(H_C) for r0 in range(0, lw.shape[0], sub)]
    a = [jnp.where(strict, _dot_nt3(kk_in[rs, cs], b_out[rs, cs]), 0.0) for _, rs, cs in parts]
    yield
    w_s = [_split(x) for x in (yield from _unit_lower_inverses(a, eye, same, c))]
    yield
    bv = [_dot(jnp.where(strict, _dot_nt3(kk_in[rs, cs], k_out[rs, cs]), 0.0), v[rs, cs]) for _, rs, cs in parts]
    yield
    for k, (h, rs, cs) in enumerate(parts):
        z = _dot3s(w_s[k], _split(jnp.concatenate([kk_in[rs, cs], bv[k]], axis=1)), NN)
        r["u0"][rs, cs] = z[:, N_C:2 * N_C]
        for j in range(sub // c):
            r["khr"][rs.start // c + j, 0:c, cs] = z[j * c:(j + 1) * c, 0:N_C]
    yield
    for h, rs, cs in parts:
        r["rb"][h] = jnp.where(causal, _dot_nt(r_in[rs, cs], b_out[rs, cs]), 0.0).astype(BF16)
        rk = jnp.where(causal, _dot_nt(r_in[rs, cs], k_out[rs, cs]), 0.0)
        r["o"][rs, OUT_C + h * N_C:OUT_C + (h + 1) * N_C] = _dot(rk, v[rs, cs])


def _stage_a(r, cfg):
    rows, c = cfg.rows, cfg.chunk
    p = r["p"]
    x = r["x"][...].reshape(rows, D_MODEL)
    hn = x * lax.rsqrt(jnp.mean(x * x, axis=-1, keepdims=True) + 1e-6) * r["norm_w"][...]
    p[...] = jnp.dot(hn.astype(BF16), r["w_in"][...], preferred_element_type=F32)
    yield

    pos = lax.broadcasted_iota(jnp.int32, (rows, 1), 0) & (c - 1)
    shift = c.bit_length() - 1
    i = lax.broadcasted_iota(jnp.int32, (rows, rows), 0)
    j = lax.broadcasted_iota(jnp.int32, (rows, rows), 1)
    same = (i >> shift) == (j >> shift)
    masks = (same & (j <= i), same & (j < i), jnp.where(i == j, 1.0, 0.0), same)

    lw, kk, b = _rwkv_gates(r, p, cfg)
    yield
    yield from _rwkv_block_prep(r, lw, kk, b, pos, masks, c)
    yield
    g_a, g_b = _gla_gates(r, p, cfg)
    yield
    _gla_block_prep(r, p[:, QA[0]:QA[1]] * (DK_A ** -0.5), r["ka"][...], p[:, IA[0]:IA[1]], g_a, pos, masks,
                    ("qa", "kta", "da"), H_A, DK_A, DV_A, OUT_A, c)
    yield
    _gla_block_prep(r, p[:, QB[0]:QB[1]] * (DK_B ** -0.5), p[:, KB[0]:KB[1]], p[:, VB[0]:VB[1]], g_b, pos, masks,
                    ("qb", "ktb", "db"), H_B, DK_B, DV_B, OUT_B, c)


def _stage_b(r, cfg):
    rows, c = cfg.rows, cfg.chunk
    assert cfg.tseq == c
    p = r["p"]

    da, db, dcol = r["da"][...], r["db"][...], r["dc"][...]
    for q in range(rows // c):
        rs = slice(q * c, (q + 1) * c)
        for (qn, kn, d, st, v, base, heads, dk, dv) in (
                ("qa", "kta", da, r["sa"], p[rs, IA[0]:IA[1]], OUT_A, H_A, DK_A, DV_A),
                ("qb", "ktb", db, r["sb"], p[rs, VB[0]:VB[1]], OUT_B, H_B, DK_B, DV_B)):
            qx, kt = r[qn][rs, :], r[kn][q]
            for h in range(heads):
                ks = slice(h * dk, (h + 1) * dk)
                vs = slice(h * dv, (h + 1) * dv)
                s = st[q, h]
                r["oi"][rs, base + h * dv:base + (h + 1) * dv] = _dot(qx[:, ks], s)
                st[q, h] = s * d[ks, q:q + 1] + jnp.dot(kt[ks, :], v[:, vs].astype(BF16),
                                                        preferred_element_type=F32)
        khr, bkt = r["khr"][q], r["bkt"][q]
        u0, v = r["u0"][rs, :], r["vc"][rs, :]
        for h in range(H_C):
            cs = slice(h * N_C, (h + 1) * N_C)
            s = r["sct"][q, h]
            x = _dot(khr[:, cs], s)
            u = -(x[0:c] + u0[:, cs])
            r["u"][rs, cs] = u
            r["oi"][rs, OUT_C + h * N_C:OUT_C + (h + 1) * N_C] = x[c:2 * c]
            r["sct"][q, h] = s * dcol[cs, q:q + 1] + jnp.dot(
                bkt[cs, :], jnp.concatenate([u, v[:, cs]], axis=0).astype(BF16),
                preferred_element_type=F32)
        yield

    r["o"][...] += r["oi"][...]
    for h in range(H_C):
        cs = slice(h * N_C, (h + 1) * N_C)
        r["o"][:, OUT_C + h * N_C:OUT_C + (h + 1) * N_C] += jnp.dot(
            r["rb"][h], r["u"][:, cs].astype(BF16), preferred_element_type=F32)

    yield

    o = r["o"]
    oa = o[:, OUT_A:OUT_A + W_A]
    ya = (oa * lax.rsqrt(_segsum(oa * oa, r["m64a"][...]) * (1.0 / DV_A) + 1e-5)
          * r["hgrn_norm_w"][...] * _silu(p[:, GATE_A[0]:GATE_A[1]]))
    ob = o[:, OUT_B:OUT_B + W_B]
    yb = (ob * lax.rsqrt(_segsum(ob * ob, r["m96"][...]) * (1.0 / DV_B) + 1e-5)
          * r["gla_norm_w"][...] * _silu(p[:, GATE_B[0]:GATE_B[1]]))
    oc = o[:, OUT_C:D_MODEL]
    m64c = r["m64c"][...]
    dc = oc - _segsum(oc, m64c) * (1.0 / N_C)
    ocn = dc * lax.rsqrt(_segsum(dc * dc, m64c) * (1.0 / N_C) + GN_EPS) * r["ln_w"][...] + r["ln_b"][...]
    bonus = _segsum(r["rc"][...] * r["kc"][...] * r["r_k"][...], m64c) * r["vc"][...]
    yc = (ocn + bonus) * _silu(p[:, GATE_C[0]:GATE_C[1]])
    y = jnp.concatenate([ya, yb, yc], axis=-1)
    out = r["xres"][...].reshape(rows, D_MODEL) + jnp.dot(y.astype(BF16), r["w_out"][...],
                                                          preferred_element_type=F32)
    if cfg.final:
        out = out * lax.rsqrt(jnp.mean(out * out, axis=-1, keepdims=True) + 1e-6) * r["final_norm_w"][...]
    r["xo"][...] = out.reshape(r["xo"].shape)


HANDOFF = ("p", "rc", "vc", "kc", "o", "qa", "qb", "kta", "da", "ktb", "db", "bkt", "dc", "khr", "u0", "rb")


def _interleave(*stages):
    stages = list(stages)
    while stages:
        for stage in list(stages):
            try:
                next(stage)
            except StopIteration:
                stages.remove(stage)


def _layer_kernel(names, cfg, nb, nt, *refs):
    r = dict(zip(names, refs))
    nseq = cfg.rows // cfg.tseq
    g = pl.program_id(0)
    last = nb * nt - 1
    t_a = jnp.minimum(g, last) % nt
    t_b = jnp.maximum(g - 1, 0) % nt

    @pl.when(g == 0)
    def _first():
        for n in HANDOFF:
            r[n + "1"][...] = jnp.zeros(r[n + "1"].shape, r[n + "1"].dtype)

    if not cfg.has_state:
        @pl.when(t_a == 0)
        def _init_carry():
            r["pct"][...] = jnp.zeros(r["pct"].shape, F32)

    @pl.when(t_b == 0)
    def _init_states():
        if cfg.has_state:
            r["sa"][...] = r["sa_in"][...]
            r["sb"][...] = r["sb_in"][...]
            for q in range(nseq):
                for h in range(H_C):
                    r["sct"][q, h] = r["sc_in"][q, h].T
        else:
            r["sa"][...] = jnp.zeros(r["sa"].shape, F32)
            r["sb"][...] = jnp.zeros(r["sb"].shape, F32)
            r["sct"][...] = jnp.zeros(r["sct"].shape, F32)

    def view(slot):
        return {**r, **{n: r[n + str(slot)] for n in HANDOFF}}

    for parity in (0, 1):
        @pl.when(g % 2 == parity)
        def _step(parity=parity):
            _interleave(_stage_a(view(parity), cfg), _stage_b(view(1 - parity), cfg))

    @pl.when(t_b == nt - 1)
    def _emit():
        for q in range(nseq):
            for h in range(H_C):
                r["sc"][q, h] = r["sct"][q, h].T


def _gla_gates(r, p, cfg):
    hl = r["hgrn_lb"][...]
    e = jnp.exp(hl - jnp.max(hl, axis=0, keepdims=True))
    sm = e / jnp.sum(e, axis=0, keepdims=True)
    cum = sm[0:1]
    for j in range(1, cfg.layer + 1):
        cum = cum + sm[j:j + 1]
    lb = cum - sm[0:1]
    za = p[:, ZA[0]:ZA[1]]
    ls = _logsig(za)
    la = jnp.log(jnp.maximum(lb, TINY))
    bb = jnp.log1p(-lb) + ls
    lae = jnp.maximum(la, bb) + _log1pexp(la - bb)
    g_a = jnp.where(lb > 0.0, lae, ls)
    r["ka"][...] = (1.0 - lb) * jax.nn.sigmoid(-za)
    g_b = _logsig(_dot_hi(p[:, GKL[0]:GKL[1]], r["gk_w2"][...]) + r["gk_b"][...]) * (1.0 / GLA_NORM)
    return g_a, g_b


def _rwkv_gates(r, p, cfg):
    rows, tseq = cfg.rows, cfg.tseq
    pc = jnp.concatenate([p[:, PC_MAIN[0]:PC_MAIN[1]], p[:, PC_TAIL[0]:PC_TAIL[1]]], axis=1)
    xs = r["xs"]
    if cfg.has_state:
        row = lax.broadcasted_iota(jnp.int32, (rows, 1), 0)
        prev = jnp.where((row & (tseq - 1)) == 0, r["shift_rows"][...], pltpu.roll(pc, 1, axis=0))
        r["pct"][...] = pc
    else:
        xs[...] = pltpu.roll(pc, 1, axis=0)
        for q in range(rows // tseq):
            xs[q * tseq:q * tseq + 1, :] = r["pct"][(q + 1) * SUBLANES - 1:(q + 1) * SUBLANES, :]
            last = slice((q + 1) * tseq - SUBLANES, (q + 1) * tseq)
            r["pct"][q * SUBLANES:(q + 1) * SUBLANES, 0:N_MAIN] = p[last, PC_MAIN[0]:PC_MAIN[1]]
            r["pct"][q * SUBLANES:(q + 1) * SUBLANES, N_MAIN:D_SHIFT] = p[last, PC_TAIL[0]:PC_TAIL[1]]
        prev = xs[...]
    xs[...] = pc + (prev - pc) * r["mu"][...]
    kc = xs[:, SH_K[0]:SH_K[1]]
    wlog = -_softplus(-(r["w0"][...] + _dot_hi(jnp.tanh(xs[:, SH_WL[0]:SH_WL[1]]), r["w2"][...]))) - 0.5
    lw = -jnp.exp(wlog)
    av = jax.nn.sigmoid(r["a0"][...] + _dot_hi(xs[:, SH_AL[0]:SH_AL[1]], r["a2"][...]))
    kk = kc * r["k_k"][...]
    kk = kk * lax.rsqrt(jnp.maximum(_segsum(kk * kk, r["m64c"][...]), 1e-24))
    r["rc"][...] = xs[:, SH_R[0]:SH_R[1]]
    r["vc"][...] = xs[:, SH_V[0]:SH_V[1]]
    r["kc"][...] = kc * (1.0 + (av - 1.0) * r["k_a"][...])
    return lw, kk, kk * av


def _block_diag_ones(width, seg):
    i = jnp.arange(width) // seg
    return (i[:, None] == i[None, :]).astype(BF16)


def _run_layer(cfg, nb, nt, x3, state, new_states, params):
    l, rows = cfg.layer, cfg.rows
    nseq = rows // cfg.tseq
    nseq_total = nb * nseq
    assert not cfg.has_state or nt == 1
    x_block = (1, rows, D_MODEL) if cfg.has_state else (nseq, cfg.tseq, D_MODEL)
    ins, specs = [], []

    def add(name, arr, spec):
        ins.append((name, arr))
        specs.append(spec)

    def layer_row(name, arr):
        w = arr.shape[-1]
        add(name, arr.reshape(DEPTH, 1, w), pl.BlockSpec((None, 1, w), lambda g: (l, 0, 0)))

    once = pl.Buffered(1)

    def layer_mat(name, arr):
        add(name, arr, pl.BlockSpec((None,) + arr.shape[1:], lambda g: (l, 0, 0), pipeline_mode=once))

    def whole(name, arr):
        add(name, arr, pl.BlockSpec(arr.shape, lambda g: (0,) * arr.ndim, pipeline_mode=once))

    last = nb * nt - 1

    def blk_a(g):
        ga = jnp.minimum(g, last)
        return ga // nt, ga % nt

    def blk_b(g):
        gb = jnp.maximum(g - 1, 0)
        return gb // nt, gb % nt

    add("x", x3, pl.BlockSpec(x_block, lambda g: (*blk_a(g), 0)))
    add("xres", x3, pl.BlockSpec(x_block, lambda g: (*blk_b(g), 0)))
    if cfg.has_state:
        st_a, st_b, st_c, shift_rows = state
        add("shift_rows", shift_rows, pl.BlockSpec((None, rows, D_SHIFT), lambda g: (l, blk_a(g)[0], 0)))
        add("sa_in", st_a, pl.BlockSpec((None, nseq, H_A, DK_A, DV_A), lambda g: (l, blk_b(g)[0], 0, 0, 0),
                                        pipeline_mode=once))
        add("sb_in", st_b, pl.BlockSpec((None, nseq, H_B, DK_B, DV_B), lambda g: (l, blk_b(g)[0], 0, 0, 0),
                                        pipeline_mode=once))
        add("sc_in", st_c, pl.BlockSpec((None, nseq, H_C, N_C, N_C), lambda g: (l, blk_b(g)[0], 0, 0, 0),
                                        pipeline_mode=once))
    layer_row("norm_w", params["norm_w"])
    layer_mat("w_in", params["w_in"])
    whole("hgrn_lb", params["hgrn_lb"])
    layer_row("hgrn_norm_w", params["hgrn_norm_w"])
    layer_mat("gk_w2", params["gla_gk_w2"])
    layer_row("gk_b", params["gla_gk_b"])
    layer_row("gla_norm_w", params["gla_norm_w"])
    layer_row("mu", params["rwkv_mu"])
    layer_row("w0", params["rwkv_w0"])
    layer_mat("w2", params["rwkv_w2"])
    layer_row("a0", params["rwkv_a0"])
    layer_mat("a2", params["rwkv_a2"])
    layer_row("k_k", params["rwkv_k_k"])
    layer_row("k_a", params["rwkv_k_a"])
    layer_row("r_k", params["rwkv_r_k"])
    layer_row("ln_w", params["rwkv_ln_w"])
    layer_row("ln_b", params["rwkv_ln_b"])
    layer_mat("w_out", params["w_out"])
    if cfg.final:
        whole("final_norm_w", params["final_norm_w"].reshape(1, D_MODEL))
    whole("m64a", params["m64a"])
    whole("m96", params["m96"])
    whole("m64c", params["m64c"])

    pct_rows = rows if cfg.has_state else nseq * SUBLANES
    state_dims = ((H_A, DK_A, DV_A), (H_B, DK_B, DV_B), (H_C, N_C, N_C))
    aliases = {}
    for k, (name, arr, dims) in enumerate(zip(("sa_all", "sb_all", "sc_all"), new_states, state_dims)):
        assert arr.shape == (DEPTH, nseq_total) + dims
        aliases[len(ins)] = 1 + k
        add(name, arr, pl.BlockSpec(memory_space=pl.ANY))
    outs = [("xo", jax.ShapeDtypeStruct(x3.shape, F32), pl.BlockSpec(x_block, lambda g: (*blk_b(g), 0)))]
    outs += [(name, jax.ShapeDtypeStruct((DEPTH, nseq_total) + dims, F32),
              pl.BlockSpec((None, nseq) + dims, lambda g: (l, blk_b(g)[0], 0, 0, 0)))
             for name, dims in zip(("sa", "sb", "sc"), state_dims)]
    outs += [("pct", jax.ShapeDtypeStruct((nb * pct_rows, D_SHIFT), F32),
              pl.BlockSpec((pct_rows, D_SHIFT), lambda g: (blk_a(g)[0], 0)))]
    a_w, b_w, c = H_A * DK_A, H_B * DK_B, cfg.chunk
    nch = rows // c
    scratch = [("p", (rows, D_IN), F32), ("xs", (rows, D_SHIFT), F32), ("o", (rows, D_MODEL), F32),
               ("oi", (rows, D_MODEL), F32),
               ("qa", (rows, a_w), F32), ("ka", (rows, a_w), F32), ("qb", (rows, b_w), F32),
               ("rc", (rows, W_C), F32), ("vc", (rows, W_C), F32), ("kc", (rows, W_C), F32),
               ("u0", (rows, W_C), F32), ("u", (rows, W_C), F32),
               ("kta", (nch, a_w, c), BF16), ("da", (a_w, nch), F32),
               ("ktb", (nch, b_w, c), BF16), ("db", (b_w, nch), F32),
               ("bkt", (nch, W_C, 2 * c), BF16), ("dc", (W_C, nch), F32),
               ("khr", (nch, 2 * c, W_C), F32), ("rb", (H_C, rows, rows), BF16),
               ("sct", (nseq, H_C, N_C, N_C), F32)]
    scratch = ([s for s in scratch if s[0] not in HANDOFF]
               + [(n + slot, shape, dt) for n, shape, dt in scratch if n in HANDOFF for slot in "01"])
    names = tuple(n for n, _ in ins) + tuple(n for n, _, _ in outs) + tuple(n for n, _, _ in scratch)
    return pl.pallas_call(
        functools.partial(_layer_kernel, names, cfg, nb, nt),
        grid=(nb * nt + 1,),
        in_specs=specs,
        out_specs=[s for _, _, s in outs],
        out_shape=[s for _, s, _ in outs],
        scratch_shapes=[pltpu.VMEM(shape, dt) for _, shape, dt in scratch],
        input_output_aliases=aliases,
        compiler_params=pltpu.CompilerParams(
            dimension_semantics=("arbitrary",),
            vmem_limit_bytes=VMEM_LIMIT_BYTES),
        name=f"layer{l}_{'sample' if cfg.has_state else 'prompt'}",
    )(*[a for _, a in ins])


_MODEL_SHIFT_ORDER = tuple(n for n in _MODEL_COLS if n in _SHIFT_ORDER)


def _to_kernel_cols(w):
    return jnp.concatenate([w[..., _MODEL_COLS[n][0]:_MODEL_COLS[n][1]] for n in _COL_ORDER], axis=-1)


def _shift_to_kernel(s):
    base = _MODEL_COLS[_MODEL_SHIFT_ORDER[0]][0]
    return jnp.concatenate([s[..., _MODEL_COLS[n][0] - base:_MODEL_COLS[n][1] - base] for n in _SHIFT_ORDER], axis=-1)


def _shift_to_model(s):
    return jnp.concatenate([s[..., _SHIFT[n][0]:_SHIFT[n][1]] for n in _MODEL_SHIFT_ORDER], axis=-1)


def _make_params(norm_w, w_in, hgrn_lb, hgrn_norm_w, gla_gk_w2, gla_gk_b, gla_norm_w, rwkv_mu, rwkv_w0, rwkv_w2,
                 rwkv_a0, rwkv_a2, rwkv_k_k, rwkv_k_a, rwkv_r_k, rwkv_ln_w, rwkv_ln_b, w_out, final_norm_w):
    return dict(
        norm_w=norm_w, w_in=_to_kernel_cols(w_in).astype(BF16), hgrn_lb=hgrn_lb, hgrn_norm_w=hgrn_norm_w,
        gla_gk_w2=gla_gk_w2, gla_gk_b=gla_gk_b, gla_norm_w=gla_norm_w, rwkv_mu=_shift_to_kernel(rwkv_mu),
        rwkv_w0=rwkv_w0, rwkv_w2=rwkv_w2, rwkv_a0=rwkv_a0, rwkv_a2=rwkv_a2, rwkv_k_k=rwkv_k_k,
        rwkv_k_a=rwkv_k_a, rwkv_r_k=rwkv_r_k, rwkv_ln_w=rwkv_ln_w, rwkv_ln_b=rwkv_ln_b,
        w_out=w_out.astype(BF16), final_norm_w=final_norm_w,
        m64a=_block_diag_ones(W_A, DV_A), m96=_block_diag_ones(W_B, DV_B),
        m64c=_block_diag_ones(W_C, N_C),
    )


def _trunk(x, state, params, rows, tseq, chunk):
    bsz, tlen, _ = x.shape
    has_state = state is not None
    nseq = rows // tseq
    assert rows % tseq == 0 and tseq % chunk == 0 and bsz % nseq == 0 and tlen % tseq == 0
    nb, nt = bsz // nseq, tlen // tseq
    if has_state:
        assert tlen == tseq
        st_a, st_b, st_c, st_s = state
        shift_rows = jnp.pad(_shift_to_kernel(st_s)[:, :, None, :], ((0, 0), (0, 0), (0, tlen - 1), (0, 0)))
        state = (st_a, st_b, st_c, shift_rows.reshape(DEPTH, bsz * tlen, D_SHIFT))
        x3 = x.reshape(nb, rows, D_MODEL)
    else:
        x3 = x
    new_states = tuple(jnp.zeros((DEPTH, bsz) + dims, F32)
                       for dims in ((H_A, DK_A, DV_A), (H_B, DK_B, DV_B), (H_C, N_C, N_C)))
    new_s = []
    for l in range(DEPTH):
        cfg = Cfg(layer=l, rows=rows, tseq=tseq, chunk=chunk, has_state=has_state, final=(l == DEPTH - 1))
        x3, *new_states, pct = _run_layer(cfg, nb, nt, x3, state, tuple(new_states), params)
        if has_state:
            new_s.append(pct.reshape(bsz, tlen, D_SHIFT)[:, tlen - 1])
        else:
            new_s.append(pct.reshape(bsz, SUBLANES, D_SHIFT)[:, SUBLANES - 1])
    return (x3.reshape(bsz, tlen, D_MODEL), *new_states, _shift_to_model(jnp.stack(new_s)))


PROMPT_ROWS = 256
PROMPT_TSEQ = 32
PROMPT_CHUNK = 32
SAMPLE_ROWS = 64


def kernel(x_prompt, x_sample, state_hgrn, state_gla, state_rwkv, state_shift, norm_w, w_in, hgrn_lb, hgrn_norm_w, gla_gk_w2, gla_gk_b, gla_norm_w, rwkv_mu, rwkv_w0, rwkv_w2, rwkv_a0, rwkv_a2, rwkv_k_k, rwkv_k_a, rwkv_r_k, rwkv_ln_w, rwkv_ln_b, w_out, final_norm_w):
    params = _make_params(norm_w, w_in, hgrn_lb, hgrn_norm_w, gla_gk_w2, gla_gk_b, gla_norm_w, rwkv_mu, rwkv_w0,
                          rwkv_w2, rwkv_a0, rwkv_a2, rwkv_k_k, rwkv_k_a, rwkv_r_k, rwkv_ln_w, rwkv_ln_b, w_out,
                          final_norm_w)
    y_p, hgrn_p, gla_p, rwkv_p, shift_p = _trunk(x_prompt, None, params, PROMPT_ROWS, PROMPT_TSEQ, PROMPT_CHUNK)
    tok = x_sample.shape[1]
    y_s, hgrn_s, gla_s, rwkv_s, shift_s = _trunk(
        x_sample, (state_hgrn, state_gla, state_rwkv, state_shift), params, SAMPLE_ROWS, tok, tok)
    return (y_p, y_s, hgrn_p, gla_p, rwkv_p, shift_p, hgrn_s, gla_s, rwkv_s, shift_s)
```

```python
import collections
import functools

import jax
import jax.numpy as jnp
from jax import lax
from jax.experimental import pallas as pl
from jax.experimental.pallas import tpu as pltpu

F32 = jnp.float32
BF16 = jnp.bfloat16

D_MODEL = 1024
DEPTH = 4
HEAD = 64
W_A = D_MODEL // 4
H_A = W_A // HEAD
DK_A = HEAD
DV_A = W_A // H_A
W_B = (D_MODEL - W_A) // 2
H_B = 4
DV_B = W_B // H_B
DK_B = DV_B // 2
GLA_LR = 16
GLA_NORM = 16.0
W_C = D_MODEL - W_A - W_B
H_C = W_C // HEAD
N_C = HEAD
DECAY_LR = 32
AAA_LR = 32
GN_EPS = 64e-5
TINY = 1e-30
D_A_IN = 2 * H_A * DK_A + 2 * W_A
D_B_IN = 2 * H_B * DK_B + 2 * W_B + GLA_LR
D_SHIFT = 3 * W_C + DECAY_LR + AAA_LR
D_IN = D_A_IN + D_B_IN + D_SHIFT + W_C


def _ranges(pieces):
    out, pos = {}, 0
    for name, width in pieces:
        out[name] = (pos, pos + width)
        pos += width
    return out


_MODEL_COLS = _ranges((
    ("qa", H_A * DK_A), ("za", H_A * DK_A), ("ia", W_A), ("gate_a", W_A),
    ("qb", H_B * DK_B), ("kb", H_B * DK_B), ("vb", W_B), ("gkl", GLA_LR), ("gate_b", W_B),
    ("r", W_C), ("wl", DECAY_LR), ("kc", W_C), ("vc", W_C), ("al", AAA_LR), ("gate_c", W_C)))
_COL_ORDER = ("qa", "za", "ia", "gate_a", "qb", "kb", "vb", "gate_b", "r", "kc", "vc", "gate_c",
              "gkl", "wl", "al")
_COLS = _ranges(tuple((n, _MODEL_COLS[n][1] - _MODEL_COLS[n][0]) for n in _COL_ORDER))
QA, ZA, IA, GATE_A = _COLS["qa"], _COLS["za"], _COLS["ia"], _COLS["gate_a"]
QB, KB, VB, GATE_B, GKL = _COLS["qb"], _COLS["kb"], _COLS["vb"], _COLS["gate_b"], _COLS["gkl"]
GATE_C = _COLS["gate_c"]
PC_MAIN = (_COLS["r"][0], _COLS["vc"][1])
PC_TAIL = (_COLS["wl"][0], _COLS["al"][1])
_SHIFT_ORDER = ("r", "kc", "vc", "wl", "al")
_SHIFT = _ranges(tuple((n, _MODEL_COLS[n][1] - _MODEL_COLS[n][0]) for n in _SHIFT_ORDER))
SH_R, SH_K, SH_V, SH_WL, SH_AL = (_SHIFT[n] for n in _SHIFT_ORDER)
N_MAIN = PC_MAIN[1] - PC_MAIN[0]
OUT_A = 0
OUT_B = W_A
OUT_C = W_A + W_B

SUBLANES = 8
LANES = 128
VMEM_LIMIT_BYTES = 58 * 1024 * 1024

Cfg = collections.namedtuple("Cfg", "layer rows tseq chunk lag has_state final")

NN = (((1,), (0,)), ((), ()))
NT = (((1,), (1,)), ((), ()))


def _dot(a, b):
    return jnp.dot(a.astype(BF16), b.astype(BF16), preferred_element_type=F32)


def _split(a):
    hi = a.astype(BF16)
    return hi, (a - hi.astype(F32)).astype(BF16)


def _dot3s(a_split, b_split, dims):
    (ah, al), (bh, bl) = a_split, b_split
    return (lax.dot_general(ah, bh, dims, preferred_element_type=F32)
            + lax.dot_general(ah, bl, dims, preferred_element_type=F32)
            + lax.dot_general(al, bh, dims, preferred_element_type=F32))


def _dot_nt(a, b):
    return lax.dot_general(a.astype(BF16), b.astype(BF16), NT, preferred_element_type=F32)


def _dot_nt3(a, b):
    return _dot3s(_split(a), _split(b), NT)


def _dot_hi(a, b):
    return _dot3s(_split(a), _split(b), NN)


def _segsum(x, m):
    return jnp.dot(x.astype(BF16), m, preferred_element_type=F32)


def _log1pexp(x):
    return jnp.log(1.0 + jnp.exp(-jnp.abs(x)))


def _softplus(x):
    return jnp.maximum(x, 0.0) + _log1pexp(x)


def _logsig(x):
    return jnp.minimum(x, 0.0) - _log1pexp(x)


def _silu(x):
    return x * jax.nn.sigmoid(x)


def _chunk_scan(x, chunk, pos, reverse=False):
    n = x.shape[0]
    s = 1
    while s < chunk:
        if reverse:
            x = x + jnp.where(pos < chunk - s, pltpu.roll(x, n - s, axis=0), 0.0)
        else:
            x = x + jnp.where(pos >= s, pltpu.roll(x, s, axis=0), 0.0)
        s *= 2
    return x


def _chunk_row(x, m, chunk, pos):
    rows, w = x.shape
    if chunk % SUBLANES == 0:
        x3 = x.reshape(rows // chunk, chunk, w)
        return jnp.broadcast_to(x3[:, m:m + 1, :], x3.shape).reshape(rows, w)
    z = jnp.where(pos == m, x, 0.0)
    return jnp.where(pos >= m, _chunk_scan(z, chunk, pos), _chunk_scan(z, chunk, pos, reverse=True))


def _store_chunk_columns(kt_ref, lane0, x, c):
    tile = max(c, SUBLANES)
    packed = 2 * SUBLANES
    for t0 in range(0, x.shape[0], tile):
        if tile % packed == 0:
            xt = x[t0:t0 + tile, :].astype(BF16).T
        else:
            xt = x[t0:t0 + tile, :].T.astype(BF16)
        for j in range(tile // c):
            kt_ref[t0 // c + j, :, lane0:lane0 + c] = xt[:, j * c:(j + 1) * c]


def _store_chunk_decay(d_ref, decay, c):
    for ch in range(decay.shape[0] // c):
        last = (ch + 1) * c - 1
        t0 = last // SUBLANES * SUBLANES
        d_ref[:, ch:ch + 1] = decay[t0:t0 + SUBLANES, :].T[:, last - t0:last - t0 + 1]


def _gla_block_prep(r, q, k, v, g, pos, masks, names, heads, dk, dv, out_base, c):
    qx_name, kt_name, d_name = names
    causal = masks[0]
    cum = _chunk_scan(g, c, pos)
    rel = cum - _chunk_row(cum, c // 2 - 1, c, pos)
    q_rel = q * jnp.exp(rel)
    k_rel = k * jnp.exp(-rel)
    eg = jnp.exp(cum)
    r[qx_name][...] = q * eg
    k_end = k * jnp.exp(_chunk_row(cum, c - 1, c, pos) - cum)
    _store_chunk_columns(r[kt_name], 0, k_end, c)
    _store_chunk_decay(r[d_name], eg, c)
    sub = causal.shape[0]
    for h in range(heads):
        ks = slice(h * dk, (h + 1) * dk)
        for r0 in range(0, q.shape[0], sub):
            rs = slice(r0, r0 + sub)
            sc = jnp.where(causal, _dot_nt(q_rel[rs, ks], k_rel[rs, ks]), 0.0)
            r["o"][rs, out_base + h * dv:out_base + (h + 1) * dv] = _dot(sc, v[rs, h * dv:(h + 1) * dv])


def _unit_lower_inverses(a_list, eye, same, c):
    if c % SUBLANES:
        w, p = [eye - a for a in a_list], a_list
        n = 2
        while n < c:
            p = [_dot3s(_split(x), _split(x), NN) for x in p]
            w = [x + _dot3s(_split(x), _split(y), NN) for x, y in zip(w, p)]
            n *= 2
            yield
        return w
    nblk = eye.shape[0] // c

    def side_by_side(x):
        out = x[0:c]
        for k in range(1, nblk):
            out = out + x[k * c:(k + 1) * c]
        return out

    def block_diag(x):
        return jnp.where(same, jnp.tile(x, (nblk, 1)), 0.0)

    def block_diag_split(x):
        hi = x.astype(BF16).astype(F32)
        return block_diag(hi).astype(BF16), block_diag(x - hi).astype(BF16)

    eye_w = side_by_side(eye)
    pw = [side_by_side(a) for a in a_list]
    w = [eye_w - x for x in pw]
    pw = [_dot3s(_split(x), _split(a), NN) for x, a in zip(pw, a_list)]
    n = 2
    while n < c:
        yield
        p_bd = [block_diag_split(x) for x in pw]
        n *= 2
        if n < c:
            prod = [_dot3s(_split(jnp.concatenate([x, y], axis=0)), s, NN) for x, y, s in zip(pw, w, p_bd)]
            pw = [z[0:c] for z in prod]
            w = [y + z[c:2 * c] for y, z in zip(w, prod)]
        else:
            w = [y + _dot3s(_split(y), s, NN) for y, s in zip(w, p_bd)]
    return [block_diag(y) for y in w]


def _rwkv_block_prep(r, lw, kk, b, pos, masks, c):
    causal, strict, eye, same = masks
    nch = lw.shape[0] // c
    rr, kmod, v = r["rc"][...], r["kc"][...], r["vc"][...]
    lc = _chunk_scan(lw, c, pos)
    el = jnp.exp(lc)
    inv_cum = jnp.exp(-lc)
    kk_in = kk * jnp.exp(lc - lw)
    b_out = b * inv_cum
    k_out = kmod * inv_cum
    r_in = rr * el
    to_end = jnp.exp(_chunk_row(lc, c - 1, c, pos) - lc)
    _store_chunk_columns(r["bkt"], 0, b * to_end, c)
    _store_chunk_columns(r["bkt"], c, kmod * to_end, c)
    _store_chunk_decay(r["dc"], el, c)
    for ch in range(nch):
        r["khr"][ch, c:2 * c, :] = r_in[ch * c:(ch + 1) * c, :]
    sub = eye.shape[0]
    parts = [(h, slice(r0, r0 + sub), slice(h * N_C, (h + 1) * N_C))
             for h in range(H_C) for r0 in range(0, lw.shape[0], sub)]
    a = [jnp.where(strict, _dot_nt3(kk_in[rs, cs], b_out[rs, cs]), 0.0) for _, rs, cs in parts]
    yield
    w_s = [_split(x) for x in (yield from _unit_lower_inverses(a, eye, same, c))]
    yield
    bv = [_dot(jnp.where(strict, _dot_nt3(kk_in[rs, cs], k_out[rs, cs]), 0.0), v[rs, cs]) for _, rs, cs in parts]
    yield
    for k, (h, rs, cs) in enumerate(parts):
        z = _dot3s(w_s[k], _split(jnp.concatenate([kk_in[rs, cs], bv[k]], axis=1)), NN)
        r["u0"][rs, cs] = z[:, N_C:2 * N_C]
        for j in range(sub // c):
            r["khr"][rs.start // c + j, 0:c, cs] = z[j * c:(j + 1) * c, 0:N_C]
    yield
    for h, rs, cs in parts:
        r["rb"][h] = jnp.where(causal, _dot_nt(r_in[rs, cs], b_out[rs, cs]), 0.0).astype(BF16)
        rk = jnp.where(causal, _dot_nt(r_in[rs, cs], k_out[rs, cs]), 0.0)
        r["o"][rs, OUT_C + h * N_C:OUT_C + (h + 1) * N_C] = _dot(rk, v[rs, cs])


def _stage_a(r, cfg):
    rows, c = cfg.rows, cfg.chunk
    p = r["p"]
    x = r["x"][...].reshape(rows, D_MODEL)
    hn = x * lax.rsqrt(jnp.mean(x * x, axis=-1, keepdims=True) + 1e-6) * r["norm_w"][...]
    p[...] = jnp.dot(hn.astype(BF16), r["w_in"][...], preferred_element_type=F32)
    yield

    pos = lax.broadcasted_iota(jnp.int32, (rows, 1), 0) & (c - 1)
    shift = c.bit_length() - 1
    i = lax.broadcasted_iota(jnp.int32, (rows, rows), 0)
    j = lax.broadcasted_iota(jnp.int32, (rows, rows), 1)
    same = (i >> shift) == (j >> shift)
    masks = (same & (j <= i), same & (j < i), jnp.where(i == j, 1.0, 0.0), same)

    lw, kk, b = _rwkv_gates(r, p, cfg)
    yield
    yield from _rwkv_block_prep(r, lw, kk, b, pos, masks, c)
    yield
    g_a, g_b = _gla_gates(r, p, cfg)
    yield
    _gla_block_prep(r, p[:, QA[0]:QA[1]] * (DK_A ** -0.5), r["ka"][...], p[:, IA[0]:IA[1]], g_a, pos, masks,
                    ("qa", "kta", "da"), H_A, DK_A, DV_A, OUT_A, c)
    yield
    _gla_block_prep(r, p[:, QB[0]:QB[1]] * (DK_B ** -0.5), p[:, KB[0]:KB[1]], p[:, VB[0]:VB[1]], g_b, pos, masks,
                    ("qb", "ktb", "db"), H_B, DK_B, DV_B, OUT_B, c)


def _stage_b(r, cfg):
    rows, c = cfg.rows, cfg.chunk
    assert cfg.tseq == c
    p = r["p"]

    da, db, dcol = r["da"][...], r["db"][...], r["dc"][...]
    for q in range(rows // c):
        rs = slice(q * c, (q + 1) * c)
        for (qn, kn, d, st, v, base, heads, dk, dv) in (
                ("qa", "kta", da, r["sa"], p[rs, IA[0]:IA[1]], OUT_A, H_A, DK_A, DV_A),
                ("qb", "ktb", db, r["sb"], p[rs, VB[0]:VB[1]], OUT_B, H_B, DK_B, DV_B)):
            qx, kt = r[qn][rs, :], r[kn][q]
            for h in range(heads):
                ks = slice(h * dk, (h + 1) * dk)
                vs = slice(h * dv, (h + 1) * dv)
                s = st[q, h]
                r["oi"][rs, base + h * dv:base + (h + 1) * dv] = _dot(qx[:, ks], s)
                st[q, h] = s * d[ks, q:q + 1] + jnp.dot(kt[ks, :], v[:, vs].astype(BF16),
                                                        preferred_element_type=F32)
        khr, bkt = r["khr"][q], r["bkt"][q]
        u0, v = r["u0"][rs, :], r["vc"][rs, :]
        for h in range(H_C):
            cs = slice(h * N_C, (h + 1) * N_C)
            s = r["sct"][q, h]
            x = _dot(khr[:, cs], s)
            u = -(x[0:c] + u0[:, cs])
            r["u"][rs, cs] = u
            r["oi"][rs, OUT_C + h * N_C:OUT_C + (h + 1) * N_C] = x[c:2 * c]
            r["sct"][q, h] = s * dcol[cs, q:q + 1] + jnp.dot(
                bkt[cs, :], jnp.concatenate([u, v[:, cs]], axis=0).astype(BF16),
                preferred_element_type=F32)
        yield

    r["o"][...] += r["oi"][...]
    for h in range(H_C):
        cs = slice(h * N_C, (h + 1) * N_C)
        r["o"][:, OUT_C + h * N_C:OUT_C + (h + 1) * N_C] += jnp.dot(
            r["rb"][h], r["u"][:, cs].astype(BF16), preferred_element_type=F32)

    yield

    o = r["o"]
    oa = o[:, OUT_A:OUT_A + W_A]
    ya = (oa * lax.rsqrt(_segsum(oa * oa, r["m64a"][...]) * (1.0 / DV_A) + 1e-5)
          * r["hgrn_norm_w"][...] * _silu(p[:, GATE_A[0]:GATE_A[1]]))
    ob = o[:, OUT_B:OUT_B + W_B]
    yb = (ob * lax.rsqrt(_segsum(ob * ob, r["m96"][...]) * (1.0 / DV_B) + 1e-5)
          * r["gla_norm_w"][...] * _silu(p[:, GATE_B[0]:GATE_B[1]]))
    oc = o[:, OUT_C:D_MODEL]
    m64c = r["m64c"][...]
    dc = oc - _segsum(oc, m64c) * (1.0 / N_C)
    ocn = dc * lax.rsqrt(_segsum(dc * dc, m64c) * (1.0 / N_C) + GN_EPS) * r["ln_w"][...] + r["ln_b"][...]
    bonus = _segsum(r["rc"][...] * r["kc"][...] * r["r_k"][...], m64c) * r["vc"][...]
    yc = (ocn + bonus) * _silu(p[:, GATE_C[0]:GATE_C[1]])
    y = jnp.concatenate([ya, yb, yc], axis=-1)
    out = r["xres"][...].reshape(rows, D_MODEL) + jnp.dot(y.astype(BF16), r["w_out"][...],
                                                          preferred_element_type=F32)
    if cfg.final:
        out = out * lax.rsqrt(jnp.mean(out * out, axis=-1, keepdims=True) + 1e-6) * r["final_norm_w"][...]
    r["xo"][...] = out.reshape(r["xo"].shape)


HANDOFF = ("p", "rc", "vc", "kc", "o", "qa", "qb", "kta", "da", "ktb", "db", "bkt", "dc", "khr", "u0", "rb")


def _interleave(*stages):
    stages = list(stages)
    while stages:
        for stage in list(stages):
            try:
                next(stage)
            except StopIteration:
                stages.remove(stage)


def _layer_kernel(names, cfg, nb, nt, *refs):
    r = dict(zip(names, refs))
    nseq = cfg.rows // cfg.tseq
    g = pl.program_id(0)
    last = nb * nt - 1
    t_a = jnp.minimum(g, last) % nt
    t_b = jnp.maximum(g - cfg.lag, 0) % nt

    if cfg.lag:
        @pl.when(g == 0)
        def _first():
            for n in HANDOFF:
                r[n + "1"][...] = jnp.zeros(r[n + "1"].shape, r[n + "1"].dtype)

    if not cfg.has_state:
        @pl.when(t_a == 0)
        def _init_carry():
            r["pct"][...] = jnp.zeros(r["pct"].shape, F32)

    @pl.when(t_b == 0)
    def _init_states():
        if cfg.has_state:
            r["sa"][...] = r["sa_in"][...]
            r["sb"][...] = r["sb_in"][...]
            for q in range(nseq):
                for h in range(H_C):
                    r["sct"][q, h] = r["sc_in"][q, h].T
        else:
            r["sa"][...] = jnp.zeros(r["sa"].shape, F32)
            r["sb"][...] = jnp.zeros(r["sb"].shape, F32)
            r["sct"][...] = jnp.zeros(r["sct"].shape, F32)

    def view(slot):
        return {**r, **{n: r[n + str(slot)] for n in HANDOFF}}

    if cfg.lag:
        for parity in (0, 1):
            @pl.when(g % 2 == parity)
            def _step(parity=parity):
                _interleave(_stage_a(view(parity), cfg), _stage_b(view(1 - parity), cfg))
    else:
        _interleave(_stage_a(view(0), cfg))
        _interleave(_stage_b(view(0), cfg))

    @pl.when(t_b == nt - 1)
    def _emit():
        for q in range(nseq):
            for h in range(H_C):
                r["sc"][q, h] = r["sct"][q, h].T


def _gla_gates(r, p, cfg):
    hl = r["hgrn_lb"][...]
    e = jnp.exp(hl - jnp.max(hl, axis=0, keepdims=True))
    sm = e / jnp.sum(e, axis=0, keepdims=True)
    cum = sm[0:1]
    for j in range(1, cfg.layer + 1):
        cum = cum + sm[j:j + 1]
    lb = cum - sm[0:1]
    za = p[:, ZA[0]:ZA[1]]
    ls = _logsig(za)
    la = jnp.log(jnp.maximum(lb, TINY))
    bb = jnp.log1p(-lb) + ls
    lae = jnp.maximum(la, bb) + _log1pexp(la - bb)
    g_a = jnp.where(lb > 0.0, lae, ls)
    r["ka"][...] = (1.0 - lb) * jax.nn.sigmoid(-za)
    g_b = _logsig(_dot_hi(p[:, GKL[0]:GKL[1]], r["gk_w2"][...]) + r["gk_b"][...]) * (1.0 / GLA_NORM)
    return g_a, g_b


def _rwkv_gates(r, p, cfg):
    rows, tseq = cfg.rows, cfg.tseq
    pc = jnp.concatenate([p[:, PC_MAIN[0]:PC_MAIN[1]], p[:, PC_TAIL[0]:PC_TAIL[1]]], axis=1)
    xs = r["xs"]
    if cfg.has_state:
        row = lax.broadcasted_iota(jnp.int32, (rows, 1), 0)
        prev = jnp.where((row & (tseq - 1)) == 0, r["shift_rows"][...], pltpu.roll(pc, 1, axis=0))
        r["pct"][...] = pc
    else:
        xs[...] = pltpu.roll(pc, 1, axis=0)
        for q in range(rows // tseq):
            xs[q * tseq:q * tseq + 1, :] = r["pct"][(q + 1) * SUBLANES - 1:(q + 1) * SUBLANES, :]
            last = slice((q + 1) * tseq - SUBLANES, (q + 1) * tseq)
            r["pct"][q * SUBLANES:(q + 1) * SUBLANES, 0:N_MAIN] = p[last, PC_MAIN[0]:PC_MAIN[1]]
            r["pct"][q * SUBLANES:(q + 1) * SUBLANES, N_MAIN:D_SHIFT] = p[last, PC_TAIL[0]:PC_TAIL[1]]
        prev = xs[...]
    xs[...] = pc + (prev - pc) * r["mu"][...]
    kc = xs[:, SH_K[0]:SH_K[1]]
    wlog = -_softplus(-(r["w0"][...] + _dot_hi(jnp.tanh(xs[:, SH_WL[0]:SH_WL[1]]), r["w2"][...]))) - 0.5
    lw = -jnp.exp(wlog)
    av = jax.nn.sigmoid(r["a0"][...] + _dot_hi(xs[:, SH_AL[0]:SH_AL[1]], r["a2"][...]))
    kk = kc * r["k_k"][...]
    kk = kk * lax.rsqrt(jnp.maximum(_segsum(kk * kk, r["m64c"][...]), 1e-24))
    r["rc"][...] = xs[:, SH_R[0]:SH_R[1]]
    r["vc"][...] = xs[:, SH_V[0]:SH_V[1]]
    r["kc"][...] = kc * (1.0 + (av - 1.0) * r["k_a"][...])
    return lw, kk, kk * av


def _block_diag_ones(width, seg):
    i = jnp.arange(width) // seg
    return (i[:, None] == i[None, :]).astype(BF16)


def _run_layer(cfg, nb, nt, x3, state, new_states, params):
    l, rows = cfg.layer, cfg.rows
    nseq = rows // cfg.tseq
    nseq_total = nb * nseq
    assert not cfg.has_state or nt == 1
    x_block = (1, rows, D_MODEL) if cfg.has_state else (nseq, cfg.tseq, D_MODEL)
    ins, specs = [], []

    def add(name, arr, spec):
        ins.append((name, arr))
        specs.append(spec)

    def layer_row(name, arr):
        w = arr.shape[-1]
        add(name, arr.reshape(DEPTH, 1, w), pl.BlockSpec((None, 1, w), lambda g: (l, 0, 0)))

    once = pl.Buffered(1)

    def layer_mat(name, arr):
        add(name, arr, pl.BlockSpec((None,) + arr.shape[1:], lambda g: (l, 0, 0), pipeline_mode=once))

    def whole(name, arr):
        add(name, arr, pl.BlockSpec(arr.shape, lambda g: (0,) * arr.ndim, pipeline_mode=once))

    last = nb * nt - 1
    lag = cfg.lag

    def blk_a(g):
        ga = jnp.minimum(g, last)
        return ga // nt, ga % nt

    def blk_b(g):
        gb = jnp.maximum(g - lag, 0)
        return gb // nt, gb % nt

    add("x", x3, pl.BlockSpec(x_block, lambda g: (*blk_a(g), 0)))
    add("xres", x3, pl.BlockSpec(x_block, lambda g: (*blk_b(g), 0)))
    if cfg.has_state:
        st_a, st_b, st_c, shift_rows = state
        add("shift_rows", shift_rows, pl.BlockSpec((None, rows, D_SHIFT), lambda g: (l, blk_a(g)[0], 0)))
        add("sa_in", st_a, pl.BlockSpec((None, nseq, H_A, DK_A, DV_A), lambda g: (l, blk_b(g)[0], 0, 0, 0)))
        add("sb_in", st_b, pl.BlockSpec((None, nseq, H_B, DK_B, DV_B), lambda g: (l, blk_b(g)[0], 0, 0, 0)))
        add("sc_in", st_c, pl.BlockSpec((None, nseq, H_C, N_C, N_C), lambda g: (l, blk_b(g)[0], 0, 0, 0)))
    layer_row("norm_w", params["norm_w"])
    layer_mat("w_in", params["w_in"])
    whole("hgrn_lb", params["hgrn_lb"])
    layer_row("hgrn_norm_w", params["hgrn_norm_w"])
    layer_mat("gk_w2", params["gla_gk_w2"])
    layer_row("gk_b", params["gla_gk_b"])
    layer_row("gla_norm_w", params["gla_norm_w"])
    layer_row("mu", params["rwkv_mu"])
    layer_row("w0", params["rwkv_w0"])
    layer_mat("w2", params["rwkv_w2"])
    layer_row("a0", params["rwkv_a0"])
    layer_mat("a2", params["rwkv_a2"])
    layer_row("k_k", params["rwkv_k_k"])
    layer_row("k_a", params["rwkv_k_a"])
    layer_row("r_k", params["rwkv_r_k"])
    layer_row("ln_w", params["rwkv_ln_w"])
    layer_row("ln_b", params["rwkv_ln_b"])
    layer_mat("w_out", params["w_out"])
    if cfg.final:
        whole("final_norm_w", params["final_norm_w"].reshape(1, D_MODEL))
    whole("m64a", params["m64a"])
    whole("m96", params["m96"])
    whole("m64c", params["m64c"])

    pct_rows = rows if cfg.has_state else nseq * SUBLANES
    state_dims = ((H_A, DK_A, DV_A), (H_B, DK_B, DV_B), (H_C, N_C, N_C))
    aliases = {}
    for k, (name, arr, dims) in enumerate(zip(("sa_all", "sb_all", "sc_all"), new_states, state_dims)):
        assert arr.shape == (DEPTH, nseq_total) + dims
        aliases[len(ins)] = 1 + k
        add(name, arr, pl.BlockSpec(memory_space=pl.ANY))
    outs = [("xo", jax.ShapeDtypeStruct(x3.shape, F32), pl.BlockSpec(x_block, lambda g: (*blk_b(g), 0)))]
    outs += [(name, jax.ShapeDtypeStruct((DEPTH, nseq_total) + dims, F32),
              pl.BlockSpec((None, nseq) + dims, lambda g: (l, blk_b(g)[0], 0, 0, 0)))
             for name, dims in zip(("sa", "sb", "sc"), state_dims)]
    outs += [("pct", jax.ShapeDtypeStruct((nb * pct_rows, D_SHIFT), F32),
              pl.BlockSpec((pct_rows, D_SHIFT), lambda g: (blk_a(g)[0], 0)))]
    a_w, b_w, c = H_A * DK_A, H_B * DK_B, cfg.chunk
    nch = rows // c
    scratch = [("p", (rows, D_IN), F32), ("xs", (rows, D_SHIFT), F32), ("o", (rows, D_MODEL), F32),
               ("oi", (rows, D_MODEL), F32),
               ("qa", (rows, a_w), F32), ("ka", (rows, a_w), F32), ("qb", (rows, b_w), F32),
               ("rc", (rows, W_C), F32), ("vc", (rows, W_C), F32), ("kc", (rows, W_C), F32),
               ("u0", (rows, W_C), F32), ("u", (rows, W_C), F32),
               ("kta", (nch, a_w, c), BF16), ("da", (a_w, nch), F32),
               ("ktb", (nch, b_w, c), BF16), ("db", (b_w, nch), F32),
               ("bkt", (nch, W_C, 2 * c), BF16), ("dc", (W_C, nch), F32),
               ("khr", (nch, 2 * c, W_C), F32), ("rb", (H_C, rows, rows), BF16),
               ("sct", (nseq, H_C, N_C, N_C), F32)]
    scratch = ([s for s in scratch if s[0] not in HANDOFF]
               + [(n + slot, shape, dt) for n, shape, dt in scratch if n in HANDOFF for slot in "01"[:1 + lag]])
    names = tuple(n for n, _ in ins) + tuple(n for n, _, _ in outs) + tuple(n for n, _, _ in scratch)
    return pl.pallas_call(
        functools.partial(_layer_kernel, names, cfg, nb, nt),
        grid=(nb * nt + lag,),
        in_specs=specs,
        out_specs=[s for _, _, s in outs],
        out_shape=[s for _, s, _ in outs],
        scratch_shapes=[pltpu.VMEM(shape, dt) for _, shape, dt in scratch],
        input_output_aliases=aliases,
        compiler_params=pltpu.CompilerParams(
            dimension_semantics=("arbitrary",),
            vmem_limit_bytes=VMEM_LIMIT_BYTES),
        name=f"layer{l}_{'sample' if cfg.has_state else 'prompt'}",
    )(*[a for _, a in ins])


_MODEL_SHIFT_ORDER = tuple(n for n in _MODEL_COLS if n in _SHIFT_ORDER)


def _to_kernel_cols(w):
    return jnp.concatenate([w[..., _MODEL_COLS[n][0]:_MODEL_COLS[n][1]] for n in _COL_ORDER], axis=-1)


def _shift_to_kernel(s):
    base = _MODEL_COLS[_MODEL_SHIFT_ORDER[0]][0]
    return jnp.concatenate([s[..., _MODEL_COLS[n][0] - base:_MODEL_COLS[n][1] - base] for n in _SHIFT_ORDER], axis=-1)


def _shift_to_model(s):
    return jnp.concatenate([s[..., _SHIFT[n][0]:_SHIFT[n][1]] for n in _MODEL_SHIFT_ORDER], axis=-1)


def _make_params(norm_w, w_in, hgrn_lb, hgrn_norm_w, gla_gk_w2, gla_gk_b, gla_norm_w, rwkv_mu, rwkv_w0, rwkv_w2,
                 rwkv_a0, rwkv_a2, rwkv_k_k, rwkv_k_a, rwkv_r_k, rwkv_ln_w, rwkv_ln_b, w_out, final_norm_w):
    return dict(
        norm_w=norm_w, w_in=_to_kernel_cols(w_in).astype(BF16), hgrn_lb=hgrn_lb, hgrn_norm_w=hgrn_norm_w,
        gla_gk_w2=gla_gk_w2, gla_gk_b=gla_gk_b, gla_norm_w=gla_norm_w, rwkv_mu=_shift_to_kernel(rwkv_mu),
        rwkv_w0=rwkv_w0, rwkv_w2=rwkv_w2, rwkv_a0=rwkv_a0, rwkv_a2=rwkv_a2, rwkv_k_k=rwkv_k_k,
        rwkv_k_a=rwkv_k_a, rwkv_r_k=rwkv_r_k, rwkv_ln_w=rwkv_ln_w, rwkv_ln_b=rwkv_ln_b,
        w_out=w_out.astype(BF16), final_norm_w=final_norm_w,
        m64a=_block_diag_ones(W_A, DV_A), m96=_block_diag_ones(W_B, DV_B),
        m64c=_block_diag_ones(W_C, N_C),
    )


def _trunk(x, state, params, rows, tseq, chunk, lag):
    bsz, tlen, _ = x.shape
    has_state = state is not None
    nseq = rows // tseq
    assert rows % tseq == 0 and tseq % chunk == 0 and bsz % nseq == 0 and tlen % tseq == 0
    nb, nt = bsz // nseq, tlen // tseq
    if has_state:
        assert tlen == tseq
        st_a, st_b, st_c, st_s = state
        shift_rows = jnp.pad(_shift_to_kernel(st_s)[:, :, None, :], ((0, 0), (0, 0), (0, tlen - 1), (0, 0)))
        state = (st_a, st_b, st_c, shift_rows.reshape(DEPTH, bsz * tlen, D_SHIFT))
        x3 = x.reshape(nb, rows, D_MODEL)
    else:
        x3 = x
    new_states = tuple(jnp.zeros((DEPTH, bsz) + dims, F32)
                       for dims in ((H_A, DK_A, DV_A), (H_B, DK_B, DV_B), (H_C, N_C, N_C)))
    new_s = []
    for l in range(DEPTH):
        cfg = Cfg(layer=l, rows=rows, tseq=tseq, chunk=chunk, lag=lag, has_state=has_state,
                  final=(l == DEPTH - 1))
        x3, *new_states, pct = _run_layer(cfg, nb, nt, x3, state, tuple(new_states), params)
        if has_state:
            new_s.append(pct.reshape(bsz, tlen, D_SHIFT)[:, tlen - 1])
        else:
            new_s.append(pct.reshape(bsz, SUBLANES, D_SHIFT)[:, SUBLANES - 1])
    return (x3.reshape(bsz, tlen, D_MODEL), *new_states, _shift_to_model(jnp.stack(new_s)))


PROMPT_ROWS = 256
PROMPT_TSEQ = 32
PROMPT_CHUNK = 32
PROMPT_LAG = 1
SAMPLE_ROWS = 64
SAMPLE_LAG = 0


def kernel(x_prompt, x_sample, state_hgrn, state_gla, state_rwkv, state_shift, norm_w, w_in, hgrn_lb, hgrn_norm_w, gla_gk_w2, gla_gk_b, gla_norm_w, rwkv_mu, rwkv_w0, rwkv_w2, rwkv_a0, rwkv_a2, rwkv_k_k, rwkv_k_a, rwkv_r_k, rwkv_ln_w, rwkv_ln_b, w_out, final_norm_w):
    params = _make_params(norm_w, w_in, hgrn_lb, hgrn_norm_w, gla_gk_w2, gla_gk_b, gla_norm_w, rwkv_mu, rwkv_w0,
                          rwkv_w2, rwkv_a0, rwkv_a2, rwkv_k_k, rwkv_k_a, rwkv_r_k, rwkv_ln_w, rwkv_ln_b, w_out,
                          final_norm_w)
    y_p, hgrn_p, gla_p, rwkv_p, shift_p = _trunk(x_prompt, None, params, PROMPT_ROWS, PROMPT_TSEQ, PROMPT_CHUNK,
                                                 PROMPT_LAG)
    tok = x_sample.shape[1]
    y_s, hgrn_s, gla_s, rwkv_s, shift_s = _trunk(
        x_sample, (state_hgrn, state_gla, state_rwkv, state_shift), params, SAMPLE_ROWS, tok, tok, SAMPLE_LAG)
    return (y_p, y_s, hgrn_p, gla_p, rwkv_p, shift_p, hgrn_s, gla_s, rwkv_s, shift_s)
```

```python
import collections
import functools

import jax
import jax.numpy as jnp
from jax import lax
from jax.experimental import pallas as pl
from jax.experimental.pallas import tpu as pltpu

F32 = jnp.float32
BF16 = jnp.bfloat16

D_MODEL = 1024
DEPTH = 4
HEAD = 64
W_A = D_MODEL // 4
H_A = W_A // HEAD
DK_A = HEAD
DV_A = W_A // H_A
W_B = (D_MODEL - W_A) // 2
H_B = 4
DV_B = W_B // H_B
DK_B = DV_B // 2
GLA_LR = 16
GLA_NORM = 16.0
W_C = D_MODEL - W_A - W_B
H_C = W_C // HEAD
N_C = HEAD
DECAY_LR = 32
AAA_LR = 32
GN_EPS = 64e-5
TINY = 1e-30
D_A_IN = 2 * H_A * DK_A + 2 * W_A
D_B_IN = 2 * H_B * DK_B + 2 * W_B + GLA_LR
D_SHIFT = 3 * W_C + DECAY_LR + AAA_LR
D_IN = D_A_IN + D_B_IN + D_SHIFT + W_C


def _ranges(pieces):
    out, pos = {}, 0
    for name, width in pieces:
        out[name] = (pos, pos + width)
        pos += width
    return out


_MODEL_COLS = _ranges((
    ("qa", H_A * DK_A), ("za", H_A * DK_A), ("ia", W_A), ("gate_a", W_A),
    ("qb", H_B * DK_B), ("kb", H_B * DK_B), ("vb", W_B), ("gkl", GLA_LR), ("gate_b", W_B),
    ("r", W_C), ("wl", DECAY_LR), ("kc", W_C), ("vc", W_C), ("al", AAA_LR), ("gate_c", W_C)))
_COL_ORDER = ("qa", "za", "ia", "gate_a", "qb", "kb", "vb", "gate_b", "r", "kc", "vc", "gate_c",
              "gkl", "wl", "al")
_COLS = _ranges(tuple((n, _MODEL_COLS[n][1] - _MODEL_COLS[n][0]) for n in _COL_ORDER))
QA, ZA, IA, GATE_A = _COLS["qa"], _COLS["za"], _COLS["ia"], _COLS["gate_a"]
QB, KB, VB, GATE_B, GKL = _COLS["qb"], _COLS["kb"], _COLS["vb"], _COLS["gate_b"], _COLS["gkl"]
GATE_C = _COLS["gate_c"]
PC_MAIN = (_COLS["r"][0], _COLS["vc"][1])
PC_TAIL = (_COLS["wl"][0], _COLS["al"][1])
_SHIFT_ORDER = ("r", "kc", "vc", "wl", "al")
_SHIFT = _ranges(tuple((n, _MODEL_COLS[n][1] - _MODEL_COLS[n][0]) for n in _SHIFT_ORDER))
SH_R, SH_K, SH_V, SH_WL, SH_AL = (_SHIFT[n] for n in _SHIFT_ORDER)
N_MAIN = PC_MAIN[1] - PC_MAIN[0]
OUT_A = 0
OUT_B = W_A
OUT_C = W_A + W_B

SUBLANES = 8
LANES = 128
VMEM_LIMIT_BYTES = 58 * 1024 * 1024

Cfg = collections.namedtuple("Cfg", "layer rows tseq chunk lag has_state final")

NN = (((1,), (0,)), ((), ()))
NT = (((1,), (1,)), ((), ()))


def _dot(a, b):
    return jnp.dot(a.astype(BF16), b.astype(BF16), preferred_element_type=F32)


def _split(a):
    hi = a.astype(BF16)
    return hi, (a - hi.astype(F32)).astype(BF16)


def _dot3s(a_split, b_split, dims):
    (ah, al), (bh, bl) = a_split, b_split
    return (lax.dot_general(ah, bh, dims, preferred_element_type=F32)
            + lax.dot_general(ah, bl, dims, preferred_element_type=F32)
            + lax.dot_general(al, bh, dims, preferred_element_type=F32))


def _dot_nt(a, b):
    return lax.dot_general(a.astype(BF16), b.astype(BF16), NT, preferred_element_type=F32)


def _dot_nt3(a, b):
    return _dot3s(_split(a), _split(b), NT)


def _dot_hi(a, b):
    return _dot3s(_split(a), _split(b), NN)


def _segsum(x, m):
    return jnp.dot(x.astype(BF16), m, preferred_element_type=F32)


def _log1pexp(x):
    return jnp.log(1.0 + jnp.exp(-jnp.abs(x)))


def _softplus(x):
    return jnp.maximum(x, 0.0) + _log1pexp(x)


def _logsig(x):
    return jnp.minimum(x, 0.0) - _log1pexp(x)


def _silu(x):
    return x * jax.nn.sigmoid(x)


def _chunk_scan(x, chunk, pos, reverse=False):
    n = x.shape[0]
    s = 1
    while s < chunk:
        if reverse:
            x = x + jnp.where(pos < chunk - s, pltpu.roll(x, n - s, axis=0), 0.0)
        else:
            x = x + jnp.where(pos >= s, pltpu.roll(x, s, axis=0), 0.0)
        s *= 2
    return x


def _chunk_row(x, m, chunk, pos):
    rows, w = x.shape
    if chunk % SUBLANES == 0:
        x3 = x.reshape(rows // chunk, chunk, w)
        return jnp.broadcast_to(x3[:, m:m + 1, :], x3.shape).reshape(rows, w)
    out = x
    for d in range(-m, chunk - m):
        if d:
            out = jnp.where(pos == m + d, pltpu.roll(x, d % rows, axis=0), out)
    return out


def _store_chunk_columns(kt_ref, lane0, x, c):
    tile = max(c, SUBLANES)
    packed = 2 * SUBLANES
    for t0 in range(0, x.shape[0], tile):
        if tile % packed == 0:
            xt = x[t0:t0 + tile, :].astype(BF16).T
        else:
            xt = x[t0:t0 + tile, :].T.astype(BF16)
        for j in range(tile // c):
            kt_ref[t0 // c + j, :, lane0:lane0 + c] = xt[:, j * c:(j + 1) * c]


def _store_chunk_decay(d_ref, decay, c):
    for ch in range(decay.shape[0] // c):
        last = (ch + 1) * c - 1
        t0 = last // SUBLANES * SUBLANES
        d_ref[:, ch:ch + 1] = decay[t0:t0 + SUBLANES, :].T[:, last - t0:last - t0 + 1]


def _gla_block_prep(r, q, k, v, g, pos, gla_masks, names, heads, dk, dv, out_base, c):
    qx_name, kt_name, d_name = names
    cum = _chunk_scan(g, c, pos)
    eg = jnp.exp(cum)
    r[qx_name][...] = q * eg
    k_end = k * jnp.exp(_chunk_row(cum, c - 1, c, pos) - cum)
    _store_chunk_columns(r[kt_name], 0, k_end, c)
    _store_chunk_decay(r[d_name], eg, c)

    terms = [(q, k, gla_masks[0])]
    half = 1
    while half < c:
        ref = _chunk_row(cum, half - 1, 2 * half, pos & (2 * half - 1))
        terms.append((q * jnp.exp(jnp.minimum(cum - ref, 0.0)), k * jnp.exp(jnp.minimum(ref - cum, 0.0)),
                      gla_masks[len(terms)]))
        half *= 2
    for h in range(heads):
        ks = slice(h * dk, (h + 1) * dk)
        sc = 0.0
        for qf, kf, mask in terms:
            sc = jnp.where(mask, _dot_nt(qf[:, ks], kf[:, ks]), sc)
        r["o"][:, out_base + h * dv:out_base + (h + 1) * dv] = _dot(sc, v[:, h * dv:(h + 1) * dv])


def _unit_lower_inverses(a_list, eye, same, c):
    if c % SUBLANES:
        w, p = [eye - a for a in a_list], a_list
        n = 2
        while n < c:
            p = [_dot3s(_split(x), _split(x), NN) for x in p]
            w = [x + _dot3s(_split(x), _split(y), NN) for x, y in zip(w, p)]
            n *= 2
            yield
        return w
    nblk = eye.shape[0] // c

    def side_by_side(x):
        out = x[0:c]
        for k in range(1, nblk):
            out = out + x[k * c:(k + 1) * c]
        return out

    def block_diag(x):
        return jnp.where(same, jnp.tile(x, (nblk, 1)), 0.0)

    def block_diag_split(x):
        hi = x.astype(BF16).astype(F32)
        return block_diag(hi).astype(BF16), block_diag(x - hi).astype(BF16)

    eye_w = side_by_side(eye)
    pw = [side_by_side(a) for a in a_list]
    w = [eye_w - x for x in pw]
    pw = [_dot3s(_split(x), _split(a), NN) for x, a in zip(pw, a_list)]
    n = 2
    while n < c:
        yield
        p_bd = [block_diag_split(x) for x in pw]
        n *= 2
        if n < c:
            prod = [_dot3s(_split(jnp.concatenate([x, y], axis=0)), s, NN) for x, y, s in zip(pw, w, p_bd)]
            pw = [z[0:c] for z in prod]
            w = [y + z[c:2 * c] for y, z in zip(w, prod)]
        else:
            w = [y + _dot3s(_split(y), s, NN) for y, s in zip(w, p_bd)]
    return [block_diag(y) for y in w]


def _rwkv_block_prep(r, lw, kk, b, pos, masks, c):
    causal, strict, eye, same = masks
    nch = lw.shape[0] // c
    rr, kmod, v = r["rc"][...], r["kc"][...], r["vc"][...]
    lc = _chunk_scan(lw, c, pos)
    el = jnp.exp(lc)
    inv_cum = jnp.exp(-lc)
    kk_in = kk * jnp.exp(lc - lw)
    b_out = b * inv_cum
    k_out = kmod * inv_cum
    r_in = rr * el
    to_end = jnp.exp(_chunk_row(lc, c - 1, c, pos) - lc)
    _store_chunk_columns(r["bkt"], 0, b * to_end, c)
    _store_chunk_columns(r["bkt"], c, kmod * to_end, c)
    _store_chunk_decay(r["dc"], el, c)
    for ch in range(nch):
        r["khr"][ch, c:2 * c, :] = r_in[ch * c:(ch + 1) * c, :]
    sub = eye.shape[0]
    parts = [(h, slice(r0, r0 + sub), slice(h * N_C, (h + 1) * N_C))
             for h in range(H_C) for r0 in range(0, lw.shape[0], sub)]
    a = [jnp.where(strict, _dot_nt3(kk_in[rs, cs], b_out[rs, cs]), 0.0) for _, rs, cs in parts]
    yield
    w_s = [_split(x) for x in (yield from _unit_lower_inverses(a, eye, same, c))]
    yield
    bv = [_dot(jnp.where(strict, _dot_nt3(kk_in[rs, cs], k_out[rs, cs]), 0.0), v[rs, cs]) for _, rs, cs in parts]
    yield
    for k, (h, rs, cs) in enumerate(parts):
        z = _dot3s(w_s[k], _split(jnp.concatenate([kk_in[rs, cs], bv[k]], axis=1)), NN)
        r["u0"][rs, cs] = z[:, N_C:2 * N_C]
        for j in range(sub // c):
            r["khr"][rs.start // c + j, 0:c, cs] = z[j * c:(j + 1) * c, 0:N_C]
    yield
    for h, rs, cs in parts:
        r["rb"][h] = jnp.where(causal, _dot_nt(r_in[rs, cs], b_out[rs, cs]), 0.0).astype(BF16)
        rk = jnp.where(causal, _dot_nt(r_in[rs, cs], k_out[rs, cs]), 0.0)
        r["o"][rs, OUT_C + h * N_C:OUT_C + (h + 1) * N_C] = _dot(rk, v[rs, cs])


def _stage_a(r, cfg):
    rows, c = cfg.rows, cfg.chunk
    p = r["p"]
    x = r["x"][...].reshape(rows, D_MODEL)
    hn = x * lax.rsqrt(jnp.mean(x * x, axis=-1, keepdims=True) + 1e-6) * r["norm_w"][...]
    p[...] = jnp.dot(hn.astype(BF16), r["w_in"][...], preferred_element_type=F32)
    yield

    pos = lax.broadcasted_iota(jnp.int32, (rows, 1), 0) & (c - 1)
    shift = c.bit_length() - 1
    i = lax.broadcasted_iota(jnp.int32, (rows, rows), 0)
    j = lax.broadcasted_iota(jnp.int32, (rows, rows), 1)
    same = (i >> shift) == (j >> shift)
    masks = (same & (j <= i), same & (j < i), jnp.where(i == j, 1.0, 0.0), same)
    gla_masks = [i == j]
    half = 1
    while half < c:
        lvl = (2 * half).bit_length() - 1
        gla_masks.append(((i >> lvl) == (j >> lvl)) & ((i & half) != 0) & ((j & half) == 0))
        half *= 2

    lw, kk, b = _rwkv_gates(r, p, cfg)
    yield
    yield from _rwkv_block_prep(r, lw, kk, b, pos, masks, c)
    yield
    g_a, g_b = _gla_gates(r, p, cfg)
    yield
    _gla_block_prep(r, p[:, QA[0]:QA[1]] * (DK_A ** -0.5), r["ka"][...], p[:, IA[0]:IA[1]], g_a, pos, gla_masks,
                    ("qa", "kta", "da"), H_A, DK_A, DV_A, OUT_A, c)
    yield
    _gla_block_prep(r, p[:, QB[0]:QB[1]] * (DK_B ** -0.5), p[:, KB[0]:KB[1]], p[:, VB[0]:VB[1]], g_b, pos,
                    gla_masks, ("qb", "ktb", "db"), H_B, DK_B, DV_B, OUT_B, c)


def _stage_b(r, cfg):
    rows, c = cfg.rows, cfg.chunk
    assert cfg.tseq == c
    p = r["p"]

    da, db, dcol = r["da"][...], r["db"][...], r["dc"][...]
    for q in range(rows // c):
        rs = slice(q * c, (q + 1) * c)
        for (qn, kn, d, st, v, base, heads, dk, dv) in (
                ("qa", "kta", da, r["sa"], p[rs, IA[0]:IA[1]], OUT_A, H_A, DK_A, DV_A),
                ("qb", "ktb", db, r["sb"], p[rs, VB[0]:VB[1]], OUT_B, H_B, DK_B, DV_B)):
            qx, kt = r[qn][rs, :], r[kn][q]
            for h in range(heads):
                ks = slice(h * dk, (h + 1) * dk)
                vs = slice(h * dv, (h + 1) * dv)
                s = st[q, h]
                r["oi"][rs, base + h * dv:base + (h + 1) * dv] = _dot(qx[:, ks], s)
                st[q, h] = s * d[ks, q:q + 1] + jnp.dot(kt[ks, :], v[:, vs].astype(BF16),
                                                        preferred_element_type=F32)
        khr, bkt = r["khr"][q], r["bkt"][q]
        u0, v = r["u0"][rs, :], r["vc"][rs, :]
        for h in range(H_C):
            cs = slice(h * N_C, (h + 1) * N_C)
            s = r["sct"][q, h]
            x = _dot(khr[:, cs], s)
            u = -(x[0:c] + u0[:, cs])
            r["u"][rs, cs] = u
            r["oi"][rs, OUT_C + h * N_C:OUT_C + (h + 1) * N_C] = x[c:2 * c]
            r["sct"][q, h] = s * dcol[cs, q:q + 1] + jnp.dot(
                bkt[cs, :], jnp.concatenate([u, v[:, cs]], axis=0).astype(BF16),
                preferred_element_type=F32)
        yield

    r["o"][...] += r["oi"][...]
    for h in range(H_C):
        cs = slice(h * N_C, (h + 1) * N_C)
        r["o"][:, OUT_C + h * N_C:OUT_C + (h + 1) * N_C] += jnp.dot(
            r["rb"][h], r["u"][:, cs].astype(BF16), preferred_element_type=F32)

    yield

    o = r["o"]
    oa = o[:, OUT_A:OUT_A + W_A]
    ya = (oa * lax.rsqrt(_segsum(oa * oa, r["m64a"][...]) * (1.0 / DV_A) + 1e-5)
          * r["hgrn_norm_w"][...] * _silu(p[:, GATE_A[0]:GATE_A[1]]))
    ob = o[:, OUT_B:OUT_B + W_B]
    yb = (ob * lax.rsqrt(_segsum(ob * ob, r["m96"][...]) * (1.0 / DV_B) + 1e-5)
          * r["gla_norm_w"][...] * _silu(p[:, GATE_B[0]:GATE_B[1]]))
    oc = o[:, OUT_C:D_MODEL]
    m64c = r["m64c"][...]
    dc = oc - _segsum(oc, m64c) * (1.0 / N_C)
    ocn = dc * lax.rsqrt(_segsum(dc * dc, m64c) * (1.0 / N_C) + GN_EPS) * r["ln_w"][...] + r["ln_b"][...]
    bonus = _segsum(r["rc"][...] * r["kc"][...] * r["r_k"][...], m64c) * r["vc"][...]
    yc = (ocn + bonus) * _silu(p[:, GATE_C[0]:GATE_C[1]])
    y = jnp.concatenate([ya, yb, yc], axis=-1)
    out = r["xres"][...].reshape(rows, D_MODEL) + jnp.dot(y.astype(BF16), r["w_out"][...],
                                                          preferred_element_type=F32)
    if cfg.final:
        out = out * lax.rsqrt(jnp.mean(out * out, axis=-1, keepdims=True) + 1e-6) * r["final_norm_w"][...]
    r["xo"][...] = out.reshape(r["xo"].shape)


HANDOFF = ("p", "rc", "vc", "kc", "o", "qa", "qb", "kta", "da", "ktb", "db", "bkt", "dc", "khr", "u0", "rb")


def _interleave(*stages):
    stages = list(stages)
    while stages:
        for stage in list(stages):
            try:
                next(stage)
            except StopIteration:
                stages.remove(stage)


def _layer_kernel(names, cfg, nb, nt, *refs):
    r = dict(zip(names, refs))
    nseq = cfg.rows // cfg.tseq
    g = pl.program_id(0)
    last = nb * nt - 1
    t_a = jnp.minimum(g, last) % nt
    t_b = jnp.maximum(g - cfg.lag, 0) % nt

    if cfg.lag:
        @pl.when(g == 0)
        def _first():
            for n in HANDOFF:
                r[n + "1"][...] = jnp.zeros(r[n + "1"].shape, r[n + "1"].dtype)

    if not cfg.has_state:
        @pl.when(t_a == 0)
        def _init_carry():
            r["pct"][...] = jnp.zeros(r["pct"].shape, F32)

    @pl.when(t_b == 0)
    def _init_states():
        if cfg.has_state:
            r["sa"][...] = r["sa_in"][...]
            r["sb"][...] = r["sb_in"][...]
            for q in range(nseq):
                for h in range(H_C):
                    r["sct"][q, h] = r["sc_in"][q, h].T
        else:
            r["sa"][...] = jnp.zeros(r["sa"].shape, F32)
            r["sb"][...] = jnp.zeros(r["sb"].shape, F32)
            r["sct"][...] = jnp.zeros(r["sct"].shape, F32)

    def view(slot):
        return {**r, **{n: r[n + str(slot)] for n in HANDOFF}}

    if cfg.lag:
        for parity in (0, 1):
            @pl.when(g % 2 == parity)
            def _step(parity=parity):
                _interleave(_stage_a(view(parity), cfg), _stage_b(view(1 - parity), cfg))
    else:
        _interleave(_stage_a(view(0), cfg))
        _interleave(_stage_b(view(0), cfg))

    @pl.when(t_b == nt - 1)
    def _emit():
        for q in range(nseq):
            for h in range(H_C):
                r["sc"][q, h] = r["sct"][q, h].T


def _gla_gates(r, p, cfg):
    hl = r["hgrn_lb"][...]
    e = jnp.exp(hl - jnp.max(hl, axis=0, keepdims=True))
    sm = e / jnp.sum(e, axis=0, keepdims=True)
    cum = sm[0:1]
    for j in range(1, cfg.layer + 1):
        cum = cum + sm[j:j + 1]
    lb = cum - sm[0:1]
    za = p[:, ZA[0]:ZA[1]]
    ls = _logsig(za)
    la = jnp.log(jnp.maximum(lb, TINY))
    bb = jnp.log1p(-lb) + ls
    lae = jnp.maximum(la, bb) + _log1pexp(la - bb)
    g_a = jnp.where(lb > 0.0, lae, ls)
    r["ka"][...] = (1.0 - lb) * jax.nn.sigmoid(-za)
    g_b = _logsig(_dot_hi(p[:, GKL[0]:GKL[1]], r["gk_w2"][...]) + r["gk_b"][...]) * (1.0 / GLA_NORM)
    return g_a, g_b


def _rwkv_gates(r, p, cfg):
    rows, tseq = cfg.rows, cfg.tseq
    pc = jnp.concatenate([p[:, PC_MAIN[0]:PC_MAIN[1]], p[:, PC_TAIL[0]:PC_TAIL[1]]], axis=1)
    xs = r["xs"]
    if cfg.has_state:
        row = lax.broadcasted_iota(jnp.int32, (rows, 1), 0)
        prev = jnp.where((row & (tseq - 1)) == 0, r["shift_rows"][...], pltpu.roll(pc, 1, axis=0))
        r["pct"][...] = pc
    else:
        xs[...] = pltpu.roll(pc, 1, axis=0)
        for q in range(rows // tseq):
            xs[q * tseq:q * tseq + 1, :] = r["pct"][(q + 1) * SUBLANES - 1:(q + 1) * SUBLANES, :]
            last = slice((q + 1) * tseq - SUBLANES, (q + 1) * tseq)
            r["pct"][q * SUBLANES:(q + 1) * SUBLANES, 0:N_MAIN] = p[last, PC_MAIN[0]:PC_MAIN[1]]
            r["pct"][q * SUBLANES:(q + 1) * SUBLANES, N_MAIN:D_SHIFT] = p[last, PC_TAIL[0]:PC_TAIL[1]]
        prev = xs[...]
    xs[...] = pc + (prev - pc) * r["mu"][...]
    kc = xs[:, SH_K[0]:SH_K[1]]
    wlog = -_softplus(-(r["w0"][...] + _dot_hi(jnp.tanh(xs[:, SH_WL[0]:SH_WL[1]]), r["w2"][...]))) - 0.5
    lw = -jnp.exp(wlog)
    av = jax.nn.sigmoid(r["a0"][...] + _dot_hi(xs[:, SH_AL[0]:SH_AL[1]], r["a2"][...]))
    kk = kc * r["k_k"][...]
    kk = kk * lax.rsqrt(jnp.maximum(_segsum(kk * kk, r["m64c"][...]), 1e-24))
    r["rc"][...] = xs[:, SH_R[0]:SH_R[1]]
    r["vc"][...] = xs[:, SH_V[0]:SH_V[1]]
    r["kc"][...] = kc * (1.0 + (av - 1.0) * r["k_a"][...])
    return lw, kk, kk * av


def _block_diag_ones(width, seg):
    i = jnp.arange(width) // seg
    return (i[:, None] == i[None, :]).astype(BF16)


def _run_layer(cfg, nb, nt, x3, state, new_states, params):
    l, rows = cfg.layer, cfg.rows
    nseq = rows // cfg.tseq
    nseq_total = nb * nseq
    assert not cfg.has_state or nt == 1
    x_block = (1, rows, D_MODEL) if cfg.has_state else (nseq, cfg.tseq, D_MODEL)
    ins, specs = [], []

    def add(name, arr, spec):
        ins.append((name, arr))
        specs.append(spec)

    def layer_row(name, arr):
        w = arr.shape[-1]
        add(name, arr.reshape(DEPTH, 1, w), pl.BlockSpec((None, 1, w), lambda g: (l, 0, 0)))

    once = pl.Buffered(1)

    def layer_mat(name, arr):
        add(name, arr, pl.BlockSpec((None,) + arr.shape[1:], lambda g: (l, 0, 0), pipeline_mode=once))

    def whole(name, arr):
        add(name, arr, pl.BlockSpec(arr.shape, lambda g: (0,) * arr.ndim, pipeline_mode=once))

    last = nb * nt - 1
    lag = cfg.lag

    def blk_a(g):
        ga = jnp.minimum(g, last)
        return ga // nt, ga % nt

    def blk_b(g):
        gb = jnp.maximum(g - lag, 0)
        return gb // nt, gb % nt

    add("x", x3, pl.BlockSpec(x_block, lambda g: (*blk_a(g), 0)))
    add("xres", x3, pl.BlockSpec(x_block, lambda g: (*blk_b(g), 0)))
    if cfg.has_state:
        st_a, st_b, st_c, shift_rows = state
        add("shift_rows", shift_rows, pl.BlockSpec((None, rows, D_SHIFT), lambda g: (l, blk_a(g)[0], 0)))
        add("sa_in", st_a, pl.BlockSpec((None, nseq, H_A, DK_A, DV_A), lambda g: (l, blk_b(g)[0], 0, 0, 0)))
        add("sb_in", st_b, pl.BlockSpec((None, nseq, H_B, DK_B, DV_B), lambda g: (l, blk_b(g)[0], 0, 0, 0)))
        add("sc_in", st_c, pl.BlockSpec((None, nseq, H_C, N_C, N_C), lambda g: (l, blk_b(g)[0], 0, 0, 0)))
    layer_row("norm_w", params["norm_w"])
    layer_mat("w_in", params["w_in"])
    whole("hgrn_lb", params["hgrn_lb"])
    layer_row("hgrn_norm_w", params["hgrn_norm_w"])
    layer_mat("gk_w2", params["gla_gk_w2"])
    layer_row("gk_b", params["gla_gk_b"])
    layer_row("gla_norm_w", params["gla_norm_w"])
    layer_row("mu", params["rwkv_mu"])
    layer_row("w0", params["rwkv_w0"])
    layer_mat("w2", params["rwkv_w2"])
    layer_row("a0", params["rwkv_a0"])
    layer_mat("a2", params["rwkv_a2"])
    layer_row("k_k", params["rwkv_k_k"])
    layer_row("k_a", params["rwkv_k_a"])
    layer_row("r_k", params["rwkv_r_k"])
    layer_row("ln_w", params["rwkv_ln_w"])
    layer_row("ln_b", params["rwkv_ln_b"])
    layer_mat("w_out", params["w_out"])
    if cfg.final:
        whole("final_norm_w", params["final_norm_w"].reshape(1, D_MODEL))
    whole("m64a", params["m64a"])
    whole("m96", params["m96"])
    whole("m64c", params["m64c"])

    pct_rows = rows if cfg.has_state else nseq * SUBLANES
    state_dims = ((H_A, DK_A, DV_A), (H_B, DK_B, DV_B), (H_C, N_C, N_C))
    aliases = {}
    for k, (name, arr, dims) in enumerate(zip(("sa_all", "sb_all", "sc_all"), new_states, state_dims)):
        assert arr.shape == (DEPTH, nseq_total) + dims
        aliases[len(ins)] = 1 + k
        add(name, arr, pl.BlockSpec(memory_space=pl.ANY))
    outs = [("xo", jax.ShapeDtypeStruct(x3.shape, F32), pl.BlockSpec(x_block, lambda g: (*blk_b(g), 0)))]
    outs += [(name, jax.ShapeDtypeStruct((DEPTH, nseq_total) + dims, F32),
              pl.BlockSpec((None, nseq) + dims, lambda g: (l, blk_b(g)[0], 0, 0, 0)))
             for name, dims in zip(("sa", "sb", "sc"), state_dims)]
    outs += [("pct", jax.ShapeDtypeStruct((nb * pct_rows, D_SHIFT), F32),
              pl.BlockSpec((pct_rows, D_SHIFT), lambda g: (blk_a(g)[0], 0)))]
    a_w, b_w, c = H_A * DK_A, H_B * DK_B, cfg.chunk
    nch = rows // c
    scratch = [("p", (rows, D_IN), F32), ("xs", (rows, D_SHIFT), F32), ("o", (rows, D_MODEL), F32),
               ("oi", (rows, D_MODEL), F32),
               ("qa", (rows, a_w), F32), ("ka", (rows, a_w), F32), ("qb", (rows, b_w), F32),
               ("rc", (rows, W_C), F32), ("vc", (rows, W_C), F32), ("kc", (rows, W_C), F32),
               ("u0", (rows, W_C), F32), ("u", (rows, W_C), F32),
               ("kta", (nch, a_w, c), BF16), ("da", (a_w, nch), F32),
               ("ktb", (nch, b_w, c), BF16), ("db", (b_w, nch), F32),
               ("bkt", (nch, W_C, 2 * c), BF16), ("dc", (W_C, nch), F32),
               ("khr", (nch, 2 * c, W_C), F32), ("rb", (H_C, rows, rows), BF16),
               ("sct", (nseq, H_C, N_C, N_C), F32)]
    scratch = ([s for s in scratch if s[0] not in HANDOFF]
               + [(n + slot, shape, dt) for n, shape, dt in scratch if n in HANDOFF for slot in "01"[:1 + lag]])
    names = tuple(n for n, _ in ins) + tuple(n for n, _, _ in outs) + tuple(n for n, _, _ in scratch)
    return pl.pallas_call(
        functools.partial(_layer_kernel, names, cfg, nb, nt),
        grid=(nb * nt + lag,),
        in_specs=specs,
        out_specs=[s for _, _, s in outs],
        out_shape=[s for _, s, _ in outs],
        scratch_shapes=[pltpu.VMEM(shape, dt) for _, shape, dt in scratch],
        input_output_aliases=aliases,
        compiler_params=pltpu.CompilerParams(
            dimension_semantics=("arbitrary",),
            vmem_limit_bytes=VMEM_LIMIT_BYTES),
        name=f"layer{l}_{'sample' if cfg.has_state else 'prompt'}",
    )(*[a for _, a in ins])


_MODEL_SHIFT_ORDER = tuple(n for n in _MODEL_COLS if n in _SHIFT_ORDER)


def _to_kernel_cols(w, dtype):
    return jnp.concatenate([w[..., _MODEL_COLS[n][0]:_MODEL_COLS[n][1]].astype(dtype) for n in _COL_ORDER], axis=-1)


def _shift_to_kernel(s):
    base = _MODEL_COLS[_MODEL_SHIFT_ORDER[0]][0]
    return jnp.concatenate([s[..., _MODEL_COLS[n][0] - base:_MODEL_COLS[n][1] - base] for n in _SHIFT_ORDER], axis=-1)


def _shift_to_model(s):
    return jnp.concatenate([s[..., _SHIFT[n][0]:_SHIFT[n][1]] for n in _MODEL_SHIFT_ORDER], axis=-1)


def _make_params(norm_w, w_in, hgrn_lb, hgrn_norm_w, gla_gk_w2, gla_gk_b, gla_norm_w, rwkv_mu, rwkv_w0, rwkv_w2,
                 rwkv_a0, rwkv_a2, rwkv_k_k, rwkv_k_a, rwkv_r_k, rwkv_ln_w, rwkv_ln_b, w_out, final_norm_w):
    return dict(
        norm_w=norm_w, w_in=_to_kernel_cols(w_in, BF16), hgrn_lb=hgrn_lb, hgrn_norm_w=hgrn_norm_w,
        gla_gk_w2=gla_gk_w2, gla_gk_b=gla_gk_b, gla_norm_w=gla_norm_w, rwkv_mu=_shift_to_kernel(rwkv_mu),
        rwkv_w0=rwkv_w0, rwkv_w2=rwkv_w2, rwkv_a0=rwkv_a0, rwkv_a2=rwkv_a2, rwkv_k_k=rwkv_k_k,
        rwkv_k_a=rwkv_k_a, rwkv_r_k=rwkv_r_k, rwkv_ln_w=rwkv_ln_w, rwkv_ln_b=rwkv_ln_b,
        w_out=w_out.astype(BF16), final_norm_w=final_norm_w,
        m64a=_block_diag_ones(W_A, DV_A), m96=_block_diag_ones(W_B, DV_B),
        m64c=_block_diag_ones(W_C, N_C),
    )


def _trunk(x, state, params, rows, tseq, chunk, lag):
    bsz, tlen, _ = x.shape
    has_state = state is not None
    nseq = rows // tseq
    assert rows % tseq == 0 and tseq % chunk == 0 and bsz % nseq == 0 and tlen % tseq == 0
    nb, nt = bsz // nseq, tlen // tseq
    if has_state:
        assert tlen == tseq
        st_a, st_b, st_c, st_s = state
        shift_rows = jnp.pad(_shift_to_kernel(st_s)[:, :, None, :], ((0, 0), (0, 0), (0, tlen - 1), (0, 0)))
        state = (st_a, st_b, st_c, shift_rows.reshape(DEPTH, bsz * tlen, D_SHIFT))
        x3 = x.reshape(nb, rows, D_MODEL)
    else:
        x3 = x
    new_states = tuple(jnp.zeros((DEPTH, bsz) + dims, F32)
                       for dims in ((H_A, DK_A, DV_A), (H_B, DK_B, DV_B), (H_C, N_C, N_C)))
    new_s = []
    for l in range(DEPTH):
        cfg = Cfg(layer=l, rows=rows, tseq=tseq, chunk=chunk, lag=lag, has_state=has_state,
                  final=(l == DEPTH - 1))
        x3, *new_states, pct = _run_layer(cfg, nb, nt, x3, state, tuple(new_states), params)
        if has_state:
            new_s.append(pct.reshape(bsz, tlen, D_SHIFT)[:, tlen - 1])
        else:
            new_s.append(pct.reshape(bsz, SUBLANES, D_SHIFT)[:, SUBLANES - 1])
    return (x3.reshape(bsz, tlen, D_MODEL), *new_states, _shift_to_model(jnp.stack(new_s)))


PROMPT_ROWS = 256
PROMPT_TSEQ = 32
PROMPT_CHUNK = 32
PROMPT_LAG = 1
SAMPLE_ROWS = 64
SAMPLE_LAG = 0


def kernel(x_prompt, x_sample, state_hgrn, state_gla, state_rwkv, state_shift, norm_w, w_in, hgrn_lb, hgrn_norm_w, gla_gk_w2, gla_gk_b, gla_norm_w, rwkv_mu, rwkv_w0, rwkv_w2, rwkv_a0, rwkv_a2, rwkv_k_k, rwkv_k_a, rwkv_r_k, rwkv_ln_w, rwkv_ln_b, w_out, final_norm_w):
    params = _make_params(norm_w, w_in, hgrn_lb, hgrn_norm_w, gla_gk_w2, gla_gk_b, gla_norm_w, rwkv_mu, rwkv_w0,
                          rwkv_w2, rwkv_a0, rwkv_a2, rwkv_k_k, rwkv_k_a, rwkv_r_k, rwkv_ln_w, rwkv_ln_b, w_out,
                          final_norm_w)
    y_p, hgrn_p, gla_p, rwkv_p, shift_p = _trunk(x_prompt, None, params, PROMPT_ROWS, PROMPT_TSEQ, PROMPT_CHUNK,
                                                 PROMPT_LAG)
    tok = x_sample.shape[1]
    y_s, hgrn_s, gla_s, rwkv_s, shift_s = _trunk(
        x_sample, (state_hgrn, state_gla, state_rwkv, state_shift), params, SAMPLE_ROWS, tok, tok, SAMPLE_LAG)
    return (y_p, y_s, hgrn_p, gla_p, rwkv_p, shift_p, hgrn_s, gla_s, rwkv_s, shift_s)
```

```python
import collections
import functools

import jax
import jax.numpy as jnp
from jax import lax
from jax.experimental import pallas as pl
from jax.experimental.pallas import tpu as pltpu

F32 = jnp.float32
BF16 = jnp.bfloat16

D_MODEL = 1024
DEPTH = 4
HEAD = 64
W_A = D_MODEL // 4
H_A = W_A // HEAD
DK_A = HEAD
DV_A = W_A // H_A
W_B = (D_MODEL - W_A) // 2
H_B = 4
DV_B = W_B // H_B
DK_B = DV_B // 2
GLA_LR = 16
GLA_NORM = 16.0
W_C = D_MODEL - W_A - W_B
H_C = W_C // HEAD
N_C = HEAD
DECAY_LR = 32
AAA_LR = 32
GN_EPS = 64e-5
TINY = 1e-30
D_A_IN = 2 * H_A * DK_A + 2 * W_A
D_B_IN = 2 * H_B * DK_B + 2 * W_B + GLA_LR
D_SHIFT = 3 * W_C + DECAY_LR + AAA_LR
D_IN = D_A_IN + D_B_IN + D_SHIFT + W_C


def _ranges(pieces):
    out, pos = {}, 0
    for name, width in pieces:
        out[name] = (pos, pos + width)
        pos += width
    return out


_MODEL_COLS = _ranges((
    ("qa", H_A * DK_A), ("za", H_A * DK_A), ("ia", W_A), ("gate_a", W_A),
    ("qb", H_B * DK_B), ("kb", H_B * DK_B), ("vb", W_B), ("gkl", GLA_LR), ("gate_b", W_B),
    ("r", W_C), ("wl", DECAY_LR), ("kc", W_C), ("vc", W_C), ("al", AAA_LR), ("gate_c", W_C)))
_COL_ORDER = ("qa", "za", "ia", "gate_a", "qb", "kb", "vb", "gate_b", "r", "kc", "vc", "gate_c",
              "gkl", "wl", "al")
_COLS = _ranges(tuple((n, _MODEL_COLS[n][1] - _MODEL_COLS[n][0]) for n in _COL_ORDER))
QA, ZA, IA, GATE_A = _COLS["qa"], _COLS["za"], _COLS["ia"], _COLS["gate_a"]
QB, KB, VB, GATE_B, GKL = _COLS["qb"], _COLS["kb"], _COLS["vb"], _COLS["gate_b"], _COLS["gkl"]
GATE_C = _COLS["gate_c"]
PC_MAIN = (_COLS["r"][0], _COLS["vc"][1])
PC_TAIL = (_COLS["wl"][0], _COLS["al"][1])
_SHIFT_ORDER = ("r", "kc", "vc", "wl", "al")
_SHIFT = _ranges(tuple((n, _MODEL_COLS[n][1] - _MODEL_COLS[n][0]) for n in _SHIFT_ORDER))
SH_R, SH_K, SH_V, SH_WL, SH_AL = (_SHIFT[n] for n in _SHIFT_ORDER)
N_MAIN = PC_MAIN[1] - PC_MAIN[0]
OUT_A = 0
OUT_B = W_A
OUT_C = W_A + W_B

SUBLANES = 8
VMEM_LIMIT_BYTES = 58 * 1024 * 1024

Cfg = collections.namedtuple("Cfg", "layer rows tseq chunk lag has_state final")

NN = (((1,), (0,)), ((), ()))
NT = (((1,), (1,)), ((), ()))


def _dot(a, b):
    return jnp.dot(a.astype(BF16), b.astype(BF16), preferred_element_type=F32)


def _split(a):
    hi = a.astype(BF16)
    return hi, (a - hi.astype(F32)).astype(BF16)


def _dot3s(a_split, b_split, dims):
    (ah, al), (bh, bl) = a_split, b_split
    return (lax.dot_general(ah, bh, dims, preferred_element_type=F32)
            + lax.dot_general(ah, bl, dims, preferred_element_type=F32)
            + lax.dot_general(al, bh, dims, preferred_element_type=F32))


def _dot_nt(a, b):
    return lax.dot_general(a.astype(BF16), b.astype(BF16), NT, preferred_element_type=F32)


def _dot_nt3(a, b):
    return _dot3s(_split(a), _split(b), NT)


def _dot_nn3(a, b):
    return _dot3s(_split(a), _split(b), NN)


def _segsum(x, m):
    return jnp.dot(x.astype(BF16), m, preferred_element_type=F32)


def _log1pexp(x):
    return jnp.log(1.0 + jnp.exp(-jnp.abs(x)))


def _softplus(x):
    return jnp.maximum(x, 0.0) + _log1pexp(x)


def _logsig(x):
    return jnp.minimum(x, 0.0) - _log1pexp(x)


def _silu(x):
    return x * jax.nn.sigmoid(x)


def _chunk_scan(x, chunk, pos, reverse=False):
    n = x.shape[0]
    s = 1
    while s < chunk:
        if reverse:
            x = x + jnp.where(pos < chunk - s, pltpu.roll(x, n - s, axis=0), 0.0)
        else:
            x = x + jnp.where(pos >= s, pltpu.roll(x, s, axis=0), 0.0)
        s *= 2
    return x


def _chunk_row(x, m, chunk, pos):
    rows, w = x.shape
    if chunk % SUBLANES == 0:
        x3 = x.reshape(rows // chunk, chunk, w)
        return jnp.broadcast_to(x3[:, m:m + 1, :], x3.shape).reshape(rows, w)
    out = x
    for d in range(-m, chunk - m):
        if d:
            out = jnp.where(pos == m + d, pltpu.roll(x, d % rows, axis=0), out)
    return out


def _store_chunk_columns(kt_ref, lane0, x, c):
    tile = max(c, SUBLANES)
    packed = 2 * SUBLANES
    for t0 in range(0, x.shape[0], tile):
        if tile % packed == 0:
            xt = x[t0:t0 + tile, :].astype(BF16).T
        else:
            xt = x[t0:t0 + tile, :].T.astype(BF16)
        for j in range(tile // c):
            kt_ref[t0 // c + j, :, lane0:lane0 + c] = xt[:, j * c:(j + 1) * c]


def _store_chunk_decay(d_ref, decay, c):
    for ch in range(decay.shape[0] // c):
        last = (ch + 1) * c - 1
        t0 = last // SUBLANES * SUBLANES
        d_ref[:, ch:ch + 1] = decay[t0:t0 + SUBLANES, :].T[:, last - t0:last - t0 + 1]


def _gla_block_prep(r, q, k, v, g, pos, gla_masks, names, heads, dk, dv, out_base, c):
    qx_name, kt_name, d_name = names
    cum = _chunk_scan(g, c, pos)
    eg = jnp.exp(cum)
    r[qx_name][...] = q * eg
    k_end = k * jnp.exp(_chunk_row(cum, c - 1, c, pos) - cum)
    _store_chunk_columns(r[kt_name], 0, k_end, c)
    _store_chunk_decay(r[d_name], eg, c)

    terms = [(q, k, gla_masks[0])]
    half = 1
    while half < c:
        ref = _chunk_row(cum, half - 1, 2 * half, pos & (2 * half - 1))
        terms.append((q * jnp.exp(jnp.minimum(cum - ref, 0.0)), k * jnp.exp(jnp.minimum(ref - cum, 0.0)),
                      gla_masks[len(terms)]))
        half *= 2
    for h in range(heads):
        ks = slice(h * dk, (h + 1) * dk)
        sc = 0.0
        for qf, kf, mask in terms:
            sc = jnp.where(mask, _dot_nt(qf[:, ks], kf[:, ks]), sc)
        r["o"][:, out_base + h * dv:out_base + (h + 1) * dv] = _dot(sc, v[:, h * dv:(h + 1) * dv])


def _unit_lower_inverses(a_list, eye, same, c):
    if c % SUBLANES:
        w, p = [eye - a for a in a_list], a_list
        n = 2
        while n < c:
            p = [_dot3s(_split(x), _split(x), NN) for x in p]
            w = [x + _dot3s(_split(x), _split(y), NN) for x, y in zip(w, p)]
            n *= 2
            yield
        return w
    nblk = eye.shape[0] // c

    def side_by_side(x):
        out = x[0:c]
        for k in range(1, nblk):
            out = out + x[k * c:(k + 1) * c]
        return out

    def block_diag(x):
        return jnp.where(same, jnp.tile(x, (nblk, 1)), 0.0)

    def block_diag_split(x):
        hi = x.astype(BF16).astype(F32)
        return block_diag(hi).astype(BF16), block_diag(x - hi).astype(BF16)

    eye_w = side_by_side(eye)
    pw = [side_by_side(a) for a in a_list]
    w = [eye_w - x for x in pw]
    pw = [_dot3s(_split(x), _split(a), NN) for x, a in zip(pw, a_list)]
    n = 2
    while n < c:
        yield
        p_bd = [block_diag_split(x) for x in pw]
        n *= 2
        if n < c:
            prod = [_dot3s(_split(jnp.concatenate([x, y], axis=0)), s, NN) for x, y, s in zip(pw, w, p_bd)]
            pw = [z[0:c] for z in prod]
            w = [y + z[c:2 * c] for y, z in zip(w, prod)]
        else:
            w = [y + _dot3s(_split(y), s, NN) for y, s in zip(w, p_bd)]
    return [block_diag(y) for y in w]


def _rwkv_block_prep(r, lw, kk, b, pos, masks, c):
    causal, strict, eye, same = masks
    nch = lw.shape[0] // c
    rr, kmod, v = r["rc"][...], r["kc"][...], r["vc"][...]
    lc = _chunk_scan(lw, c, pos)
    el = jnp.exp(lc)
    inv_cum = jnp.exp(-lc)
    kk_in = kk * jnp.exp(lc - lw)
    b_out = b * inv_cum
    k_out = kmod * inv_cum
    r_in = rr * el
    to_end = jnp.exp(_chunk_row(lc, c - 1, c, pos) - lc)
    _store_chunk_columns(r["bkt"], 0, b * to_end, c)
    _store_chunk_columns(r["bkt"], c, kmod * to_end, c)
    _store_chunk_decay(r["dc"], el, c)
    for ch in range(nch):
        r["khr"][ch, c:2 * c, :] = r_in[ch * c:(ch + 1) * c, :]
    sub = eye.shape[0]
    parts = [(h, slice(r0, r0 + sub), slice(h * N_C, (h + 1) * N_C))
             for h in range(H_C) for r0 in range(0, lw.shape[0], sub)]
    a = [jnp.where(strict, _dot_nt3(kk_in[rs, cs], b_out[rs, cs]), 0.0) for _, rs, cs in parts]
    yield
    w_s = [_split(x) for x in (yield from _unit_lower_inverses(a, eye, same, c))]
    yield
    bv = [_dot(jnp.where(strict, _dot_nt3(kk_in[rs, cs], k_out[rs, cs]), 0.0), v[rs, cs]) for _, rs, cs in parts]
    yield
    for k, (h, rs, cs) in enumerate(parts):
        z = _dot3s(w_s[k], _split(jnp.concatenate([kk_in[rs, cs], bv[k]], axis=1)), NN)
        r["u0"][rs, cs] = z[:, N_C:2 * N_C]
        for j in range(sub // c):
            r["khr"][rs.start // c + j, 0:c, cs] = z[j * c:(j + 1) * c, 0:N_C]
    yield
    for h, rs, cs in parts:
        r["rb"][h] = jnp.where(causal, _dot_nt(r_in[rs, cs], b_out[rs, cs]), 0.0).astype(BF16)
        rk = jnp.where(causal, _dot_nt(r_in[rs, cs], k_out[rs, cs]), 0.0)
        r["o"][rs, OUT_C + h * N_C:OUT_C + (h + 1) * N_C] = _dot(rk, v[rs, cs])


def _stage_a(r, cfg):
    rows, c = cfg.rows, cfg.chunk
    p = r["p"]
    x = r["x"][...].reshape(rows, D_MODEL)
    hn = x * lax.rsqrt(jnp.mean(x * x, axis=-1, keepdims=True) + 1e-6) * r["norm_w"][...]
    p[...] = jnp.dot(hn.astype(BF16), r["w_in"][...], preferred_element_type=F32)
    yield

    pos = lax.broadcasted_iota(jnp.int32, (rows, 1), 0) & (c - 1)
    shift = c.bit_length() - 1
    i = lax.broadcasted_iota(jnp.int32, (rows, rows), 0)
    j = lax.broadcasted_iota(jnp.int32, (rows, rows), 1)
    same = (i >> shift) == (j >> shift)
    masks = (same & (j <= i), same & (j < i), jnp.where(i == j, 1.0, 0.0), same)
    gla_masks = [i == j]
    half = 1
    while half < c:
        lvl = (2 * half).bit_length() - 1
        gla_masks.append(((i >> lvl) == (j >> lvl)) & ((i & half) != 0) & ((j & half) == 0))
        half *= 2

    lw, kk, b = _rwkv_gates(r, p, cfg)
    yield
    yield from _rwkv_block_prep(r, lw, kk, b, pos, masks, c)
    yield
    g_a, g_b = _gla_gates(r, p, cfg)
    yield
    _gla_block_prep(r, p[:, QA[0]:QA[1]] * (DK_A ** -0.5), r["ka"][...], p[:, IA[0]:IA[1]], g_a, pos, gla_masks,
                    ("qa", "kta", "da"), H_A, DK_A, DV_A, OUT_A, c)
    yield
    _gla_block_prep(r, p[:, QB[0]:QB[1]] * (DK_B ** -0.5), p[:, KB[0]:KB[1]], p[:, VB[0]:VB[1]], g_b, pos,
                    gla_masks, ("qb", "ktb", "db"), H_B, DK_B, DV_B, OUT_B, c)


def _stage_b(r, cfg):
    rows, c = cfg.rows, cfg.chunk
    assert cfg.tseq == c
    p = r["p"]

    da, db, dcol = r["da"][...], r["db"][...], r["dc"][...]
    for q in range(rows // c):
        rs = slice(q * c, (q + 1) * c)
        for (qn, kn, d, st, v, base, heads, dk, dv) in (
                ("qa", "kta", da, r["sa"], p[rs, IA[0]:IA[1]], OUT_A, H_A, DK_A, DV_A),
                ("qb", "ktb", db, r["sb"], p[rs, VB[0]:VB[1]], OUT_B, H_B, DK_B, DV_B)):
            qx, kt = r[qn][rs, :], r[kn][q]
            for h in range(heads):
                ks = slice(h * dk, (h + 1) * dk)
                vs = slice(h * dv, (h + 1) * dv)
                s = st[q, h]
                r["oi"][rs, base + h * dv:base + (h + 1) * dv] = _dot(qx[:, ks], s)
                st[q, h] = s * d[ks, q:q + 1] + jnp.dot(kt[ks, :], v[:, vs].astype(BF16),
                                                        preferred_element_type=F32)
        khr, bkt = r["khr"][q], r["bkt"][q]
        u0, v = r["u0"][rs, :], r["vc"][rs, :]
        for h in range(H_C):
            cs = slice(h * N_C, (h + 1) * N_C)
            s = r["sct"][q, h]
            x = _dot(khr[:, cs], s)
            u = -(x[0:c] + u0[:, cs])
            r["u"][rs, cs] = u
            r["oi"][rs, OUT_C + h * N_C:OUT_C + (h + 1) * N_C] = x[c:2 * c]
            r["sct"][q, h] = s * dcol[cs, q:q + 1] + jnp.dot(
                bkt[cs, :], jnp.concatenate([u, v[:, cs]], axis=0).astype(BF16),
                preferred_element_type=F32)
        yield

    r["o"][...] += r["oi"][...]
    for h in range(H_C):
        cs = slice(h * N_C, (h + 1) * N_C)
        r["o"][:, OUT_C + h * N_C:OUT_C + (h + 1) * N_C] += jnp.dot(
            r["rb"][h], r["u"][:, cs].astype(BF16), preferred_element_type=F32)

    yield

    o = r["o"]
    oa = o[:, OUT_A:OUT_A + W_A]
    ya = (oa * lax.rsqrt(_segsum(oa * oa, r["m64a"][...]) * (1.0 / DV_A) + 1e-5)
          * r["hgrn_norm_w"][...] * _silu(p[:, GATE_A[0]:GATE_A[1]]))
    ob = o[:, OUT_B:OUT_B + W_B]
    yb = (ob * lax.rsqrt(_segsum(ob * ob, r["m96"][...]) * (1.0 / DV_B) + 1e-5)
          * r["gla_norm_w"][...] * _silu(p[:, GATE_B[0]:GATE_B[1]]))
    oc = o[:, OUT_C:D_MODEL]
    m64c = r["m64c"][...]
    dc = oc - _segsum(oc, m64c) * (1.0 / N_C)
    ocn = dc * lax.rsqrt(_segsum(dc * dc, m64c) * (1.0 / N_C) + GN_EPS) * r["ln_w"][...] + r["ln_b"][...]
    bonus = _segsum(r["rc"][...] * r["kc"][...] * r["r_k"][...], m64c) * r["vc"][...]
    yc = (ocn + bonus) * _silu(p[:, GATE_C[0]:GATE_C[1]])
    y = jnp.concatenate([ya, yb, yc], axis=-1)
    out = r["xres"][...].reshape(rows, D_MODEL) + jnp.dot(y.astype(BF16), r["w_out"][...],
                                                          preferred_element_type=F32)
    if cfg.final:
        out = out * lax.rsqrt(jnp.mean(out * out, axis=-1, keepdims=True) + 1e-6) * r["final_norm_w"][...]
    r["xo"][...] = out.reshape(r["xo"].shape)


HANDOFF = ("p", "rc", "vc", "kc", "o", "qa", "qb", "kta", "da", "ktb", "db", "bkt", "dc", "khr", "u0", "rb")


def _interleave(*stages):
    stages = list(stages)
    while stages:
        for stage in list(stages):
            try:
                next(stage)
            except StopIteration:
                stages.remove(stage)


def _layer_kernel(names, cfg, nb, nt, *refs):
    r = dict(zip(names, refs))
    nseq = cfg.rows // cfg.tseq
    g = pl.program_id(0)
    last = nb * nt - 1
    t_a = jnp.minimum(g, last) % nt
    t_b = jnp.maximum(g - cfg.lag, 0) % nt

    if cfg.lag:
        @pl.when(g == 0)
        def _first():
            for n in HANDOFF:
                r[n + "1"][...] = jnp.zeros(r[n + "1"].shape, r[n + "1"].dtype)

    if not cfg.has_state:
        @pl.when(t_a == 0)
        def _init_carry():
            r["pct"][...] = jnp.zeros(r["pct"].shape, F32)

    @pl.when(t_b == 0)
    def _init_states():
        if cfg.has_state:
            r["sa"][...] = r["sa_in"][...]
            r["sb"][...] = r["sb_in"][...]
            for q in range(nseq):
                for h in range(H_C):
                    r["sct"][q, h] = r["sc_in"][q, h].T
        else:
            r["sa"][...] = jnp.zeros(r["sa"].shape, F32)
            r["sb"][...] = jnp.zeros(r["sb"].shape, F32)
            r["sct"][...] = jnp.zeros(r["sct"].shape, F32)

    def view(slot):
        return {**r, **{n: r[n + str(slot)] for n in HANDOFF}}

    if cfg.lag:
        for parity in (0, 1):
            @pl.when(g % 2 == parity)
            def _step(parity=parity):
                _interleave(_stage_a(view(parity), cfg), _stage_b(view(1 - parity), cfg))
    else:
        _interleave(_stage_a(view(0), cfg))
        _interleave(_stage_b(view(0), cfg))

    @pl.when(t_b == nt - 1)
    def _emit():
        for q in range(nseq):
            for h in range(H_C):
                r["sc"][q, h] = r["sct"][q, h].T


def _gla_gates(r, p, cfg):
    hl = r["hgrn_lb"][...]
    e = jnp.exp(hl - jnp.max(hl, axis=0, keepdims=True))
    sm = e / jnp.sum(e, axis=0, keepdims=True)
    cum = sm[0:1]
    for j in range(1, cfg.layer + 1):
        cum = cum + sm[j:j + 1]
    lb = cum - sm[0:1]
    za = p[:, ZA[0]:ZA[1]]
    ls = _logsig(za)
    la = jnp.log(jnp.maximum(lb, TINY))
    bb = jnp.log1p(-lb) + ls
    lae = jnp.maximum(la, bb) + _log1pexp(la - bb)
    g_a = jnp.where(lb > 0.0, lae, ls)
    r["ka"][...] = (1.0 - lb) * jax.nn.sigmoid(-za)
    g_b = _logsig(_dot_nn3(p[:, GKL[0]:GKL[1]], r["gk_w2"][...]) + r["gk_b"][...]) * (1.0 / GLA_NORM)
    return g_a, g_b


def _rwkv_gates(r, p, cfg):
    rows, tseq = cfg.rows, cfg.tseq
    pc = jnp.concatenate([p[:, PC_MAIN[0]:PC_MAIN[1]], p[:, PC_TAIL[0]:PC_TAIL[1]]], axis=1)
    xs = r["xs"]
    if cfg.has_state:
        row = lax.broadcasted_iota(jnp.int32, (rows, 1), 0)
        prev = jnp.where((row & (tseq - 1)) == 0, r["shift_rows"][...], pltpu.roll(pc, 1, axis=0))
        r["pct"][...] = pc
    else:
        xs[...] = pltpu.roll(pc, 1, axis=0)
        for q in range(rows // tseq):
            xs[q * tseq:q * tseq + 1, :] = r["pct"][(q + 1) * SUBLANES - 1:(q + 1) * SUBLANES, :]
            last = slice((q + 1) * tseq - SUBLANES, (q + 1) * tseq)
            r["pct"][q * SUBLANES:(q + 1) * SUBLANES, 0:N_MAIN] = p[last, PC_MAIN[0]:PC_MAIN[1]]
            r["pct"][q * SUBLANES:(q + 1) * SUBLANES, N_MAIN:D_SHIFT] = p[last, PC_TAIL[0]:PC_TAIL[1]]
        prev = xs[...]
    xs[...] = pc + (prev - pc) * r["mu"][...]
    kc = xs[:, SH_K[0]:SH_K[1]]
    wlog = -_softplus(-(r["w0"][...] + _dot_nn3(jnp.tanh(xs[:, SH_WL[0]:SH_WL[1]]), r["w2"][...]))) - 0.5
    lw = -jnp.exp(wlog)
    av = jax.nn.sigmoid(r["a0"][...] + _dot_nn3(xs[:, SH_AL[0]:SH_AL[1]], r["a2"][...]))
    kk = kc * r["k_k"][...]
    kk = kk * lax.rsqrt(jnp.maximum(_segsum(kk * kk, r["m64c"][...]), 1e-24))
    r["rc"][...] = xs[:, SH_R[0]:SH_R[1]]
    r["vc"][...] = xs[:, SH_V[0]:SH_V[1]]
    r["kc"][...] = kc * (1.0 + (av - 1.0) * r["k_a"][...])
    return lw, kk, kk * av


def _block_diag_ones(width, seg):
    i = jnp.arange(width) // seg
    return (i[:, None] == i[None, :]).astype(BF16)


def _run_layer(cfg, nb, nt, x3, state, new_states, params):
    l, rows = cfg.layer, cfg.rows
    nseq = rows // cfg.tseq
    nseq_total = nb * nseq
    assert not cfg.has_state or nt == 1
    x_block = (1, rows, D_MODEL) if cfg.has_state else (nseq, cfg.tseq, D_MODEL)
    ins, specs = [], []

    def add(name, arr, spec):
        ins.append((name, arr))
        specs.append(spec)

    def layer_row(name, arr):
        w = arr.shape[-1]
        add(name, arr.reshape(DEPTH, 1, w), pl.BlockSpec((None, 1, w), lambda g: (l, 0, 0)))

    once = pl.Buffered(1)

    def layer_mat(name, arr):
        add(name, arr, pl.BlockSpec((None,) + arr.shape[1:], lambda g: (l, 0, 0), pipeline_mode=once))

    def whole(name, arr):
        add(name, arr, pl.BlockSpec(arr.shape, lambda g: (0,) * arr.ndim, pipeline_mode=once))

    last = nb * nt - 1
    lag = cfg.lag

    def blk_a(g):
        ga = jnp.minimum(g, last)
        return ga // nt, ga % nt

    def blk_b(g):
        gb = jnp.maximum(g - lag, 0)
        return gb // nt, gb % nt

    add("x", x3, pl.BlockSpec(x_block, lambda g: (*blk_a(g), 0)))
    add("xres", x3, pl.BlockSpec(x_block, lambda g: (*blk_b(g), 0)))
    if cfg.has_state:
        st_a, st_b, st_c, shift_rows = state
        add("shift_rows", shift_rows, pl.BlockSpec((None, rows, D_SHIFT), lambda g: (l, blk_a(g)[0], 0)))
        add("sa_in", st_a, pl.BlockSpec((None, nseq, H_A, DK_A, DV_A), lambda g: (l, blk_b(g)[0], 0, 0, 0)))
        add("sb_in", st_b, pl.BlockSpec((None, nseq, H_B, DK_B, DV_B), lambda g: (l, blk_b(g)[0], 0, 0, 0)))
        add("sc_in", st_c, pl.BlockSpec((None, nseq, H_C, N_C, N_C), lambda g: (l, blk_b(g)[0], 0, 0, 0)))
    layer_row("norm_w", params["norm_w"])
    layer_mat("w_in", params["w_in"])
    whole("hgrn_lb", params["hgrn_lb"])
    layer_row("hgrn_norm_w", params["hgrn_norm_w"])
    layer_mat("gk_w2", params["gla_gk_w2"])
    layer_row("gk_b", params["gla_gk_b"])
    layer_row("gla_norm_w", params["gla_norm_w"])
    layer_row("mu", params["rwkv_mu"])
    layer_row("w0", params["rwkv_w0"])
    layer_mat("w2", params["rwkv_w2"])
    layer_row("a0", params["rwkv_a0"])
    layer_mat("a2", params["rwkv_a2"])
    layer_row("k_k", params["rwkv_k_k"])
    layer_row("k_a", params["rwkv_k_a"])
    layer_row("r_k", params["rwkv_r_k"])
    layer_row("ln_w", params["rwkv_ln_w"])
    layer_row("ln_b", params["rwkv_ln_b"])
    layer_mat("w_out", params["w_out"])
    if cfg.final:
        whole("final_norm_w", params["final_norm_w"].reshape(1, D_MODEL))
    whole("m64a", params["m64a"])
    whole("m96", params["m96"])
    whole("m64c", params["m64c"])

    pct_rows = rows if cfg.has_state else nseq * SUBLANES
    state_dims = ((H_A, DK_A, DV_A), (H_B, DK_B, DV_B), (H_C, N_C, N_C))
    aliases = {}
    for k, (name, arr, dims) in enumerate(zip(("sa_all", "sb_all", "sc_all"), new_states, state_dims)):
        assert arr.shape == (DEPTH, nseq_total) + dims
        aliases[len(ins)] = 1 + k
        add(name, arr, pl.BlockSpec(memory_space=pl.ANY))
    outs = [("xo", jax.ShapeDtypeStruct(x3.shape, F32), pl.BlockSpec(x_block, lambda g: (*blk_b(g), 0)))]
    outs += [(name, jax.ShapeDtypeStruct((DEPTH, nseq_total) + dims, F32),
              pl.BlockSpec((None, nseq) + dims, lambda g: (l, blk_b(g)[0], 0, 0, 0)))
             for name, dims in zip(("sa", "sb", "sc"), state_dims)]
    outs += [("pct", jax.ShapeDtypeStruct((nb * pct_rows, D_SHIFT), F32),
              pl.BlockSpec((pct_rows, D_SHIFT), lambda g: (blk_a(g)[0], 0)))]
    a_w, b_w, c = H_A * DK_A, H_B * DK_B, cfg.chunk
    nch = rows // c
    scratch = [("p", (rows, D_IN), F32), ("xs", (rows, D_SHIFT), F32), ("o", (rows, D_MODEL), F32),
               ("oi", (rows, D_MODEL), F32),
               ("qa", (rows, a_w), F32), ("ka", (rows, a_w), F32), ("qb", (rows, b_w), F32),
               ("rc", (rows, W_C), F32), ("vc", (rows, W_C), F32), ("kc", (rows, W_C), F32),
               ("u0", (rows, W_C), F32), ("u", (rows, W_C), F32),
               ("kta", (nch, a_w, c), BF16), ("da", (a_w, nch), F32),
               ("ktb", (nch, b_w, c), BF16), ("db", (b_w, nch), F32),
               ("bkt", (nch, W_C, 2 * c), BF16), ("dc", (W_C, nch), F32),
               ("khr", (nch, 2 * c, W_C), F32), ("rb", (H_C, rows, rows), BF16),
               ("sct", (nseq, H_C, N_C, N_C), F32)]
    scratch = ([s for s in scratch if s[0] not in HANDOFF]
               + [(n + slot, shape, dt) for n, shape, dt in scratch if n in HANDOFF for slot in "01"[:1 + lag]])
    names = tuple(n for n, _ in ins) + tuple(n for n, _, _ in outs) + tuple(n for n, _, _ in scratch)
    return pl.pallas_call(
        functools.partial(_layer_kernel, names, cfg, nb, nt),
        grid=(nb * nt + lag,),
        in_specs=specs,
        out_specs=[s for _, _, s in outs],
        out_shape=[s for _, s, _ in outs],
        scratch_shapes=[pltpu.VMEM(shape, dt) for _, shape, dt in scratch],
        input_output_aliases=aliases,
        compiler_params=pltpu.CompilerParams(
            dimension_semantics=("arbitrary",),
            vmem_limit_bytes=VMEM_LIMIT_BYTES),
        name=f"layer{l}_{'sample' if cfg.has_state else 'prompt'}",
    )(*[a for _, a in ins])


_MODEL_SHIFT_ORDER = tuple(n for n in _MODEL_COLS if n in _SHIFT_ORDER)


def _to_kernel_cols(w, dtype):
    return jnp.concatenate([w[..., _MODEL_COLS[n][0]:_MODEL_COLS[n][1]].astype(dtype) for n in _COL_ORDER], axis=-1)


def _shift_to_kernel(s):
    base = _MODEL_COLS[_MODEL_SHIFT_ORDER[0]][0]
    return jnp.concatenate([s[..., _MODEL_COLS[n][0] - base:_MODEL_COLS[n][1] - base] for n in _SHIFT_ORDER], axis=-1)


def _shift_to_model(s):
    return jnp.concatenate([s[..., _SHIFT[n][0]:_SHIFT[n][1]] for n in _MODEL_SHIFT_ORDER], axis=-1)


def _make_params(norm_w, w_in, hgrn_lb, hgrn_norm_w, gla_gk_w2, gla_gk_b, gla_norm_w, rwkv_mu, rwkv_w0, rwkv_w2,
                 rwkv_a0, rwkv_a2, rwkv_k_k, rwkv_k_a, rwkv_r_k, rwkv_ln_w, rwkv_ln_b, w_out, final_norm_w):
    return dict(
        norm_w=norm_w, w_in=_to_kernel_cols(w_in, BF16), hgrn_lb=hgrn_lb, hgrn_norm_w=hgrn_norm_w,
        gla_gk_w2=gla_gk_w2, gla_gk_b=gla_gk_b, gla_norm_w=gla_norm_w, rwkv_mu=_shift_to_kernel(rwkv_mu),
        rwkv_w0=rwkv_w0, rwkv_w2=rwkv_w2, rwkv_a0=rwkv_a0, rwkv_a2=rwkv_a2, rwkv_k_k=rwkv_k_k,
        rwkv_k_a=rwkv_k_a, rwkv_r_k=rwkv_r_k, rwkv_ln_w=rwkv_ln_w, rwkv_ln_b=rwkv_ln_b,
        w_out=w_out.astype(BF16), final_norm_w=final_norm_w,
        m64a=_block_diag_ones(W_A, DV_A), m96=_block_diag_ones(W_B, DV_B),
        m64c=_block_diag_ones(W_C, N_C),
    )


def _trunk(x, state, params, rows, tseq, chunk, lag):
    bsz, tlen, _ = x.shape
    has_state = state is not None
    nseq = rows // tseq
    assert rows % tseq == 0 and tseq % chunk == 0 and bsz % nseq == 0 and tlen % tseq == 0
    nb, nt = bsz // nseq, tlen // tseq
    if has_state:
        assert tlen == tseq
        st_a, st_b, st_c, st_s = state
        shift_rows = jnp.pad(_shift_to_kernel(st_s)[:, :, None, :], ((0, 0), (0, 0), (0, tlen - 1), (0, 0)))
        state = (st_a, st_b, st_c, shift_rows.reshape(DEPTH, bsz * tlen, D_SHIFT))
        x3 = x.reshape(nb, rows, D_MODEL)
    else:
        x3 = x
    new_states = tuple(jnp.zeros((DEPTH, bsz) + dims, F32)
                       for dims in ((H_A, DK_A, DV_A), (H_B, DK_B, DV_B), (H_C, N_C, N_C)))
    new_s = []
    for l in range(DEPTH):
        cfg = Cfg(layer=l, rows=rows, tseq=tseq, chunk=chunk, lag=lag, has_state=has_state,
                  final=(l == DEPTH - 1))
        x3, *new_states, pct = _run_layer(cfg, nb, nt, x3, state, tuple(new_states), params)
        if has_state:
            new_s.append(pct.reshape(bsz, tlen, D_SHIFT)[:, tlen - 1])
        else:
            new_s.append(pct.reshape(bsz, SUBLANES, D_SHIFT)[:, SUBLANES - 1])
    return (x3.reshape(bsz, tlen, D_MODEL), *new_states, _shift_to_model(jnp.stack(new_s)))


PROMPT_ROWS = 256
PROMPT_TSEQ = 32
PROMPT_CHUNK = 32
PROMPT_LAG = 1
SAMPLE_ROWS = 64
SAMPLE_LAG = 0


def kernel(x_prompt, x_sample, state_hgrn, state_gla, state_rwkv, state_shift, norm_w, w_in, hgrn_lb, hgrn_norm_w, gla_gk_w2, gla_gk_b, gla_norm_w, rwkv_mu, rwkv_w0, rwkv_w2, rwkv_a0, rwkv_a2, rwkv_k_k, rwkv_k_a, rwkv_r_k, rwkv_ln_w, rwkv_ln_b, w_out, final_norm_w):
    params = _make_params(norm_w, w_in, hgrn_lb, hgrn_norm_w, gla_gk_w2, gla_gk_b, gla_norm_w, rwkv_mu, rwkv_w0,
                          rwkv_w2, rwkv_a0, rwkv_a2, rwkv_k_k, rwkv_k_a, rwkv_r_k, rwkv_ln_w, rwkv_ln_b, w_out,
                          final_norm_w)
    y_p, hgrn_p, gla_p, rwkv_p, shift_p = _trunk(x_prompt, None, params, PROMPT_ROWS, PROMPT_TSEQ, PROMPT_CHUNK,
                                                 PROMPT_LAG)
    tok = x_sample.shape[1]
    y_s, hgrn_s, gla_s, rwkv_s, shift_s = _trunk(
        x_sample, (state_hgrn, state_gla, state_rwkv, state_shift), params, SAMPLE_ROWS, tok, tok, SAMPLE_LAG)
    return (y_p, y_s, hgrn_p, gla_p, rwkv_p, shift_p, hgrn_s, gla_s, rwkv_s, shift_s)
```

```python
import collections
import functools

import jax
import jax.numpy as jnp
from jax import lax
from jax.experimental import pallas as pl
from jax.experimental.pallas import tpu as pltpu

F32 = jnp.float32
BF16 = jnp.bfloat16

D_MODEL = 1024
DEPTH = 4
HEAD = 64
W_A = D_MODEL // 4
H_A = W_A // HEAD
DK_A = HEAD
DV_A = W_A // H_A
W_B = (D_MODEL - W_A) // 2
H_B = 4
DV_B = W_B // H_B
DK_B = DV_B // 2
GLA_LR = 16
GLA_NORM = 16.0
W_C = D_MODEL - W_A - W_B
H_C = W_C // HEAD
N_C = HEAD
DECAY_LR = 32
AAA_LR = 32
GN_EPS = 64e-5
TINY = 1e-30
D_A_IN = 2 * H_A * DK_A + 2 * W_A
D_B_IN = 2 * H_B * DK_B + 2 * W_B + GLA_LR
D_SHIFT = 3 * W_C + DECAY_LR + AAA_LR
D_IN = D_A_IN + D_B_IN + D_SHIFT + W_C


def _ranges(pieces):
    out, pos = {}, 0
    for name, width in pieces:
        out[name] = (pos, pos + width)
        pos += width
    return out


_MODEL_COLS = _ranges((
    ("qa", H_A * DK_A), ("za", H_A * DK_A), ("ia", W_A), ("gate_a", W_A),
    ("qb", H_B * DK_B), ("kb", H_B * DK_B), ("vb", W_B), ("gkl", GLA_LR), ("gate_b", W_B),
    ("r", W_C), ("wl", DECAY_LR), ("kc", W_C), ("vc", W_C), ("al", AAA_LR), ("gate_c", W_C)))
_COL_ORDER = ("qa", "za", "ia", "gate_a", "qb", "kb", "vb", "gate_b", "r", "kc", "vc", "gate_c",
              "gkl", "wl", "al")
_COLS = _ranges(tuple((n, _MODEL_COLS[n][1] - _MODEL_COLS[n][0]) for n in _COL_ORDER))
QA, ZA, IA, GATE_A = _COLS["qa"], _COLS["za"], _COLS["ia"], _COLS["gate_a"]
QB, KB, VB, GATE_B, GKL = _COLS["qb"], _COLS["kb"], _COLS["vb"], _COLS["gate_b"], _COLS["gkl"]
GATE_C = _COLS["gate_c"]
PC_MAIN = (_COLS["r"][0], _COLS["vc"][1])
PC_TAIL = (_COLS["wl"][0], _COLS["al"][1])
_SHIFT_ORDER = ("r", "kc", "vc", "wl", "al")
_SHIFT = _ranges(tuple((n, _MODEL_COLS[n][1] - _MODEL_COLS[n][0]) for n in _SHIFT_ORDER))
SH_R, SH_K, SH_V, SH_WL, SH_AL = (_SHIFT[n] for n in _SHIFT_ORDER)
N_MAIN = PC_MAIN[1] - PC_MAIN[0]
OUT_A = 0
OUT_B = W_A
OUT_C = W_A + W_B

SUBLANES = 8
VMEM_LIMIT_BYTES = 58 * 1024 * 1024

Cfg = collections.namedtuple("Cfg", "layer rows tseq chunk lag has_state final")

NN = (((1,), (0,)), ((), ()))
NT = (((1,), (1,)), ((), ()))


def _dot(a, b):
    return jnp.dot(a.astype(BF16), b.astype(BF16), preferred_element_type=F32)


def _split(a):
    hi = a.astype(BF16)
    return hi, (a - hi.astype(F32)).astype(BF16)


def _dot3s(a_split, b_split, dims):
    (ah, al), (bh, bl) = a_split, b_split
    return (lax.dot_general(ah, bh, dims, preferred_element_type=F32)
            + lax.dot_general(ah, bl, dims, preferred_element_type=F32)
            + lax.dot_general(al, bh, dims, preferred_element_type=F32))


def _dot_nt(a, b):
    return lax.dot_general(a.astype(BF16), b.astype(BF16), NT, preferred_element_type=F32)


def _dot_nt3(a, b):
    return _dot3s(_split(a), _split(b), NT)


def _dot_nn3(a, b):
    return _dot3s(_split(a), _split(b), NN)


def _segsum(x, m):
    return jnp.dot(x.astype(BF16), m, preferred_element_type=F32)


def _log1pexp(x):
    return jnp.log(1.0 + jnp.exp(-jnp.abs(x)))


def _softplus(x):
    return jnp.maximum(x, 0.0) + _log1pexp(x)


def _logsig(x):
    return jnp.minimum(x, 0.0) - _log1pexp(x)


def _silu(x):
    return x * jax.nn.sigmoid(x)


def _chunk_scan(x, chunk, pos, reverse=False):
    n = x.shape[0]
    s = 1
    while s < chunk:
        if reverse:
            x = x + jnp.where(pos < chunk - s, pltpu.roll(x, n - s, axis=0), 0.0)
        else:
            x = x + jnp.where(pos >= s, pltpu.roll(x, s, axis=0), 0.0)
        s *= 2
    return x


def _chunk_row(x, m, chunk, pos):
    rows, w = x.shape
    if chunk % SUBLANES == 0:
        x3 = x.reshape(rows // chunk, chunk, w)
        return jnp.broadcast_to(x3[:, m:m + 1, :], x3.shape).reshape(rows, w)
    out = x
    for d in range(-m, chunk - m):
        if d:
            out = jnp.where(pos == m + d, pltpu.roll(x, d % rows, axis=0), out)
    return out


def _store_chunk_columns(kt_ref, lane0, x, c):
    tile = max(c, SUBLANES)
    packed = 2 * SUBLANES
    for t0 in range(0, x.shape[0], tile):
        if tile % packed == 0:
            xt = x[t0:t0 + tile, :].astype(BF16).T
        else:
            xt = x[t0:t0 + tile, :].T.astype(BF16)
        for j in range(tile // c):
            kt_ref[t0 // c + j, :, lane0:lane0 + c] = xt[:, j * c:(j + 1) * c]


def _store_chunk_decay(d_ref, decay, c):
    for ch in range(decay.shape[0] // c):
        last = (ch + 1) * c - 1
        t0 = last // SUBLANES * SUBLANES
        d_ref[:, ch:ch + 1] = decay[t0:t0 + SUBLANES, :].T[:, last - t0:last - t0 + 1]


def _gla_block_prep(r, q, k, v, g, pos, gla_masks, names, heads, dk, dv, out_base, c):
    qx_name, kt_name, d_name = names
    cum = _chunk_scan(g, c, pos)
    eg = jnp.exp(cum)
    r[qx_name][...] = q * eg
    k_end = k * jnp.exp(_chunk_row(cum, c - 1, c, pos) - cum)
    _store_chunk_columns(r[kt_name], 0, k_end, c)
    _store_chunk_decay(r[d_name], eg, c)

    terms = [(q, k, gla_masks[0])]
    half = 1
    while half < c:
        ref = _chunk_row(cum, half - 1, 2 * half, pos & (2 * half - 1))
        decay = jnp.exp(-jnp.abs(cum - ref))
        terms.append((q * decay, k * decay, gla_masks[len(terms)]))
        half *= 2
    for h in range(heads):
        ks = slice(h * dk, (h + 1) * dk)
        sc = 0.0
        for qf, kf, mask in terms:
            sc = jnp.where(mask, _dot_nt(qf[:, ks], kf[:, ks]), sc)
        r["o"][:, out_base + h * dv:out_base + (h + 1) * dv] = _dot(sc, v[:, h * dv:(h + 1) * dv])


def _unit_lower_inverses(a_list, eye, same, c):
    if c % SUBLANES:
        w, p = [eye - a for a in a_list], a_list
        n = 2
        while n < c:
            p = [_dot3s(_split(x), _split(x), NN) for x in p]
            w = [x + _dot3s(_split(x), _split(y), NN) for x, y in zip(w, p)]
            n *= 2
            yield
        return w
    nblk = eye.shape[0] // c

    def side_by_side(x):
        out = x[0:c]
        for k in range(1, nblk):
            out = out + x[k * c:(k + 1) * c]
        return out

    def block_diag(x):
        return jnp.where(same, jnp.tile(x, (nblk, 1)), 0.0)

    def block_diag_split(x):
        hi = x.astype(BF16).astype(F32)
        return block_diag(hi).astype(BF16), block_diag(x - hi).astype(BF16)

    eye_w = side_by_side(eye)
    pw = [side_by_side(a) for a in a_list]
    w = [eye_w - x for x in pw]
    pw = [_dot3s(_split(x), _split(a), NN) for x, a in zip(pw, a_list)]
    n = 2
    while n < c:
        yield
        p_bd = [block_diag_split(x) for x in pw]
        n *= 2
        if n < c:
            prod = [_dot3s(_split(jnp.concatenate([x, y], axis=0)), s, NN) for x, y, s in zip(pw, w, p_bd)]
            pw = [z[0:c] for z in prod]
            w = [y + z[c:2 * c] for y, z in zip(w, prod)]
        else:
            w = [y + _dot3s(_split(y), s, NN) for y, s in zip(w, p_bd)]
    return [block_diag(y) for y in w]


def _rwkv_block_prep(r, lw, kk, b, pos, masks, c):
    causal, strict, eye, same = masks
    nch = lw.shape[0] // c
    rr, kmod, v = r["rc"][...], r["kc"][...], r["vc"][...]
    lc = _chunk_scan(lw, c, pos)
    el = jnp.exp(lc)
    inv_cum = jnp.exp(-lc)
    kk_in = kk * jnp.exp(lc - lw)
    b_out = b * inv_cum
    k_out = kmod * inv_cum
    r_in = rr * el
    to_end = jnp.exp(_chunk_row(lc, c - 1, c, pos) - lc)
    _store_chunk_columns(r["bkt"], 0, b * to_end, c)
    _store_chunk_columns(r["bkt"], c, kmod * to_end, c)
    _store_chunk_decay(r["dc"], el, c)
    for ch in range(nch):
        r["khr"][ch, c:2 * c, :] = r_in[ch * c:(ch + 1) * c, :]
    sub = eye.shape[0]
    parts = [(h, slice(r0, r0 + sub), slice(h * N_C, (h + 1) * N_C))
             for h in range(H_C) for r0 in range(0, lw.shape[0], sub)]
    a = [jnp.where(strict, _dot_nt3(kk_in[rs, cs], b_out[rs, cs]), 0.0) for _, rs, cs in parts]
    yield
    w_s = [_split(x) for x in (yield from _unit_lower_inverses(a, eye, same, c))]
    yield
    bv = [_dot(jnp.where(strict, _dot_nt3(kk_in[rs, cs], k_out[rs, cs]), 0.0), v[rs, cs]) for _, rs, cs in parts]
    yield
    for k, (h, rs, cs) in enumerate(parts):
        z = _dot3s(w_s[k], _split(jnp.concatenate([kk_in[rs, cs], bv[k]], axis=1)), NN)
        r["u0"][rs, cs] = z[:, N_C:2 * N_C]
        for j in range(sub // c):
            r["khr"][rs.start // c + j, 0:c, cs] = z[j * c:(j + 1) * c, 0:N_C]
    yield
    for h, rs, cs in parts:
        r["rb"][h] = jnp.where(causal, _dot_nt(r_in[rs, cs], b_out[rs, cs]), 0.0).astype(BF16)
        rk = jnp.where(causal, _dot_nt(r_in[rs, cs], k_out[rs, cs]), 0.0)
        r["o"][rs, OUT_C + h * N_C:OUT_C + (h + 1) * N_C] = _dot(rk, v[rs, cs])


def _stage_a(r, cfg):
    rows, c = cfg.rows, cfg.chunk
    p = r["p"]
    x = r["x"][...].reshape(rows, D_MODEL)
    hn = x * lax.rsqrt(jnp.mean(x * x, axis=-1, keepdims=True) + 1e-6) * r["norm_w"][...]
    p[...] = jnp.dot(hn.astype(BF16), r["w_in"][...], preferred_element_type=F32)
    yield

    pos = lax.broadcasted_iota(jnp.int32, (rows, 1), 0) & (c - 1)
    shift = c.bit_length() - 1
    i = lax.broadcasted_iota(jnp.int32, (rows, rows), 0)
    j = lax.broadcasted_iota(jnp.int32, (rows, rows), 1)
    same = (i >> shift) == (j >> shift)
    masks = (same & (j <= i), same & (j < i), jnp.where(i == j, 1.0, 0.0), same)
    gla_masks = [i == j]
    half = 1
    while half < c:
        lvl = (2 * half).bit_length() - 1
        gla_masks.append(((i >> lvl) == (j >> lvl)) & ((i & half) != 0) & ((j & half) == 0))
        half *= 2

    lw, kk, b = _rwkv_gates(r, p, cfg)
    yield
    yield from _rwkv_block_prep(r, lw, kk, b, pos, masks, c)
    yield
    g_a, g_b = _gla_gates(r, p, cfg)
    yield
    _gla_block_prep(r, p[:, QA[0]:QA[1]] * (DK_A ** -0.5), r["ka"][...], p[:, IA[0]:IA[1]], g_a, pos, gla_masks,
                    ("qa", "kta", "da"), H_A, DK_A, DV_A, OUT_A, c)
    yield
    _gla_block_prep(r, p[:, QB[0]:QB[1]] * (DK_B ** -0.5), p[:, KB[0]:KB[1]], p[:, VB[0]:VB[1]], g_b, pos,
                    gla_masks, ("qb", "ktb", "db"), H_B, DK_B, DV_B, OUT_B, c)


def _stage_b(r, cfg):
    rows, c = cfg.rows, cfg.chunk
    assert cfg.tseq == c
    p = r["p"]

    da, db, dcol = r["da"][...], r["db"][...], r["dc"][...]
    for q in range(rows // c):
        rs = slice(q * c, (q + 1) * c)
        for (qn, kn, d, st, v, base, heads, dk, dv) in (
                ("qa", "kta", da, r["sa"], p[rs, IA[0]:IA[1]], OUT_A, H_A, DK_A, DV_A),
                ("qb", "ktb", db, r["sb"], p[rs, VB[0]:VB[1]], OUT_B, H_B, DK_B, DV_B)):
            qx, kt = r[qn][rs, :], r[kn][q]
            for h in range(heads):
                ks = slice(h * dk, (h + 1) * dk)
                vs = slice(h * dv, (h + 1) * dv)
                s = st[q, h]
                r["oi"][rs, base + h * dv:base + (h + 1) * dv] = _dot(qx[:, ks], s)
                st[q, h] = s * d[ks, q:q + 1] + jnp.dot(kt[ks, :], v[:, vs].astype(BF16),
                                                        preferred_element_type=F32)
        khr, bkt = r["khr"][q], r["bkt"][q]
        u0, v = r["u0"][rs, :], r["vc"][rs, :]
        for h in range(H_C):
            cs = slice(h * N_C, (h + 1) * N_C)
            s = r["sct"][q, h]
            x = _dot(khr[:, cs], s)
            u = -(x[0:c] + u0[:, cs])
            r["u"][rs, cs] = u
            r["oi"][rs, OUT_C + h * N_C:OUT_C + (h + 1) * N_C] = x[c:2 * c]
            r["sct"][q, h] = s * dcol[cs, q:q + 1] + jnp.dot(
                bkt[cs, :], jnp.concatenate([u, v[:, cs]], axis=0).astype(BF16),
                preferred_element_type=F32)
        yield

    r["o"][...] += r["oi"][...]
    for h in range(H_C):
        cs = slice(h * N_C, (h + 1) * N_C)
        r["o"][:, OUT_C + h * N_C:OUT_C + (h + 1) * N_C] += jnp.dot(
            r["rb"][h], r["u"][:, cs].astype(BF16), preferred_element_type=F32)

    yield

    o = r["o"]
    oa = o[:, OUT_A:OUT_A + W_A]
    ya = (oa * lax.rsqrt(_segsum(oa * oa, r["m64a"][...]) * (1.0 / DV_A) + 1e-5)
          * r["hgrn_norm_w"][...] * _silu(p[:, GATE_A[0]:GATE_A[1]]))
    ob = o[:, OUT_B:OUT_B + W_B]
    yb = (ob * lax.rsqrt(_segsum(ob * ob, r["m96"][...]) * (1.0 / DV_B) + 1e-5)
          * r["gla_norm_w"][...] * _silu(p[:, GATE_B[0]:GATE_B[1]]))
    oc = o[:, OUT_C:D_MODEL]
    m64c = r["m64c"][...]
    dc = oc - _segsum(oc, m64c) * (1.0 / N_C)
    ocn = dc * lax.rsqrt(_segsum(dc * dc, m64c) * (1.0 / N_C) + GN_EPS) * r["ln_w"][...] + r["ln_b"][...]
    bonus = _segsum(r["rc"][...] * r["kc"][...] * r["r_k"][...], m64c) * r["vc"][...]
    yc = (ocn + bonus) * _silu(p[:, GATE_C[0]:GATE_C[1]])
    y = jnp.concatenate([ya, yb, yc], axis=-1)
    out = r["xres"][...].reshape(rows, D_MODEL) + jnp.dot(y.astype(BF16), r["w_out"][...],
                                                          preferred_element_type=F32)
    if cfg.final:
        out = out * lax.rsqrt(jnp.mean(out * out, axis=-1, keepdims=True) + 1e-6) * r["final_norm_w"][...]
    r["xo"][...] = out.reshape(r["xo"].shape)


HANDOFF = ("p", "rc", "vc", "kc", "o", "qa", "qb", "kta", "da", "ktb", "db", "bkt", "dc", "khr", "u0", "rb")


def _interleave(*stages):
    stages = list(stages)
    while stages:
        for stage in list(stages):
            try:
                next(stage)
            except StopIteration:
                stages.remove(stage)


def _layer_kernel(names, cfg, nb, nt, *refs):
    r = dict(zip(names, refs))
    nseq = cfg.rows // cfg.tseq
    g = pl.program_id(0)
    last = nb * nt - 1
    t_a = jnp.minimum(g, last) % nt
    t_b = jnp.maximum(g - cfg.lag, 0) % nt

    if cfg.lag:
        @pl.when(g == 0)
        def _first():
            for n in HANDOFF:
                r[n + "1"][...] = jnp.zeros(r[n + "1"].shape, r[n + "1"].dtype)

    if not cfg.has_state:
        @pl.when(t_a == 0)
        def _init_carry():
            r["pct"][...] = jnp.zeros(r["pct"].shape, F32)

    @pl.when(t_b == 0)
    def _init_states():
        if cfg.has_state:
            r["sa"][...] = r["sa_in"][...]
            r["sb"][...] = r["sb_in"][...]
            for q in range(nseq):
                for h in range(H_C):
                    r["sct"][q, h] = r["sc_in"][q, h].T
        else:
            r["sa"][...] = jnp.zeros(r["sa"].shape, F32)
            r["sb"][...] = jnp.zeros(r["sb"].shape, F32)
            r["sct"][...] = jnp.zeros(r["sct"].shape, F32)

    def view(slot):
        return {**r, **{n: r[n + str(slot)] for n in HANDOFF}}

    if cfg.lag:
        for parity in (0, 1):
            @pl.when(g % 2 == parity)
            def _step(parity=parity):
                _interleave(_stage_a(view(parity), cfg), _stage_b(view(1 - parity), cfg))
    else:
        _interleave(_stage_a(view(0), cfg))
        _interleave(_stage_b(view(0), cfg))

    @pl.when(t_b == nt - 1)
    def _emit():
        for q in range(nseq):
            for h in range(H_C):
                r["sc"][q, h] = r["sct"][q, h].T


def _gla_gates(r, p, cfg):
    hl = r["hgrn_lb"][...]
    e = jnp.exp(hl - jnp.max(hl, axis=0, keepdims=True))
    sm = e / jnp.sum(e, axis=0, keepdims=True)
    cum = sm[0:1]
    for j in range(1, cfg.layer + 1):
        cum = cum + sm[j:j + 1]
    lb = cum - sm[0:1]
    za = p[:, ZA[0]:ZA[1]]
    ls = _logsig(za)
    la = jnp.log(jnp.maximum(lb, TINY))
    bb = jnp.log1p(-lb) + ls
    lae = jnp.maximum(la, bb) + _log1pexp(la - bb)
    g_a = jnp.where(lb > 0.0, lae, ls)
    r["ka"][...] = (1.0 - lb) * jax.nn.sigmoid(-za)
    g_b = _logsig(_dot_nn3(p[:, GKL[0]:GKL[1]], r["gk_w2"][...]) + r["gk_b"][...]) * (1.0 / GLA_NORM)
    return g_a, g_b


def _rwkv_gates(r, p, cfg):
    rows, tseq = cfg.rows, cfg.tseq
    pc = jnp.concatenate([p[:, PC_MAIN[0]:PC_MAIN[1]], p[:, PC_TAIL[0]:PC_TAIL[1]]], axis=1)
    xs = r["xs"]
    if cfg.has_state:
        row = lax.broadcasted_iota(jnp.int32, (rows, 1), 0)
        prev = jnp.where((row & (tseq - 1)) == 0, r["shift_rows"][...], pltpu.roll(pc, 1, axis=0))
        r["pct"][...] = pc
    else:
        xs[...] = pltpu.roll(pc, 1, axis=0)
        for q in range(rows // tseq):
            xs[q * tseq:q * tseq + 1, :] = r["pct"][(q + 1) * SUBLANES - 1:(q + 1) * SUBLANES, :]
            last = slice((q + 1) * tseq - SUBLANES, (q + 1) * tseq)
            r["pct"][q * SUBLANES:(q + 1) * SUBLANES, 0:N_MAIN] = p[last, PC_MAIN[0]:PC_MAIN[1]]
            r["pct"][q * SUBLANES:(q + 1) * SUBLANES, N_MAIN:D_SHIFT] = p[last, PC_TAIL[0]:PC_TAIL[1]]
        prev = xs[...]
    xs[...] = pc + (prev - pc) * r["mu"][...]
    kc = xs[:, SH_K[0]:SH_K[1]]
    wlog = -_softplus(-(r["w0"][...] + _dot_nn3(jnp.tanh(xs[:, SH_WL[0]:SH_WL[1]]), r["w2"][...]))) - 0.5
    lw = -jnp.exp(wlog)
    av = jax.nn.sigmoid(r["a0"][...] + _dot_nn3(xs[:, SH_AL[0]:SH_AL[1]], r["a2"][...]))
    kk = kc * r["k_k"][...]
    kk = kk * lax.rsqrt(jnp.maximum(_segsum(kk * kk, r["m64c"][...]), 1e-24))
    r["rc"][...] = xs[:, SH_R[0]:SH_R[1]]
    r["vc"][...] = xs[:, SH_V[0]:SH_V[1]]
    r["kc"][...] = kc * (1.0 + (av - 1.0) * r["k_a"][...])
    return lw, kk, kk * av


def _block_diag_ones(width, seg):
    i = jnp.arange(width) // seg
    return (i[:, None] == i[None, :]).astype(BF16)


def _run_layer(cfg, nb, nt, x3, state, new_states, params):
    l, rows = cfg.layer, cfg.rows
    nseq = rows // cfg.tseq
    nseq_total = nb * nseq
    assert not cfg.has_state or nt == 1
    x_block = (1, rows, D_MODEL) if cfg.has_state else (nseq, cfg.tseq, D_MODEL)
    ins, specs = [], []

    def add(name, arr, spec):
        ins.append((name, arr))
        specs.append(spec)

    def layer_row(name, arr):
        w = arr.shape[-1]
        add(name, arr.reshape(DEPTH, 1, w), pl.BlockSpec((None, 1, w), lambda g: (l, 0, 0)))

    once = pl.Buffered(1)

    def layer_mat(name, arr):
        add(name, arr, pl.BlockSpec((None,) + arr.shape[1:], lambda g: (l, 0, 0), pipeline_mode=once))

    def whole(name, arr):
        add(name, arr, pl.BlockSpec(arr.shape, lambda g: (0,) * arr.ndim, pipeline_mode=once))

    last = nb * nt - 1
    lag = cfg.lag

    def blk_a(g):
        ga = jnp.minimum(g, last)
        return ga // nt, ga % nt

    def blk_b(g):
        gb = jnp.maximum(g - lag, 0)
        return gb // nt, gb % nt

    add("x", x3, pl.BlockSpec(x_block, lambda g: (*blk_a(g), 0)))
    add("xres", x3, pl.BlockSpec(x_block, lambda g: (*blk_b(g), 0)))
    if cfg.has_state:
        st_a, st_b, st_c, shift_rows = state
        add("shift_rows", shift_rows, pl.BlockSpec((None, rows, D_SHIFT), lambda g: (l, blk_a(g)[0], 0)))
        add("sa_in", st_a, pl.BlockSpec((None, nseq, H_A, DK_A, DV_A), lambda g: (l, blk_b(g)[0], 0, 0, 0)))
        add("sb_in", st_b, pl.BlockSpec((None, nseq, H_B, DK_B, DV_B), lambda g: (l, blk_b(g)[0], 0, 0, 0)))
        add("sc_in", st_c, pl.BlockSpec((None, nseq, H_C, N_C, N_C), lambda g: (l, blk_b(g)[0], 0, 0, 0)))
    layer_row("norm_w", params["norm_w"])
    layer_mat("w_in", params["w_in"])
    whole("hgrn_lb", params["hgrn_lb"])
    layer_row("hgrn_norm_w", params["hgrn_norm_w"])
    layer_mat("gk_w2", params["gla_gk_w2"])
    layer_row("gk_b", params["gla_gk_b"])
    layer_row("gla_norm_w", params["gla_norm_w"])
    layer_row("mu", params["rwkv_mu"])
    layer_row("w0", params["rwkv_w0"])
    layer_mat("w2", params["rwkv_w2"])
    layer_row("a0", params["rwkv_a0"])
    layer_mat("a2", params["rwkv_a2"])
    layer_row("k_k", params["rwkv_k_k"])
    layer_row("k_a", params["rwkv_k_a"])
    layer_row("r_k", params["rwkv_r_k"])
    layer_row("ln_w", params["rwkv_ln_w"])
    layer_row("ln_b", params["rwkv_ln_b"])
    layer_mat("w_out", params["w_out"])
    if cfg.final:
        whole("final_norm_w", params["final_norm_w"].reshape(1, D_MODEL))
    whole("m64a", params["m64a"])
    whole("m96", params["m96"])
    whole("m64c", params["m64c"])

    pct_rows = rows if cfg.has_state else nseq * SUBLANES
    state_dims = ((H_A, DK_A, DV_A), (H_B, DK_B, DV_B), (H_C, N_C, N_C))
    aliases = {}
    for k, (name, arr, dims) in enumerate(zip(("sa_all", "sb_all", "sc_all"), new_states, state_dims)):
        assert arr.shape == (DEPTH, nseq_total) + dims
        aliases[len(ins)] = 1 + k
        add(name, arr, pl.BlockSpec(memory_space=pl.ANY))
    outs = [("xo", jax.ShapeDtypeStruct(x3.shape, F32), pl.BlockSpec(x_block, lambda g: (*blk_b(g), 0)))]
    outs += [(name, jax.ShapeDtypeStruct((DEPTH, nseq_total) + dims, F32),
              pl.BlockSpec((None, nseq) + dims, lambda g: (l, blk_b(g)[0], 0, 0, 0)))
             for name, dims in zip(("sa", "sb", "sc"), state_dims)]
    outs += [("pct", jax.ShapeDtypeStruct((nb * pct_rows, D_SHIFT), F32),
              pl.BlockSpec((pct_rows, D_SHIFT), lambda g: (blk_a(g)[0], 0)))]
    a_w, b_w, c = H_A * DK_A, H_B * DK_B, cfg.chunk
    nch = rows // c
    scratch = [("p", (rows, D_IN), F32), ("xs", (rows, D_SHIFT), F32), ("o", (rows, D_MODEL), F32),
               ("oi", (rows, D_MODEL), F32),
               ("qa", (rows, a_w), F32), ("ka", (rows, a_w), F32), ("qb", (rows, b_w), F32),
               ("rc", (rows, W_C), F32), ("vc", (rows, W_C), F32), ("kc", (rows, W_C), F32),
               ("u0", (rows, W_C), F32), ("u", (rows, W_C), F32),
               ("kta", (nch, a_w, c), BF16), ("da", (a_w, nch), F32),
               ("ktb", (nch, b_w, c), BF16), ("db", (b_w, nch), F32),
               ("bkt", (nch, W_C, 2 * c), BF16), ("dc", (W_C, nch), F32),
               ("khr", (nch, 2 * c, W_C), F32), ("rb", (H_C, rows, rows), BF16),
               ("sct", (nseq, H_C, N_C, N_C), F32)]
    scratch = ([s for s in scratch if s[0] not in HANDOFF]
               + [(n + slot, shape, dt) for n, shape, dt in scratch if n in HANDOFF for slot in "01"[:1 + lag]])
    names = tuple(n for n, _ in ins) + tuple(n for n, _, _ in outs) + tuple(n for n, _, _ in scratch)
    return pl.pallas_call(
        functools.partial(_layer_kernel, names, cfg, nb, nt),
        grid=(nb * nt + lag,),
        in_specs=specs,
        out_specs=[s for _, _, s in outs],
        out_shape=[s for _, s, _ in outs],
        scratch_shapes=[pltpu.VMEM(shape, dt) for _, shape, dt in scratch],
        input_output_aliases=aliases,
        compiler_params=pltpu.CompilerParams(
            dimension_semantics=("arbitrary",),
            vmem_limit_bytes=VMEM_LIMIT_BYTES),
        name=f"layer{l}_{'sample' if cfg.has_state else 'prompt'}",
    )(*[a for _, a in ins])


_MODEL_SHIFT_ORDER = tuple(n for n in _MODEL_COLS if n in _SHIFT_ORDER)


def _to_kernel_cols(w, dtype):
    return jnp.concatenate([w[..., _MODEL_COLS[n][0]:_MODEL_COLS[n][1]].astype(dtype) for n in _COL_ORDER], axis=-1)


def _shift_to_kernel(s):
    base = _MODEL_COLS[_MODEL_SHIFT_ORDER[0]][0]
    return jnp.concatenate([s[..., _MODEL_COLS[n][0] - base:_MODEL_COLS[n][1] - base] for n in _SHIFT_ORDER], axis=-1)


def _shift_to_model(s):
    return jnp.concatenate([s[..., _SHIFT[n][0]:_SHIFT[n][1]] for n in _MODEL_SHIFT_ORDER], axis=-1)


def _make_params(norm_w, w_in, hgrn_lb, hgrn_norm_w, gla_gk_w2, gla_gk_b, gla_norm_w, rwkv_mu, rwkv_w0, rwkv_w2,
                 rwkv_a0, rwkv_a2, rwkv_k_k, rwkv_k_a, rwkv_r_k, rwkv_ln_w, rwkv_ln_b, w_out, final_norm_w):
    return dict(
        norm_w=norm_w, w_in=_to_kernel_cols(w_in, BF16), hgrn_lb=hgrn_lb, hgrn_norm_w=hgrn_norm_w,
        gla_gk_w2=gla_gk_w2, gla_gk_b=gla_gk_b, gla_norm_w=gla_norm_w, rwkv_mu=_shift_to_kernel(rwkv_mu),
        rwkv_w0=rwkv_w0, rwkv_w2=rwkv_w2, rwkv_a0=rwkv_a0, rwkv_a2=rwkv_a2, rwkv_k_k=rwkv_k_k,
        rwkv_k_a=rwkv_k_a, rwkv_r_k=rwkv_r_k, rwkv_ln_w=rwkv_ln_w, rwkv_ln_b=rwkv_ln_b,
        w_out=w_out.astype(BF16), final_norm_w=final_norm_w,
        m64a=_block_diag_ones(W_A, DV_A), m96=_block_diag_ones(W_B, DV_B),
        m64c=_block_diag_ones(W_C, N_C),
    )


def _trunk(x, state, params, rows, tseq, chunk, lag):
    bsz, tlen, _ = x.shape
    has_state = state is not None
    nseq = rows // tseq
    assert rows % tseq == 0 and tseq % chunk == 0 and bsz % nseq == 0 and tlen % tseq == 0
    nb, nt = bsz // nseq, tlen // tseq
    if has_state:
        assert tlen == tseq
        st_a, st_b, st_c, st_s = state
        shift_rows = jnp.pad(_shift_to_kernel(st_s)[:, :, None, :], ((0, 0), (0, 0), (0, tlen - 1), (0, 0)))
        state = (st_a, st_b, st_c, shift_rows.reshape(DEPTH, bsz * tlen, D_SHIFT))
        x3 = x.reshape(nb, rows, D_MODEL)
    else:
        x3 = x
    new_states = tuple(jnp.zeros((DEPTH, bsz) + dims, F32)
                       for dims in ((H_A, DK_A, DV_A), (H_B, DK_B, DV_B), (H_C, N_C, N_C)))
    new_s = []
    for l in range(DEPTH):
        cfg = Cfg(layer=l, rows=rows, tseq=tseq, chunk=chunk, lag=lag, has_state=has_state,
                  final=(l == DEPTH - 1))
        x3, *new_states, pct = _run_layer(cfg, nb, nt, x3, state, tuple(new_states), params)
        if has_state:
            new_s.append(pct.reshape(bsz, tlen, D_SHIFT)[:, tlen - 1])
        else:
            new_s.append(pct.reshape(bsz, SUBLANES, D_SHIFT)[:, SUBLANES - 1])
    return (x3.reshape(bsz, tlen, D_MODEL), *new_states, _shift_to_model(jnp.stack(new_s)))


PROMPT_ROWS = 256
PROMPT_TSEQ = 32
PROMPT_CHUNK = 32
PROMPT_LAG = 1
SAMPLE_ROWS = 64
SAMPLE_LAG = 0


def kernel(x_prompt, x_sample, state_hgrn, state_gla, state_rwkv, state_shift, norm_w, w_in, hgrn_lb, hgrn_norm_w, gla_gk_w2, gla_gk_b, gla_norm_w, rwkv_mu, rwkv_w0, rwkv_w2, rwkv_a0, rwkv_a2, rwkv_k_k, rwkv_k_a, rwkv_r_k, rwkv_ln_w, rwkv_ln_b, w_out, final_norm_w):
    params = _make_params(norm_w, w_in, hgrn_lb, hgrn_norm_w, gla_gk_w2, gla_gk_b, gla_norm_w, rwkv_mu, rwkv_w0,
                          rwkv_w2, rwkv_a0, rwkv_a2, rwkv_k_k, rwkv_k_a, rwkv_r_k, rwkv_ln_w, rwkv_ln_b, w_out,
                          final_norm_w)
    y_p, hgrn_p, gla_p, rwkv_p, shift_p = _trunk(x_prompt, None, params, PROMPT_ROWS, PROMPT_TSEQ, PROMPT_CHUNK,
                                                 PROMPT_LAG)
    tok = x_sample.shape[1]
    y_s, hgrn_s, gla_s, rwkv_s, shift_s = _trunk(
        x_sample, (state_hgrn, state_gla, state_rwkv, state_shift), params, SAMPLE_ROWS, tok, tok, SAMPLE_LAG)
    return (y_p, y_s, hgrn_p, gla_p, rwkv_p, shift_p, hgrn_s, gla_s, rwkv_s, shift_s)
```

```python
import collections
import functools

import jax
import jax.numpy as jnp
from jax import lax
from jax.experimental import pallas as pl
from jax.experimental.pallas import tpu as pltpu

F32 = jnp.float32
BF16 = jnp.bfloat16

D_MODEL = 1024
DEPTH = 4
HEAD = 64
W_A = D_MODEL // 4
H_A = W_A // HEAD
DK_A = HEAD
DV_A = W_A // H_A
W_B = (D_MODEL - W_A) // 2
H_B = 4
DV_B = W_B // H_B
DK_B = DV_B // 2
GLA_LR = 16
GLA_NORM = 16.0
W_C = D_MODEL - W_A - W_B
H_C = W_C // HEAD
N_C = HEAD
DECAY_LR = 32
AAA_LR = 32
GN_EPS = 64e-5
TINY = 1e-30
D_A_IN = 2 * H_A * DK_A + 2 * W_A
D_B_IN = 2 * H_B * DK_B + 2 * W_B + GLA_LR
D_SHIFT = 3 * W_C + DECAY_LR + AAA_LR
D_IN = D_A_IN + D_B_IN + D_SHIFT + W_C


def _ranges(pieces):
    out, pos = {}, 0
    for name, width in pieces:
        out[name] = (pos, pos + width)
        pos += width
    return out


_MODEL_COLS = _ranges((
    ("qa", H_A * DK_A), ("za", H_A * DK_A), ("ia", W_A), ("gate_a", W_A),
    ("qb", H_B * DK_B), ("kb", H_B * DK_B), ("vb", W_B), ("gkl", GLA_LR), ("gate_b", W_B),
    ("r", W_C), ("wl", DECAY_LR), ("kc", W_C), ("vc", W_C), ("al", AAA_LR), ("gate_c", W_C)))
_COL_ORDER = ("qa", "za", "ia", "gate_a", "qb", "kb", "vb", "gate_b", "r", "kc", "vc", "gate_c",
              "gkl", "wl", "al")
_COLS = _ranges(tuple((n, _MODEL_COLS[n][1] - _MODEL_COLS[n][0]) for n in _COL_ORDER))
QA, ZA, IA, GATE_A = _COLS["qa"], _COLS["za"], _COLS["ia"], _COLS["gate_a"]
QB, KB, VB, GATE_B, GKL = _COLS["qb"], _COLS["kb"], _COLS["vb"], _COLS["gate_b"], _COLS["gkl"]
GATE_C = _COLS["gate_c"]
PC_MAIN = (_COLS["r"][0], _COLS["vc"][1])
PC_TAIL = (_COLS["wl"][0], _COLS["al"][1])
_SHIFT_ORDER = ("r", "kc", "vc", "wl", "al")
_SHIFT = _ranges(tuple((n, _MODEL_COLS[n][1] - _MODEL_COLS[n][0]) for n in _SHIFT_ORDER))
SH_R, SH_K, SH_V, SH_WL, SH_AL = (_SHIFT[n] for n in _SHIFT_ORDER)
N_MAIN = PC_MAIN[1] - PC_MAIN[0]
OUT_A = 0
OUT_B = W_A
OUT_C = W_A + W_B

SUBLANES = 8
VMEM_LIMIT_BYTES = 58 * 1024 * 1024

Cfg = collections.namedtuple("Cfg", "layer rows tseq chunk lag has_state final")

NN = (((1,), (0,)), ((), ()))
NT = (((1,), (1,)), ((), ()))


def _dot(a, b):
    return jnp.dot(a.astype(BF16), b.astype(BF16), preferred_element_type=F32)


def _split(a):
    hi = a.astype(BF16)
    return hi, (a - hi.astype(F32)).astype(BF16)


def _dot3s(a_split, b_split, dims):
    (ah, al), (bh, bl) = a_split, b_split
    return (lax.dot_general(ah, bh, dims, preferred_element_type=F32)
            + lax.dot_general(ah, bl, dims, preferred_element_type=F32)
            + lax.dot_general(al, bh, dims, preferred_element_type=F32))


def _dot_nt(a, b):
    return lax.dot_general(a.astype(BF16), b.astype(BF16), NT, preferred_element_type=F32)


def _dot_nt3(a, b):
    return _dot3s(_split(a), _split(b), NT)


def _dot_nn3(a, b):
    return _dot3s(_split(a), _split(b), NN)


def _segsum(x, m):
    return jnp.dot(x.astype(BF16), m, preferred_element_type=F32)


def _log1pexp(x):
    return jnp.log(1.0 + jnp.exp(-jnp.abs(x)))


def _softplus(x):
    return jnp.maximum(x, 0.0) + _log1pexp(x)


def _logsig(x):
    return jnp.minimum(x, 0.0) - _log1pexp(x)


def _silu(x):
    return x * jax.nn.sigmoid(x)


def _chunk_scan(x, chunk, pos, reverse=False):
    n = x.shape[0]
    s = 1
    while s < chunk:
        if reverse:
            x = x + jnp.where(pos < chunk - s, pltpu.roll(x, n - s, axis=0), 0.0)
        else:
            x = x + jnp.where(pos >= s, pltpu.roll(x, s, axis=0), 0.0)
        s *= 2
    return x


def _chunk_row(x, m, chunk, pos):
    rows, w = x.shape
    if chunk % SUBLANES == 0:
        x3 = x.reshape(rows // chunk, chunk, w)
        return jnp.broadcast_to(x3[:, m:m + 1, :], x3.shape).reshape(rows, w)
    out = x
    for d in range(-m, chunk - m):
        if d:
            out = jnp.where(pos == m + d, pltpu.roll(x, d % rows, axis=0), out)
    return out


def _store_chunk_columns(kt_ref, lane0, x, c):
    tile = max(c, SUBLANES)
    packed = 2 * SUBLANES
    for t0 in range(0, x.shape[0], tile):
        if tile % packed == 0:
            xt = x[t0:t0 + tile, :].astype(BF16).T
        else:
            xt = x[t0:t0 + tile, :].T.astype(BF16)
        for j in range(tile // c):
            kt_ref[t0 // c + j, :, lane0:lane0 + c] = xt[:, j * c:(j + 1) * c]


def _store_chunk_decay(d_ref, decay, c):
    for ch in range(decay.shape[0] // c):
        last = (ch + 1) * c - 1
        t0 = last // SUBLANES * SUBLANES
        d_ref[:, ch:ch + 1] = decay[t0:t0 + SUBLANES, :].T[:, last - t0:last - t0 + 1]


def _gla_block_prep(r, q, k, v, g, pos, gla_masks, names, heads, dk, dv, out_base, c):
    qx_name, kt_name, d_name = names
    cum = _chunk_scan(g, c, pos)
    eg = jnp.exp(cum)
    r[qx_name][...] = q * eg
    k_end = k * jnp.exp(_chunk_row(cum, c - 1, c, pos) - cum)
    _store_chunk_columns(r[kt_name], 0, k_end, c)
    _store_chunk_decay(r[d_name], eg, c)

    terms = [(q, k, gla_masks[0])]
    half = 1
    while half < c:
        ref = _chunk_row(cum, half - 1, 2 * half, pos & (2 * half - 1))
        decay = jnp.exp(-jnp.abs(cum - ref))
        terms.append((q * decay, k * decay, gla_masks[len(terms)]))
        half *= 2
    for h in range(heads):
        ks = slice(h * dk, (h + 1) * dk)
        sc = 0.0
        for qf, kf, mask in terms:
            sc = jnp.where(mask, _dot_nt(qf[:, ks], kf[:, ks]), sc)
        r["o"][:, out_base + h * dv:out_base + (h + 1) * dv] = _dot(sc, v[:, h * dv:(h + 1) * dv])


def _unit_lower_inverses(a_list, eye, same, c):
    if c % SUBLANES:
        w, p = [eye - a for a in a_list], a_list
        n = 2
        while n < c:
            p = [_dot3s(_split(x), _split(x), NN) for x in p]
            w = [x + _dot3s(_split(x), _split(y), NN) for x, y in zip(w, p)]
            n *= 2
            yield
        return w
    nblk = eye.shape[0] // c

    def side_by_side(x):
        out = x[0:c]
        for k in range(1, nblk):
            out = out + x[k * c:(k + 1) * c]
        return out

    def block_diag(x):
        return jnp.where(same, jnp.tile(x, (nblk, 1)), 0.0)

    def block_diag_split(x):
        hi = x.astype(BF16).astype(F32)
        return block_diag(hi).astype(BF16), block_diag(x - hi).astype(BF16)

    eye_w = side_by_side(eye)
    pw = [side_by_side(a) for a in a_list]
    w = [eye_w - x for x in pw]
    pw = [_dot3s(_split(x), _split(a), NN) for x, a in zip(pw, a_list)]
    n = 2
    while n < c:
        yield
        p_bd = [block_diag_split(x) for x in pw]
        n *= 2
        if n < c:
            prod = [_dot3s(_split(jnp.concatenate([x, y], axis=0)), s, NN) for x, y, s in zip(pw, w, p_bd)]
            pw = [z[0:c] for z in prod]
            w = [y + z[c:2 * c] for y, z in zip(w, prod)]
        else:
            w = [y + _dot3s(_split(y), s, NN) for y, s in zip(w, p_bd)]
    return [block_diag(y) for y in w]


def _rwkv_block_prep(r, lw, kk, b, pos, masks, c):
    causal, strict, eye, same = masks
    nch = lw.shape[0] // c
    rr, kmod, v = r["rc"][...], r["kc"][...], r["vc"][...]
    lc = _chunk_scan(lw, c, pos)
    el = jnp.exp(lc)
    inv_cum = jnp.exp(-lc)
    kk_in = kk * jnp.exp(lc - lw)
    b_out = b * inv_cum
    k_out = kmod * inv_cum
    r_in = rr * el
    to_end = jnp.exp(_chunk_row(lc, c - 1, c, pos) - lc)
    _store_chunk_columns(r["bkt"], 0, b * to_end, c)
    _store_chunk_columns(r["bkt"], c, kmod * to_end, c)
    _store_chunk_decay(r["dc"], el, c)
    for ch in range(nch):
        r["khr"][ch, c:2 * c, :] = r_in[ch * c:(ch + 1) * c, :]
    sub = eye.shape[0]
    parts = [(h, slice(r0, r0 + sub), slice(h * N_C, (h + 1) * N_C))
             for h in range(H_C) for r0 in range(0, lw.shape[0], sub)]
    a = [jnp.where(strict, _dot_nt3(kk_in[rs, cs], b_out[rs, cs]), 0.0) for _, rs, cs in parts]
    yield
    w_s = [_split(x) for x in (yield from _unit_lower_inverses(a, eye, same, c))]
    yield
    bv = [_dot(jnp.where(strict, _dot_nt3(kk_in[rs, cs], k_out[rs, cs]), 0.0), v[rs, cs]) for _, rs, cs in parts]
    yield
    for k, (h, rs, cs) in enumerate(parts):
        z = _dot3s(w_s[k], _split(jnp.concatenate([kk_in[rs, cs], bv[k]], axis=1)), NN)
        r["u0"][rs, cs] = z[:, N_C:2 * N_C]
        for j in range(sub // c):
            r["khr"][rs.start // c + j, 0:c, cs] = z[j * c:(j + 1) * c, 0:N_C]
    yield
    for h, rs, cs in parts:
        r["rb"][h] = jnp.where(causal, _dot_nt(r_in[rs, cs], b_out[rs, cs]), 0.0).astype(BF16)
        rk = jnp.where(causal, _dot_nt(r_in[rs, cs], k_out[rs, cs]), 0.0)
        r["o"][rs, OUT_C + h * N_C:OUT_C + (h + 1) * N_C] = _dot(rk, v[rs, cs])


def _stage_a(r, cfg):
    rows, c = cfg.rows, cfg.chunk
    p = r["p"]
    x = r["x"][...].reshape(rows, D_MODEL)
    hn = x * lax.rsqrt(jnp.mean(x * x, axis=-1, keepdims=True) + 1e-6) * r["norm_w"][...]
    p[...] = jnp.dot(hn.astype(BF16), r["w_in"][...], preferred_element_type=F32)
    yield

    pos = lax.broadcasted_iota(jnp.int32, (rows, 1), 0) & (c - 1)
    shift = c.bit_length() - 1
    i = lax.broadcasted_iota(jnp.int32, (rows, rows), 0)
    j = lax.broadcasted_iota(jnp.int32, (rows, rows), 1)
    same = (i >> shift) == (j >> shift)
    masks = (same & (j <= i), same & (j < i), jnp.where(i == j, 1.0, 0.0), same)
    gla_masks = [i == j]
    half = 1
    while half < c:
        lvl = (2 * half).bit_length() - 1
        gla_masks.append(((i >> lvl) == (j >> lvl)) & ((i & half) != 0) & ((j & half) == 0))
        half *= 2

    lw, kk, b = _rwkv_gates(r, p, cfg)
    yield
    yield from _rwkv_block_prep(r, lw, kk, b, pos, masks, c)
    yield
    g_a, g_b = _gla_gates(r, p, cfg)
    yield
    _gla_block_prep(r, p[:, QA[0]:QA[1]] * (DK_A ** -0.5), r["ka"][...], p[:, IA[0]:IA[1]], g_a, pos, gla_masks,
                    ("qa", "kta", "da"), H_A, DK_A, DV_A, OUT_A, c)
    yield
    _gla_block_prep(r, p[:, QB[0]:QB[1]] * (DK_B ** -0.5), p[:, KB[0]:KB[1]], p[:, VB[0]:VB[1]], g_b, pos,
                    gla_masks, ("qb", "ktb", "db"), H_B, DK_B, DV_B, OUT_B, c)


def _stage_b(r, cfg):
    rows, c = cfg.rows, cfg.chunk
    assert cfg.tseq == c
    p = r["p"]

    da, db, dcol = r["da"][...], r["db"][...], r["dc"][...]
    for q in range(rows // c):
        rs = slice(q * c, (q + 1) * c)
        for (qn, kn, d, st, v, base, heads, dk, dv) in (
                ("qa", "kta", da, r["sa"], p[rs, IA[0]:IA[1]], OUT_A, H_A, DK_A, DV_A),
                ("qb", "ktb", db, r["sb"], p[rs, VB[0]:VB[1]], OUT_B, H_B, DK_B, DV_B)):
            qx, kt = r[qn][rs, :], r[kn][q]
            for h in range(heads):
                ks = slice(h * dk, (h + 1) * dk)
                vs = slice(h * dv, (h + 1) * dv)
                s = st[q, h]
                r["oi"][rs, base + h * dv:base + (h + 1) * dv] = _dot(qx[:, ks], s)
                st[q, h] = s * d[ks, q:q + 1] + jnp.dot(kt[ks, :], v[:, vs].astype(BF16),
                                                        preferred_element_type=F32)
        khr, bkt = r["khr"][q], r["bkt"][q]
        u0, v = r["u0"][rs, :], r["vc"][rs, :]
        for h in range(H_C):
            cs = slice(h * N_C, (h + 1) * N_C)
            s = r["sct"][q, h]
            x = _dot(khr[:, cs], s)
            u = -(x[0:c] + u0[:, cs])
            r["u"][rs, cs] = u
            r["oi"][rs, OUT_C + h * N_C:OUT_C + (h + 1) * N_C] = x[c:2 * c]
            r["sct"][q, h] = s * dcol[cs, q:q + 1] + jnp.dot(
                bkt[cs, :], jnp.concatenate([u, v[:, cs]], axis=0).astype(BF16),
                preferred_element_type=F32)
        yield

    r["o"][...] += r["oi"][...]
    for h in range(H_C):
        cs = slice(h * N_C, (h + 1) * N_C)
        r["o"][:, OUT_C + h * N_C:OUT_C + (h + 1) * N_C] += jnp.dot(
            r["rb"][h], r["u"][:, cs].astype(BF16), preferred_element_type=F32)

    yield

    o = r["o"]
    oa = o[:, OUT_A:OUT_A + W_A]
    ya = (oa * lax.rsqrt(_segsum(oa * oa, r["m64a"][...]) * (1.0 / DV_A) + 1e-5)
          * r["hgrn_norm_w"][...] * _silu(p[:, GATE_A[0]:GATE_A[1]]))
    ob = o[:, OUT_B:OUT_B + W_B]
    yb = (ob * lax.rsqrt(_segsum(ob * ob, r["m96"][...]) * (1.0 / DV_B) + 1e-5)
          * r["gla_norm_w"][...] * _silu(p[:, GATE_B[0]:GATE_B[1]]))
    oc = o[:, OUT_C:D_MODEL]
    m64c = r["m64c"][...]
    dc = oc - _segsum(oc, m64c) * (1.0 / N_C)
    ocn = dc * lax.rsqrt(_segsum(dc * dc, m64c) * (1.0 / N_C) + GN_EPS) * r["ln_w"][...] + r["ln_b"][...]
    bonus = _segsum(r["rc"][...] * r["kc"][...] * r["r_k"][...], m64c) * r["vc"][...]
    yc = (ocn + bonus) * _silu(p[:, GATE_C[0]:GATE_C[1]])
    y = jnp.concatenate([ya, yb, yc], axis=-1)
    out = r["xres"][...].reshape(rows, D_MODEL) + jnp.dot(y.astype(BF16), r["w_out"][...],
                                                          preferred_element_type=F32)
    if cfg.final:
        out = out * lax.rsqrt(jnp.mean(out * out, axis=-1, keepdims=True) + 1e-6) * r["final_norm_w"][...]
    r["xo"][...] = out.reshape(r["xo"].shape)


HANDOFF = ("p", "rc", "vc", "kc", "o", "qa", "qb", "kta", "da", "ktb", "db", "bkt", "dc", "khr", "u0", "rb")


def _interleave(*stages):
    stages = list(stages)
    while stages:
        for stage in list(stages):
            try:
                next(stage)
            except StopIteration:
                stages.remove(stage)


def _layer_kernel(names, cfg, nb, nt, *refs):
    r = dict(zip(names, refs))
    nseq = cfg.rows // cfg.tseq
    g = pl.program_id(0)
    last = nb * nt - 1
    t_a = jnp.minimum(g, last) % nt
    t_b = jnp.maximum(g - cfg.lag, 0) % nt

    if cfg.lag:
        @pl.when(g == 0)
        def _first():
            for n in HANDOFF:
                r[n + "1"][...] = jnp.zeros(r[n + "1"].shape, r[n + "1"].dtype)

    if not cfg.has_state:
        @pl.when(t_a == 0)
        def _init_carry():
            r["pct"][...] = jnp.zeros(r["pct"].shape, F32)

    @pl.when(t_b == 0)
    def _init_states():
        if cfg.has_state:
            r["sa"][...] = r["sa_in"][...]
            r["sb"][...] = r["sb_in"][...]
            for q in range(nseq):
                for h in range(H_C):
                    r["sct"][q, h] = r["sc_in"][q, h].T
        else:
            r["sa"][...] = jnp.zeros(r["sa"].shape, F32)
            r["sb"][...] = jnp.zeros(r["sb"].shape, F32)
            r["sct"][...] = jnp.zeros(r["sct"].shape, F32)

    def view(slot):
        return {**r, **{n: r[n + str(slot)] for n in HANDOFF}}

    if cfg.lag:
        for parity in (0, 1):
            @pl.when(g % 2 == parity)
            def _step(parity=parity):
                _interleave(_stage_a(view(parity), cfg), _stage_b(view(1 - parity), cfg))
    else:
        _interleave(_stage_a(view(0), cfg))
        _interleave(_stage_b(view(0), cfg))

    @pl.when(t_b == nt - 1)
    def _emit():
        for q in range(nseq):
            for h in range(H_C):
                r["sc"][q, h] = r["sct"][q, h].T


def _gla_gates(r, p, cfg):
    hl = r["hgrn_lb"][...]
    e = jnp.exp(hl - jnp.max(hl, axis=0, keepdims=True))
    sm = e / jnp.sum(e, axis=0, keepdims=True)
    cum = sm[0:1]
    for j in range(1, cfg.layer + 1):
        cum = cum + sm[j:j + 1]
    lb = cum - sm[0:1]
    za = p[:, ZA[0]:ZA[1]]
    ls = _logsig(za)
    la = jnp.log(jnp.maximum(lb, TINY))
    bb = jnp.log1p(-lb) + ls
    lae = jnp.maximum(la, bb) + _log1pexp(la - bb)
    g_a = jnp.where(lb > 0.0, lae, ls)
    r["ka"][...] = (1.0 - lb) * jax.nn.sigmoid(-za)
    g_b = _logsig(_dot_nn3(p[:, GKL[0]:GKL[1]], r["gk_w2"][...]) + r["gk_b"][...]) * (1.0 / GLA_NORM)
    return g_a, g_b


def _rwkv_gates(r, p, cfg):
    rows, tseq = cfg.rows, cfg.tseq
    pc = jnp.concatenate([p[:, PC_MAIN[0]:PC_MAIN[1]], p[:, PC_TAIL[0]:PC_TAIL[1]]], axis=1)
    xs = r["xs"]
    if cfg.has_state:
        row = lax.broadcasted_iota(jnp.int32, (rows, 1), 0)
        prev = jnp.where((row & (tseq - 1)) == 0, r["shift_rows"][...], pltpu.roll(pc, 1, axis=0))
        r["pct"][...] = pc
    else:
        xs[...] = pltpu.roll(pc, 1, axis=0)
        for q in range(rows // tseq):
            xs[q * tseq:q * tseq + 1, :] = r["pct"][(q + 1) * SUBLANES - 1:(q + 1) * SUBLANES, :]
            last = slice((q + 1) * tseq - SUBLANES, (q + 1) * tseq)
            r["pct"][q * SUBLANES:(q + 1) * SUBLANES, 0:N_MAIN] = p[last, PC_MAIN[0]:PC_MAIN[1]]
            r["pct"][q * SUBLANES:(q + 1) * SUBLANES, N_MAIN:D_SHIFT] = p[last, PC_TAIL[0]:PC_TAIL[1]]
        prev = xs[...]
    xs[...] = pc + (prev - pc) * r["mu"][...]
    kc = xs[:, SH_K[0]:SH_K[1]]
    wlog = -_softplus(-(r["w0"][...] + _dot_nn3(jnp.tanh(xs[:, SH_WL[0]:SH_WL[1]]), r["w2"][...]))) - 0.5
    lw = -jnp.exp(wlog)
    av = jax.nn.sigmoid(r["a0"][...] + _dot_nn3(xs[:, SH_AL[0]:SH_AL[1]], r["a2"][...]))
    kk = kc * r["k_k"][...]
    kk = kk * lax.rsqrt(jnp.maximum(_segsum(kk * kk, r["m64c"][...]), 1e-24))
    r["rc"][...] = xs[:, SH_R[0]:SH_R[1]]
    r["vc"][...] = xs[:, SH_V[0]:SH_V[1]]
    r["kc"][...] = kc * (1.0 + (av - 1.0) * r["k_a"][...])
    return lw, kk, kk * av


def _block_diag_ones(width, seg):
    i = jnp.arange(width) // seg
    return (i[:, None] == i[None, :]).astype(BF16)


def _run_layer(cfg, nb, nt, x3, state, new_states, params):
    l, rows = cfg.layer, cfg.rows
    nseq = rows // cfg.tseq
    nseq_total = nb * nseq
    assert not cfg.has_state or nt == 1
    x_block = (1, rows, D_MODEL) if cfg.has_state else (nseq, cfg.tseq, D_MODEL)
    ins, specs = [], []

    def add(name, arr, spec):
        ins.append((name, arr))
        specs.append(spec)

    def layer_row(name, arr):
        w = arr.shape[-1]
        add(name, arr.reshape(DEPTH, 1, w), pl.BlockSpec((None, 1, w), lambda g: (l, 0, 0)))

    once = pl.Buffered(1)

    def layer_mat(name, arr):
        add(name, arr, pl.BlockSpec((None,) + arr.shape[1:], lambda g: (l, 0, 0), pipeline_mode=once))

    def whole(name, arr):
        add(name, arr, pl.BlockSpec(arr.shape, lambda g: (0,) * arr.ndim, pipeline_mode=once))

    last = nb * nt - 1
    lag = cfg.lag

    def blk_a(g):
        ga = jnp.minimum(g, last)
        return ga // nt, ga % nt

    def blk_b(g):
        gb = jnp.maximum(g - lag, 0)
        return gb // nt, gb % nt

    add("x", x3, pl.BlockSpec(x_block, lambda g: (*blk_a(g), 0)))
    add("xres", x3, pl.BlockSpec(x_block, lambda g: (*blk_b(g), 0)))
    if cfg.has_state:
        st_a, st_b, st_c, shift_rows = state
        add("shift_rows", shift_rows, pl.BlockSpec((None, rows, D_SHIFT), lambda g: (l, blk_a(g)[0], 0)))
        add("sa_in", st_a, pl.BlockSpec((None, nseq, H_A, DK_A, DV_A), lambda g: (l, blk_b(g)[0], 0, 0, 0)))
        add("sb_in", st_b, pl.BlockSpec((None, nseq, H_B, DK_B, DV_B), lambda g: (l, blk_b(g)[0], 0, 0, 0)))
        add("sc_in", st_c, pl.BlockSpec((None, nseq, H_C, N_C, N_C), lambda g: (l, blk_b(g)[0], 0, 0, 0)))
    layer_row("norm_w", params["norm_w"])
    layer_mat("w_in", params["w_in"])
    whole("hgrn_lb", params["hgrn_lb"])
    layer_row("hgrn_norm_w", params["hgrn_norm_w"])
    layer_mat("gk_w2", params["gla_gk_w2"])
    layer_row("gk_b", params["gla_gk_b"])
    layer_row("gla_norm_w", params["gla_norm_w"])
    layer_row("mu", params["rwkv_mu"])
    layer_row("w0", params["rwkv_w0"])
    layer_mat("w2", params["rwkv_w2"])
    layer_row("a0", params["rwkv_a0"])
    layer_mat("a2", params["rwkv_a2"])
    layer_row("k_k", params["rwkv_k_k"])
    layer_row("k_a", params["rwkv_k_a"])
    layer_row("r_k", params["rwkv_r_k"])
    layer_row("ln_w", params["rwkv_ln_w"])
    layer_row("ln_b", params["rwkv_ln_b"])
    layer_mat("w_out", params["w_out"])
    if cfg.final:
        whole("final_norm_w", params["final_norm_w"].reshape(1, D_MODEL))
    whole("m64a", params["m64a"])
    whole("m96", params["m96"])
    whole("m64c", params["m64c"])

    pct_rows = rows if cfg.has_state else nseq * SUBLANES
    state_dims = ((H_A, DK_A, DV_A), (H_B, DK_B, DV_B), (H_C, N_C, N_C))
    aliases = {}
    for k, (name, arr, dims) in enumerate(zip(("sa_all", "sb_all", "sc_all"), new_states or (), state_dims)):
        assert arr.shape == (DEPTH, nseq_total) + dims
        aliases[len(ins)] = 1 + k
        add(name, arr, pl.BlockSpec(memory_space=pl.ANY))
    outs = [("xo", jax.ShapeDtypeStruct(x3.shape, F32), pl.BlockSpec(x_block, lambda g: (*blk_b(g), 0)))]
    outs += [(name, jax.ShapeDtypeStruct((DEPTH, nseq_total) + dims, F32),
              pl.BlockSpec((None, nseq) + dims, lambda g: (l, blk_b(g)[0], 0, 0, 0)))
             for name, dims in zip(("sa", "sb", "sc"), state_dims)]
    outs += [("pct", jax.ShapeDtypeStruct((nb * pct_rows, D_SHIFT), F32),
              pl.BlockSpec((pct_rows, D_SHIFT), lambda g: (blk_a(g)[0], 0)))]
    a_w, b_w, c = H_A * DK_A, H_B * DK_B, cfg.chunk
    nch = rows // c
    scratch = [("p", (rows, D_IN), F32), ("xs", (rows, D_SHIFT), F32), ("o", (rows, D_MODEL), F32),
               ("oi", (rows, D_MODEL), F32),
               ("qa", (rows, a_w), F32), ("ka", (rows, a_w), F32), ("qb", (rows, b_w), F32),
               ("rc", (rows, W_C), F32), ("vc", (rows, W_C), F32), ("kc", (rows, W_C), F32),
               ("u0", (rows, W_C), F32), ("u", (rows, W_C), F32),
               ("kta", (nch, a_w, c), BF16), ("da", (a_w, nch), F32),
               ("ktb", (nch, b_w, c), BF16), ("db", (b_w, nch), F32),
               ("bkt", (nch, W_C, 2 * c), BF16), ("dc", (W_C, nch), F32),
               ("khr", (nch, 2 * c, W_C), F32), ("rb", (H_C, rows, rows), BF16),
               ("sct", (nseq, H_C, N_C, N_C), F32)]
    scratch = ([s for s in scratch if s[0] not in HANDOFF]
               + [(n + slot, shape, dt) for n, shape, dt in scratch if n in HANDOFF for slot in "01"[:1 + lag]])
    names = tuple(n for n, _ in ins) + tuple(n for n, _, _ in outs) + tuple(n for n, _, _ in scratch)
    return pl.pallas_call(
        functools.partial(_layer_kernel, names, cfg, nb, nt),
        grid=(nb * nt + lag,),
        in_specs=specs,
        out_specs=[s for _, _, s in outs],
        out_shape=[s for _, s, _ in outs],
        scratch_shapes=[pltpu.VMEM(shape, dt) for _, shape, dt in scratch],
        input_output_aliases=aliases,
        compiler_params=pltpu.CompilerParams(
            dimension_semantics=("arbitrary",),
            vmem_limit_bytes=VMEM_LIMIT_BYTES),
        name=f"layer{l}_{'sample' if cfg.has_state else 'prompt'}",
    )(*[a for _, a in ins])


_MODEL_SHIFT_ORDER = tuple(n for n in _MODEL_COLS if n in _SHIFT_ORDER)


def _to_kernel_cols(w, dtype):
    return jnp.concatenate([w[..., _MODEL_COLS[n][0]:_MODEL_COLS[n][1]].astype(dtype) for n in _COL_ORDER], axis=-1)


def _shift_to_kernel(s):
    base = _MODEL_COLS[_MODEL_SHIFT_ORDER[0]][0]
    return jnp.concatenate([s[..., _MODEL_COLS[n][0] - base:_MODEL_COLS[n][1] - base] for n in _SHIFT_ORDER], axis=-1)


def _shift_to_model(s):
    return jnp.concatenate([s[..., _SHIFT[n][0]:_SHIFT[n][1]] for n in _MODEL_SHIFT_ORDER], axis=-1)


def _make_params(norm_w, w_in, hgrn_lb, hgrn_norm_w, gla_gk_w2, gla_gk_b, gla_norm_w, rwkv_mu, rwkv_w0, rwkv_w2,
                 rwkv_a0, rwkv_a2, rwkv_k_k, rwkv_k_a, rwkv_r_k, rwkv_ln_w, rwkv_ln_b, w_out, final_norm_w):
    return dict(
        norm_w=norm_w, w_in=_to_kernel_cols(w_in, BF16), hgrn_lb=hgrn_lb, hgrn_norm_w=hgrn_norm_w,
        gla_gk_w2=gla_gk_w2, gla_gk_b=gla_gk_b, gla_norm_w=gla_norm_w, rwkv_mu=_shift_to_kernel(rwkv_mu),
        rwkv_w0=rwkv_w0, rwkv_w2=rwkv_w2, rwkv_a0=rwkv_a0, rwkv_a2=rwkv_a2, rwkv_k_k=rwkv_k_k,
        rwkv_k_a=rwkv_k_a, rwkv_r_k=rwkv_r_k, rwkv_ln_w=rwkv_ln_w, rwkv_ln_b=rwkv_ln_b,
        w_out=w_out.astype(BF16), final_norm_w=final_norm_w,
        m64a=_block_diag_ones(W_A, DV_A), m96=_block_diag_ones(W_B, DV_B),
        m64c=_block_diag_ones(W_C, N_C),
    )


def _trunk(x, state, params, rows, tseq, chunk, lag):
    bsz, tlen, _ = x.shape
    has_state = state is not None
    nseq = rows // tseq
    assert rows % tseq == 0 and tseq % chunk == 0 and bsz % nseq == 0 and tlen % tseq == 0
    nb, nt = bsz // nseq, tlen // tseq
    if has_state:
        assert tlen == tseq
        st_a, st_b, st_c, st_s = state
        shift_rows = jnp.pad(_shift_to_kernel(st_s)[:, :, None, :], ((0, 0), (0, 0), (0, tlen - 1), (0, 0)))
        state = (st_a, st_b, st_c, shift_rows.reshape(DEPTH, bsz * tlen, D_SHIFT))
        x3 = x.reshape(nb, rows, D_MODEL)
    else:
        x3 = x
    new_states = ()
    new_s = []
    for l in range(DEPTH):
        cfg = Cfg(layer=l, rows=rows, tseq=tseq, chunk=chunk, lag=lag, has_state=has_state,
                  final=(l == DEPTH - 1))
        x3, *new_states, pct = _run_layer(cfg, nb, nt, x3, state, tuple(new_states), params)
        if has_state:
            new_s.append(pct.reshape(bsz, tlen, D_SHIFT)[:, tlen - 1])
        else:
            new_s.append(pct.reshape(bsz, SUBLANES, D_SHIFT)[:, SUBLANES - 1])
    return (x3.reshape(bsz, tlen, D_MODEL), *new_states, _shift_to_model(jnp.stack(new_s)))


PROMPT_ROWS = 256
PROMPT_TSEQ = 32
PROMPT_CHUNK = 32
PROMPT_LAG = 1
SAMPLE_ROWS = 64
SAMPLE_LAG = 0


def kernel(x_prompt, x_sample, state_hgrn, state_gla, state_rwkv, state_shift, norm_w, w_in, hgrn_lb, hgrn_norm_w, gla_gk_w2, gla_gk_b, gla_norm_w, rwkv_mu, rwkv_w0, rwkv_w2, rwkv_a0, rwkv_a2, rwkv_k_k, rwkv_k_a, rwkv_r_k, rwkv_ln_w, rwkv_ln_b, w_out, final_norm_w):
    params = _make_params(norm_w, w_in, hgrn_lb, hgrn_norm_w, gla_gk_w2, gla_gk_b, gla_norm_w, rwkv_mu, rwkv_w0,
                          rwkv_w2, rwkv_a0, rwkv_a2, rwkv_k_k, rwkv_k_a, rwkv_r_k, rwkv_ln_w, rwkv_ln_b, w_out,
                          final_norm_w)
    y_p, hgrn_p, gla_p, rwkv_p, shift_p = _trunk(x_prompt, None, params, PROMPT_ROWS, PROMPT_TSEQ, PROMPT_CHUNK,
                                                 PROMPT_LAG)
    tok = x_sample.shape[1]
    y_s, hgrn_s, gla_s, rwkv_s, shift_s = _trunk(
        x_sample, (state_hgrn, state_gla, state_rwkv, state_shift), params, SAMPLE_ROWS, tok, tok, SAMPLE_LAG)
    return (y_p, y_s, hgrn_p, gla_p, rwkv_p, shift_p, hgrn_s, gla_s, rwkv_s, shift_s)
```
